```python
import jax
import jax.numpy as jnp
from jax import lax
import numpy as np

D_MODEL = 1024
BATCH = 4
SEQ = 8192
DEPTH = 2

HEAD_DIM = 64
ROPE_THETA = 10000.0
NORM_EPS = 1e-6
QUERY_BLOCK = 128
N_MEM = 256

SWA_HEADS = 8
SWA_KV_HEADS = 2
SWA_WINDOW = 128

NSA_HEADS = 8
NSA_KV_GROUPS = 2
NSA_CMP_BLOCK = 32
NSA_CMP_STRIDE = 16
NSA_SEL_BLOCK = 64
NSA_N_SEL = 16
NSA_WINDOW = 512
NSA_FORCE_BONUS = 1e4

MLA_HEADS = 8
MLA_Q_RANK = 384
MLA_KV_RANK = 256
MLA_NOPE_DIM = 64
MLA_ROPE_DIM = 32
MLA_V_DIM = 64

XATTN_HEADS = 4
XATTN_HEAD_DIM = 128

D_FF = -(-8 * D_MODEL // (3 * 256)) * 256

N_BRANCH = 3
IN_SPLITS = (
    SWA_HEADS * HEAD_DIM, SWA_KV_HEADS * HEAD_DIM, SWA_KV_HEADS * HEAD_DIM,
    NSA_HEADS * HEAD_DIM,
    NSA_KV_GROUPS * HEAD_DIM, NSA_KV_GROUPS * HEAD_DIM,
    NSA_KV_GROUPS * HEAD_DIM, NSA_KV_GROUPS * HEAD_DIM,
    NSA_KV_GROUPS * HEAD_DIM, NSA_KV_GROUPS * HEAD_DIM,
    NSA_HEADS * 3,
    MLA_Q_RANK, MLA_KV_RANK, MLA_ROPE_DIM,
    N_BRANCH * D_MODEL,
)
D_IN = sum(IN_SPLITS)

kernel_name = 'hybrid_swa_nsa_mla_block'


def rmsnorm(x, g):
    x32 = x.astype(jnp.float32)
    y = x32 * lax.rsqrt(jnp.mean(x32 * x32, axis=-1, keepdims=True) + NORM_EPS)
    return (y * g.astype(jnp.float32)).astype(x.dtype)


def rope(x, pos):
    half = x.shape[-1] // 2
    inv_freq = ROPE_THETA ** (-jnp.arange(half, dtype=jnp.float32) / half)
    ang = pos.astype(jnp.float32)[..., None] * inv_freq
    cos, sin = jnp.cos(ang)[:, :, None, :], jnp.sin(ang)[:, :, None, :]
    x1, x2 = jnp.split(x.astype(jnp.float32), 2, axis=-1)
    return jnp.concatenate([x1 * cos - x2 * sin, x2 * cos + x1 * sin], axis=-1).astype(x.dtype)


def masked_probs(s, mask, sink=None):
    s = jnp.where(mask, s, -jnp.inf)
    m = jnp.max(s, axis=-1, keepdims=True)
    if sink is not None:
        m = jnp.maximum(m, sink)
    m = jnp.where(jnp.isfinite(m), m, 0.0)
    e = jnp.exp(s - m)
    denom = jnp.sum(e, axis=-1, keepdims=True)
    if sink is not None:
        denom = denom + jnp.exp(sink - m)
    return e / jnp.where(denom > 0, denom, 1.0)


def slice_seq(x, start, size):
    return lax.dynamic_slice_in_dim(x, start, size, axis=1)


def pad_front(x, n):
    return jnp.pad(x, ((0, 0), (n, 0), (0, 0), (0, 0)))


def blocked_map(fn, seq_len):
    out = lax.map(fn, jnp.arange(seq_len // QUERY_BLOCK) * QUERY_BLOCK)
    out = jnp.moveaxis(out, 0, 1)
    return out.reshape(out.shape[0], seq_len, *out.shape[3:])


def window_block(q, k_pad, v_pad, qs, window, sink=None):
    B, _, H, d = q.shape
    G = k_pad.shape[2]
    R = H // G
    L = window + QUERY_BLOCK
    qb = slice_seq(q, qs, QUERY_BLOCK).reshape(B, QUERY_BLOCK, G, R, d)
    kb = slice_seq(k_pad, qs, L)
    vb = slice_seq(v_pad, qs, L)
    t = qs + jnp.arange(QUERY_BLOCK)[:, None]
    kpos = qs - window + jnp.arange(L)[None, :]
    mask = (kpos <= t) & (kpos > t - window) & (kpos >= 0)
    s = jnp.einsum('bqgrd,bkgd->bgrqk', qb, kb).astype(jnp.float32) * (d ** -0.5)
    p = masked_probs(s, mask, sink)
    o = jnp.einsum('bgrqk,bkgd->bqgrd', p.astype(vb.dtype), vb)
    return o.reshape(B, QUERY_BLOCK, H, d)


def swa_mixer(q, k, v, pos, sinks):
    B, S, _ = q.shape
    q = rope(q.reshape(B, S, SWA_HEADS, HEAD_DIM), pos)
    k = pad_front(rope(k.reshape(B, S, SWA_KV_HEADS, HEAD_DIM), pos), SWA_WINDOW)
    v = pad_front(v.reshape(B, S, SWA_KV_HEADS, HEAD_DIM), SWA_WINDOW)
    sink = sinks.astype(jnp.float32).reshape(1, SWA_KV_HEADS, SWA_HEADS // SWA_KV_HEADS, 1, 1)
    o = blocked_map(lambda qs: window_block(q, k, v, qs, SWA_WINDOW, sink), S)
    return o.reshape(B, S, SWA_HEADS * HEAD_DIM)


def nsa_compress(k, pe, w1, w2):
    B, S, G, d = k.shape
    n_cmp = (S - NSA_CMP_BLOCK) // NSA_CMP_STRIDE + 1
    idx = np.arange(n_cmp)[:, None] * NSA_CMP_STRIDE + np.arange(NSA_CMP_BLOCK)[None, :]
    blk = k[:, idx] + pe[None, None, :, None, :]
    blk = jnp.swapaxes(blk, 2, 3).reshape(B, n_cmp, G, NSA_CMP_BLOCK * d)
    return jax.nn.silu(blk @ w1) @ w2


def nsa_mixer(q, kc, vc, ks, vs, kw, vw, g_logits, pos, pe_k, pe_v, wk1, wk2, wv1, wv2):
    B, S, _ = q.shape
    H, G, d = NSA_HEADS, NSA_KV_GROUPS, HEAD_DIM
    R = H // G
    shp = (B, S, G, d)
    q = q.reshape(B, S, H, d)
    q_rot = rope(q, pos)
    k_cmp = nsa_compress(kc.reshape(shp), pe_k, wk1, wk2)
    v_cmp = nsa_compress(vc.reshape(shp), pe_v, wv1, wv2)
    n_cmp = k_cmp.shape[1]
    cmp_start = np.arange(n_cmp) * NSA_CMP_STRIDE
    cmp_end = jnp.asarray(cmp_start + NSA_CMP_BLOCK - 1)
    n_slc = S // NSA_SEL_BLOCK
    sel_start = np.arange(n_slc) * NSA_SEL_BLOCK
    cover = (np.minimum(cmp_start[:, None] + NSA_CMP_BLOCK, sel_start[None, :] + NSA_SEL_BLOCK)
             - np.maximum(cmp_start[:, None], sel_start[None, :]))
    cmp_to_sel = jnp.asarray(np.clip(cover, 0, None) / NSA_CMP_STRIDE, dtype=jnp.float32)
    n_sel = min(NSA_N_SEL, n_slc)
    to_blocks = lambda a: jnp.transpose(a.reshape(B, n_slc, NSA_SEL_BLOCK, G, d), (0, 3, 1, 2, 4))
    k_sel_blk = to_blocks(rope(ks.reshape(shp), pos))
    v_sel_blk = to_blocks(vs.reshape(shp))
    k_win = pad_front(rope(kw.reshape(shp), pos), NSA_WINDOW)
    v_win = pad_front(vw.reshape(shp), NSA_WINDOW)
    gate = jax.nn.sigmoid(g_logits.astype(jnp.float32)).reshape(B, S, H, 3)
    gather = jax.vmap(jax.vmap(lambda blocks, ids: blocks[ids]))
    scale = d ** -0.5
    blk_ids = jnp.arange(n_slc)[None, :]
    m_sel = n_sel * NSA_SEL_BLOCK

    def block(qs):
        t = qs + jnp.arange(QUERY_BLOCK)
        qb = slice_seq(q, qs, QUERY_BLOCK).reshape(B, QUERY_BLOCK, G, R, d)
        s = jnp.einsum('bqgrd,bcgd->bgrqc', qb, k_cmp).astype(jnp.float32) * scale
        p_cmp = masked_probs(s, cmp_end[None, :] <= t[:, None])
        o_cmp = jnp.einsum('bgrqc,bcgd->bqgrd', p_cmp.astype(v_cmp.dtype), v_cmp)
        imp = jnp.einsum('bgqc,cj->bgqj', p_cmp.sum(axis=2), cmp_to_sel)
        cur = (t // NSA_SEL_BLOCK)[:, None]
        valid = blk_ids <= cur
        forced = (blk_ids == 0) | (blk_ids == cur) | (blk_ids == cur - 1)
        score = jnp.where(valid, imp + jnp.where(forced, NSA_FORCE_BONUS, 0.0), -jnp.inf)
        top_val, top_idx = lax.top_k(score, n_sel)
        k_g = gather(k_sel_blk, top_idx).reshape(B, G, QUERY_BLOCK, m_sel, d)
        v_g = gather(v_sel_blk, top_idx).reshape(B, G, QUERY_BLOCK, m_sel, d)
        tok = (top_idx[..., None] * NSA_SEL_BLOCK + jnp.arange(NSA_SEL_BLOCK)).reshape(B, G, QUERY_BLOCK, m_sel)
        ok = (tok <= t[:, None]) & jnp.repeat(jnp.isfinite(top_val), NSA_SEL_BLOCK, axis=-1)
        qrb = slice_seq(q_rot, qs, QUERY_BLOCK).reshape(B, QUERY_BLOCK, G, R, d)
        s = jnp.einsum('bqgrd,bgqmd->bgrqm', qrb, k_g).astype(jnp.float32) * scale
        p_sel = masked_probs(s, ok[:, :, None])
        o_sel = jnp.einsum('bgrqm,bgqmd->bqgrd', p_sel.astype(v_g.dtype), v_g)
        o_win = window_block(q_rot, k_win, v_win, qs, NSA_WINDOW)
        g = slice_seq(gate, qs, QUERY_BLOCK).astype(q.dtype)
        return (g[..., 0:1] * o_cmp.reshape(B, QUERY_BLOCK, H, d)
                + g[..., 1:2] * o_sel.reshape(B, QUERY_BLOCK, H, d)
                + g[..., 2:3] * o_win)

    return blocked_map(block, S).reshape(B, S, H * d)


def mla_mixer(q_lat, kv_lat, k_rot, pos, q_norm, w_q_b, kv_norm, w_kv_b):
    B, S, _ = q_lat.shape
    H = MLA_HEADS
    q = (rmsnorm(q_lat, q_norm) @ w_q_b).reshape(B, S, H, MLA_NOPE_DIM + MLA_ROPE_DIM)
    q_nope, q_pe = q[..., :MLA_NOPE_DIM], rope(q[..., MLA_NOPE_DIM:], pos)
    kv = (rmsnorm(kv_lat, kv_norm) @ w_kv_b).reshape(B, S, H, MLA_NOPE_DIM + MLA_V_DIM)
    k_nope, v = kv[..., :MLA_NOPE_DIM], kv[..., MLA_NOPE_DIM:]
    k_pe = rope(k_rot[:, :, None, :], pos)[:, :, 0]
    scale = (MLA_NOPE_DIM + MLA_ROPE_DIM) ** -0.5
    kpos = jnp.arange(S)[None, :]

    def block(qs):
        t = qs + jnp.arange(QUERY_BLOCK)[:, None]
        qn = slice_seq(q_nope, qs, QUERY_BLOCK)
        qp = slice_seq(q_pe, qs, QUERY_BLOCK)
        s = (jnp.einsum('bqhd,bkhd->bhqk', qn, k_nope)
             + jnp.einsum('bqhd,bkd->bhqk', qp, k_pe)).astype(jnp.float32) * scale
        p = masked_probs(s, kpos <= t)
        return jnp.einsum('bhqk,bkhd->bqhd', p.astype(v.dtype), v)

    return blocked_map(block, S).reshape(B, S, H * MLA_V_DIM)


def cross_attn(hx, hm, w_q, w_kv, w_o):
    B, S, _ = hx.shape
    M = hm.shape[1]
    q = (hx @ w_q).reshape(B, S, XATTN_HEADS, XATTN_HEAD_DIM)
    kv = (hm @ w_kv).reshape(B, M, 2, XATTN_HEADS, XATTN_HEAD_DIM)
    k, v = kv[:, :, 0], kv[:, :, 1]
    s = jnp.einsum('bshd,bmhd->bhsm', q, k).astype(jnp.float32) * (XATTN_HEAD_DIM ** -0.5)
    p = jax.nn.softmax(s, axis=-1)
    o = jnp.einsum('bhsm,bmhd->bshd', p.astype(v.dtype), v)
    return o.reshape(B, S, XATTN_HEADS * XATTN_HEAD_DIM) @ w_o


def setup_inputs(seed: int = 0) -> dict:
    key = jax.random.key(seed)
    ks = iter(jax.random.split(key, 32))
    f32 = jnp.float32
    L = DEPTH

    def dense(shape, fan_in):
        return jax.random.normal(next(ks), shape, f32) * fan_in ** -0.5

    def gain(shape):
        return 1.0 + 0.1 * jax.random.normal(next(ks), shape, f32)

    x = jax.random.normal(next(ks), (BATCH, SEQ, D_MODEL), f32)
    mem = jax.random.normal(next(ks), (BATCH, N_MEM, D_MODEL), f32)
    positions = (jnp.arange(SEQ, dtype=jnp.int32)[None, :]
                 + jax.random.randint(next(ks), (BATCH, 1), 0, 4096, dtype=jnp.int32))
    cmp_in = NSA_CMP_BLOCK * HEAD_DIM
    return {
        'x': x,
        'mem': mem,
        'positions': positions,
        'norm_mix': gain((L, D_MODEL)),
        'w_in': dense((L, D_MODEL, D_IN), D_MODEL),
        'swa_sinks': jax.random.normal(next(ks), (L, SWA_HEADS), f32),
        'nsa_pe_k': 0.1 * jax.random.normal(next(ks), (L, NSA_CMP_BLOCK, HEAD_DIM), f32),
        'nsa_pe_v': 0.1 * jax.random.normal(next(ks), (L, NSA_CMP_BLOCK, HEAD_DIM), f32),
        'nsa_wk1': dense((L, cmp_in, HEAD_DIM), cmp_in),
        'nsa_wk2': dense((L, HEAD_DIM, HEAD_DIM), HEAD_DIM),
        'nsa_wv1': dense((L, cmp_in, HEAD_DIM), cmp_in),
        'nsa_wv2': dense((L, HEAD_DIM, HEAD_DIM), HEAD_DIM),
        'mla_q_norm': gain((L, MLA_Q_RANK)),
        'mla_w_q_b': dense((L, MLA_Q_RANK, MLA_HEADS * (MLA_NOPE_DIM + MLA_ROPE_DIM)), MLA_Q_RANK),
        'mla_kv_norm': gain((L, MLA_KV_RANK)),
        'mla_w_kv_b': dense((L, MLA_KV_RANK, MLA_HEADS * (MLA_NOPE_DIM + MLA_V_DIM)), MLA_KV_RANK),
        'w_br_a': dense((L, SWA_HEADS * HEAD_DIM, D_MODEL), SWA_HEADS * HEAD_DIM),
        'w_br_b': dense((L, NSA_HEADS * HEAD_DIM, D_MODEL), NSA_HEADS * HEAD_DIM),
        'w_br_c': dense((L, MLA_HEADS * MLA_V_DIM, D_MODEL), MLA_HEADS * MLA_V_DIM),
        'w_out': dense((L, D_MODEL, D_MODEL), D_MODEL),
        'norm_xattn': gain((L, D_MODEL)),
        'norm_mem': gain((L, D_MODEL)),
        'w_xq': dense((L, D_MODEL, XATTN_HEADS * XATTN_HEAD_DIM), D_MODEL),
        'w_xkv': dense((L, D_MODEL, 2 * XATTN_HEADS * XATTN_HEAD_DIM), D_MODEL),
        'w_xo': dense((L, XATTN_HEADS * XATTN_HEAD_DIM, D_MODEL), XATTN_HEADS * XATTN_HEAD_DIM),
        'norm_ffn': gain((L, D_MODEL)),
        'w_gate_up': dense((L, D_MODEL, 2 * D_FF), D_MODEL),
        'w_down': dense((L, D_FF, D_MODEL), D_FF),
        'norm_final': gain((D_MODEL,)),
    }


def reference(x, mem, positions, norm_mix, w_in, swa_sinks, nsa_pe_k, nsa_pe_v, nsa_wk1, nsa_wk2,
              nsa_wv1, nsa_wv2, mla_q_norm, mla_w_q_b, mla_kv_norm, mla_w_kv_b, w_br_a, w_br_b,
              w_br_c, w_out, norm_xattn, norm_mem, w_xq, w_xkv, w_xo, norm_ffn, w_gate_up, w_down,
              norm_final):
    B, S, D = x.shape
    offsets = np.cumsum(IN_SPLITS)[:-1].tolist()
    for l in range(DEPTH):
        h = rmsnorm(x, norm_mix[l])
        (a_q, a_k, a_v, b_q, b_kc, b_vc, b_ks, b_vs, b_kw, b_vw, b_g,
         c_qa, c_kv, c_kr, g_br) = jnp.split(h @ w_in[l], offsets, axis=-1)
        o_a = swa_mixer(a_q, a_k, a_v, positions, swa_sinks[l])
        o_b = nsa_mixer(b_q, b_kc, b_vc, b_ks, b_vs, b_kw, b_vw, b_g, positions,
                        nsa_pe_k[l], nsa_pe_v[l], nsa_wk1[l], nsa_wk2[l], nsa_wv1[l], nsa_wv2[l])
        o_c = mla_mixer(c_qa, c_kv, c_kr, positions, mla_q_norm[l], mla_w_q_b[l],
                        mla_kv_norm[l], mla_w_kv_b[l])
        g = jax.nn.sigmoid(g_br.astype(jnp.float32)).reshape(B, S, N_BRANCH, D).astype(x.dtype)
        merged = (g[:, :, 0] * (o_a @ w_br_a[l])
                  + g[:, :, 1] * (o_b @ w_br_b[l])
                  + g[:, :, 2] * (o_c @ w_br_c[l]))
        x = x + merged @ w_out[l]
        x = x + cross_attn(rmsnorm(x, norm_xattn[l]), rmsnorm(mem, norm_mem[l]),
                           w_xq[l], w_xkv[l], w_xo[l])
        h = rmsnorm(x, norm_ffn[l])
        gate_h, up_h = jnp.split(h @ w_gate_up[l], 2, axis=-1)
        x = x + (jax.nn.silu(gate_h) * up_h) @ w_down[l]
    return rmsnorm(x, norm_final)
```

```python
import functools

import numpy as np
import jax
import jax.numpy as jnp
from jax import lax
from jax.experimental import pallas as pl
from jax.experimental.pallas import tpu as pltpu

HEAD_DIM = 64
ROPE_THETA = 10000.0
NORM_EPS = 1e-6
SWA_HEADS = 8
SWA_KV_HEADS = 2
SWA_WINDOW = 128
NSA_HEADS = 8
NSA_KV_GROUPS = 2
NSA_CMP_BLOCK = 32
NSA_CMP_STRIDE = 16
NSA_SEL_BLOCK = 64
NSA_N_SEL = 16
NSA_WINDOW = 512
NSA_FORCE_BONUS = 1e4
MLA_HEADS = 8
MLA_Q_RANK = 384
MLA_KV_RANK = 256
MLA_NOPE_DIM = 64
MLA_ROPE_DIM = 32
MLA_V_DIM = 64
MLA_QK_DIM = MLA_NOPE_DIM + MLA_ROPE_DIM
XATTN_HEADS = 4
XATTN_HEAD_DIM = 128
N_BRANCH = 3
GROUP_HEADS = 4
GROUP_ROWS = GROUP_HEADS * HEAD_DIM

V7X_VMEM_LIMIT_BYTES = 56 * 1024 * 1024
LANES = 128

CDT = jnp.bfloat16

MASK_VALUE = -1e30

TM_PROJ = 512
TQ_SWA = 256
TQ_NSA = 128
TK_SEL = 512
TQ_MLA = 256
TK_MLA = 512
FF_CHUNK = 256

R_AQ, R_BQ, R_K, R_V, R_C, R_CQA, R_CKV, R_CKR = 0, 512, 1024, 1408, 1792, 2048, 2432, 2688
N_MIX = 2752
O_AQ, O_BQ, O_BQR, O_KK, O_VV, O_QM, O_KV, O_KPE = 0, 512, 1024, 1536, 1920, 2304, 3072, 4096
N_OUT1 = 4128
P_C, P_GATE = 0, 256
N_OUT2 = 280


def _dot(a, b):
    return jnp.dot(a, b, preferred_element_type=jnp.float32)


def _dot_nt(a, b):
    return lax.dot_general(a, b, (((1,), (1,)), ((), ())), preferred_element_type=jnp.float32)


def _opts(name, *sem):
    return dict(name=name, compiler_params=pltpu.CompilerParams(
        dimension_semantics=sem, vmem_limit_bytes=V7X_VMEM_LIMIT_BYTES))


def _rms_fm(x, g_col):
    ms = jnp.mean(x * x, axis=0, keepdims=True)
    return x * lax.rsqrt(ms + NORM_EPS) * g_col


def _const_spec(shape):
    nd = len(shape)
    return pl.BlockSpec(shape, lambda *_: (0,) * nd)


def _to_feature_major_kernel(x_ref, o_ref):
    o_ref[0] = x_ref[0].T


def _to_feature_major(x, tm):
    B, S, D = x.shape
    return pl.pallas_call(
        _to_feature_major_kernel,
        grid=(B, S // tm),
        in_specs=[pl.BlockSpec((1, tm, D), lambda b, i: (b, i, 0))],
        out_specs=pl.BlockSpec((1, D, tm), lambda b, i: (b, 0, i)),
        out_shape=jax.ShapeDtypeStruct((B, D, S), x.dtype),
        **_opts("to_feature_major", "parallel", "parallel"),
    )(x)


def _final_norm_kernel(x_ref, g_ref, o_ref):
    o_ref[0] = _rms_fm(x_ref[0], g_ref[...]).T


def _final_norm(xT, g_col, tm):
    B, D, S = xT.shape
    return pl.pallas_call(
        _final_norm_kernel,
        grid=(B, S // tm),
        in_specs=[pl.BlockSpec((1, D, tm), lambda b, i: (b, 0, i)), _const_spec((D, 1))],
        out_specs=pl.BlockSpec((1, tm, D), lambda b, i: (b, i, 0)),
        out_shape=jax.ShapeDtypeStruct((B, S, D), xT.dtype),
        **_opts("final_norm", "parallel", "parallel"),
    )(xT, g_col)


def _rope_store(o_ref, row0, y, cos, sin, n_heads, head_dim, scale):
    half = head_dim // 2
    for h in range(n_heads):
        x1 = y[h * head_dim:h * head_dim + half]
        x2 = y[h * head_dim + half:(h + 1) * head_dim]
        r = row0 + h * head_dim
        o_ref[0, r:r + half, :] = ((x1 * cos - x2 * sin) * scale).astype(o_ref.dtype)
        o_ref[0, r + half:r + head_dim, :] = ((x2 * cos + x1 * sin) * scale).astype(o_ref.dtype)


def _proj_kernel(x_ref, g_ref, wm_ref, cos_ref, sin_ref, cosm_ref, sinm_ref,
                 qn_ref, wqb_ref, kvn_ref, wkvb_ref, o1_ref, o2_ref):
    h = _rms_fm(x_ref[0], g_ref[...]).astype(CDT)
    cos, sin = cos_ref[0], sin_ref[0]
    cosm, sinm = cosm_ref[0], sinm_ref[0]
    qk_scale = HEAD_DIM ** -0.5

    y = _dot(wm_ref[R_AQ:R_AQ + 512, :], h)
    _rope_store(o1_ref, O_AQ, y, cos, sin, SWA_HEADS, HEAD_DIM, qk_scale)
    y = _dot(wm_ref[R_BQ:R_BQ + 512, :], h)
    o1_ref[0, O_BQ:O_BQ + 512, :] = (y * qk_scale).astype(o1_ref.dtype)
    _rope_store(o1_ref, O_BQR, y, cos, sin, NSA_HEADS, HEAD_DIM, qk_scale)
    y = _dot(wm_ref[R_K:R_K + 384, :], h)
    _rope_store(o1_ref, O_KK, y, cos, sin, 6, HEAD_DIM, 1.0)
    y = _dot(wm_ref[R_V:R_V + 384, :], h)
    o1_ref[0, O_VV:O_VV + 384, :] = y.astype(o1_ref.dtype)
    o2_ref[0, P_C:P_C + 256, :] = _dot(wm_ref[R_C:R_C + 256, :], h)

    lat = _dot(wm_ref[R_CQA:R_CQA + MLA_Q_RANK, :], h)
    qm = _dot(wqb_ref[...], _rms_fm(lat, qn_ref[...]).astype(CDT))
    m_scale = MLA_QK_DIM ** -0.5
    hr = MLA_ROPE_DIM // 2
    for hd in range(MLA_HEADS):
        r = hd * MLA_QK_DIM
        o1_ref[0, O_QM + r:O_QM + r + MLA_NOPE_DIM, :] = (
            qm[r:r + MLA_NOPE_DIM] * m_scale).astype(o1_ref.dtype)
        x1 = qm[r + MLA_NOPE_DIM:r + MLA_NOPE_DIM + hr]
        x2 = qm[r + MLA_NOPE_DIM + hr:r + MLA_QK_DIM]
        o1_ref[0, O_QM + r + MLA_NOPE_DIM:O_QM + r + MLA_NOPE_DIM + hr, :] = (
            (x1 * cosm - x2 * sinm) * m_scale).astype(o1_ref.dtype)
        o1_ref[0, O_QM + r + MLA_NOPE_DIM + hr:O_QM + r + MLA_QK_DIM, :] = (
            (x2 * cosm + x1 * sinm) * m_scale).astype(o1_ref.dtype)
    lat = _dot(wm_ref[R_CKV:R_CKV + MLA_KV_RANK, :], h)
    kv = _dot(wkvb_ref[...], _rms_fm(lat, kvn_ref[...]).astype(CDT))
    o1_ref[0, O_KV:O_KV + 1024, :] = kv.astype(o1_ref.dtype)
    y = _dot(wm_ref[R_CKR:R_CKR + 64, :], h)
    x1, x2 = y[0:hr], y[hr:2 * hr]
    o1_ref[0, O_KPE:O_KPE + hr, :] = (x1 * cosm - x2 * sinm).astype(o1_ref.dtype)
    o1_ref[0, O_KPE + hr:O_KPE + 2 * hr, :] = (x2 * cosm + x1 * sinm).astype(o1_ref.dtype)
    o2_ref[0, P_GATE:P_GATE + 24, :] = jax.nn.sigmoid(y[MLA_ROPE_DIM:MLA_ROPE_DIM + 24])


def _proj(xT, g_col, wmT, cos, sin, cosm, sinm, qn, wqbT, kvn, wkvbT, tm):
    B, D, S = xT.shape
    tok = lambda rows: pl.BlockSpec((1, rows, tm), lambda b, i: (b, 0, i))
    return pl.pallas_call(
        _proj_kernel,
        grid=(B, S // tm),
        in_specs=[tok(D), _const_spec((D, 1)), _const_spec(wmT.shape),
                  tok(32), tok(32), tok(16), tok(16),
                  _const_spec(qn.shape), _const_spec(wqbT.shape),
                  _const_spec(kvn.shape), _const_spec(wkvbT.shape)],
        out_specs=[tok(N_OUT1), tok(N_OUT2)],
        out_shape=[jax.ShapeDtypeStruct((B, N_OUT1, S), CDT),
                   jax.ShapeDtypeStruct((B, N_OUT2, S), jnp.float32)],
        **_opts("mixer_proj", "parallel", "parallel"),
    )(xT, g_col, wmT, cos, sin, cosm, sinm, qn, wqbT, kvn, wkvbT)


def _compress_kernel(kr_ref, pe_ref, w1_ref, w2_ref, w2t_ref, tok_ref, fm_ref):
    kr = kr_ref[0, 0]
    nc = kr.shape[0]
    nj = nc // 4
    half = kr.shape[1]
    a = _dot((kr + pe_ref[0, 0:1, :]).astype(CDT), w1_ref[0, 0:half, :])
    bm = _dot((kr + pe_ref[0, 1:2, :]).astype(CDT), w1_ref[0, half:2 * half, :])
    wrap = pltpu.roll(bm[0:nj], nj - 1, 0)
    row = lax.broadcasted_iota(jnp.int32, (nj, 1), 0)
    wrap = jnp.where(row == nj - 1, 0.0, wrap)
    pre = a + jnp.concatenate([bm[nj:], wrap], axis=0)
    hid = (pre * jax.nn.sigmoid(pre)).astype(CDT)
    tok_ref[0, 0] = _dot(hid, w2_ref[0]).astype(tok_ref.dtype)
    fm_ref[0, 0] = _dot_nt(w2t_ref[0], hid).astype(fm_ref.dtype)


def _compress(kr, pe, w1, w2, w2t):
    B, _, NC, W = kr.shape
    return pl.pallas_call(
        _compress_kernel,
        grid=(B, 4),
        in_specs=[pl.BlockSpec((1, 1, NC, W), lambda b, n: (b, n, 0, 0)),
                  pl.BlockSpec((1, 2, W), lambda b, n: (n // 2, 0, 0)),
                  pl.BlockSpec((1, 2 * W, HEAD_DIM), lambda b, n: (n // 2, 0, 0)),
                  pl.BlockSpec((1, HEAD_DIM, HEAD_DIM), lambda b, n: (n // 2, 0, 0)),
                  pl.BlockSpec((1, HEAD_DIM, HEAD_DIM), lambda b, n: (n // 2, 0, 0))],
        out_specs=[pl.BlockSpec((1, 1, NC, HEAD_DIM), lambda b, n: (b, n, 0, 0)),
                   pl.BlockSpec((1, 1, HEAD_DIM, NC), lambda b, n: (b, n, 0, 0))],
        out_shape=[jax.ShapeDtypeStruct((B, 4, NC, HEAD_DIM), CDT),
                   jax.ShapeDtypeStruct((B, 4, HEAD_DIM, NC), CDT)],
        **_opts("nsa_compress", "parallel", "parallel"),
    )(kr, pe, w1, w2, w2t)


def _group_queries(q_ref):
    return jnp.concatenate([q_ref[0, r * HEAD_DIM:(r + 1) * HEAD_DIM, :]
                            for r in range(GROUP_HEADS)], axis=1)


def _window_attention(q4, t4, k_ref, vt_ref, qs, tq, window, sink4=None):
    span = window + tq
    k0 = pl.multiple_of(jnp.maximum(qs - window, 0), LANES)
    s = _dot(k_ref[0, 0, pl.ds(k0, span), :], q4)
    kpos = k0 + lax.broadcasted_iota(jnp.int32, (span, 1), 0)
    s = jnp.where((kpos <= t4) & (kpos > t4 - window), s, -jnp.inf)
    m = jnp.max(s, axis=0, keepdims=True)
    if sink4 is not None:
        m = jnp.maximum(m, sink4)
    e = jnp.exp(s - m)
    den = jnp.sum(e, axis=0, keepdims=True)
    if sink4 is not None:
        den = den + jnp.exp(sink4 - m)
    o = _dot(vt_ref[0, :, pl.ds(k0, span)], e.astype(CDT))
    return o * (1.0 / den)


def _swa_kernel(sink_ref, q_ref, k_ref, vt_ref, o_ref, *, tq):
    g = pl.program_id(1)
    qs = pl.program_id(2) * tq
    q4 = _group_queries(q_ref)
    t = qs + lax.broadcasted_iota(jnp.int32, (1, tq), 1)
    t4 = jnp.concatenate([t] * GROUP_HEADS, axis=1)
    sink4 = jnp.concatenate([jnp.full((1, tq), sink_ref[g * GROUP_HEADS + r], jnp.float32)
                             for r in range(GROUP_HEADS)], axis=1)
    o4 = _window_attention(q4, t4, k_ref, vt_ref, qs, tq, SWA_WINDOW, sink4)
    for r in range(GROUP_HEADS):
        o_ref[0, r * HEAD_DIM:(r + 1) * HEAD_DIM, :] = o4[:, r * tq:(r + 1) * tq].astype(o_ref.dtype)


def _swa(sinks, o1, kk, tq):
    B, _, S = o1.shape
    return pl.pallas_call(
        functools.partial(_swa_kernel, tq=tq),
        grid=(B, SWA_KV_HEADS, S // tq),
        in_specs=[pl.BlockSpec(memory_space=pltpu.SMEM),
                  pl.BlockSpec((1, GROUP_ROWS, tq), lambda b, g, i: (b, O_AQ // GROUP_ROWS + g, i)),
                  pl.BlockSpec((1, 1, S, HEAD_DIM), lambda b, g, i: (b, g, 0, 0)),
                  pl.BlockSpec((1, HEAD_DIM, S), lambda b, g, i: (b, O_VV // HEAD_DIM + g, 0))],
        out_specs=pl.BlockSpec((1, GROUP_ROWS, tq), lambda b, g, i: (b, g, i)),
        out_shape=jax.ShapeDtypeStruct((B, SWA_HEADS * HEAD_DIM, S), CDT),
        **_opts("swa_attn", "parallel", "parallel", "parallel"),
    )(sinks, o1, kk, o1)


def _nsa_kernel(q_ref, qr_ref, cend_ref, kc_ref, vct_ref, ks_ref, vst_ref, kw_ref, vwt_ref,
                gate_ref, o_ref, bias_ref, *, tq, n_sel):
    qs = pl.program_id(2) * tq
    nc = kc_ref.shape[2]
    nj = nc // 4
    q4 = _group_queries(q_ref)
    qr4 = _group_queries(qr_ref)
    t = qs + lax.broadcasted_iota(jnp.int32, (1, tq), 1)
    t4 = jnp.concatenate([t] * GROUP_HEADS, axis=1)

    s = _dot(kc_ref[0, 0], q4)
    s = jnp.where(cend_ref[...] <= t4, s, -jnp.inf)
    m = jnp.max(s, axis=0, keepdims=True)
    m = jnp.where(m == -jnp.inf, 0.0, m)
    e = jnp.exp(s - m)
    den = jnp.sum(e, axis=0, keepdims=True)
    p = e * (1.0 / jnp.where(den > 0, den, 1.0))
    o_cmp = _dot(vct_ref[0, 0], p.astype(CDT))

    psum = p[:, 0:tq]
    for r in range(1, GROUP_HEADS):
        psum = psum + p[:, r * tq:(r + 1) * tq]
    p3 = psum[3 * nj:4 * nj]
    jrow = lax.broadcasted_iota(jnp.int32, (nj, tq), 0)
    prev = jnp.where(jrow == 0, 0.0, pltpu.roll(p3, 1, 0))
    imp = prev + 2.0 * (psum[0:nj] + psum[nj:2 * nj] + psum[2 * nj:3 * nj]) + p3

    jf = jrow.astype(jnp.float32)
    cur = lax.shift_right_arithmetic(t, int(np.log2(NSA_SEL_BLOCK)))
    forced = (jrow == 0) | (jrow == cur) | (jrow == cur - 1)
    score = jnp.where(jrow <= cur, imp + jnp.where(forced, NSA_FORCE_BONUS, 0.0), -jnp.inf)
    bias = jnp.full((nj, tq), MASK_VALUE, jnp.float32)
    for _ in range(n_sel):
        mx = jnp.max(score, axis=0, keepdims=True)
        cand = jnp.where((score == mx) & (mx > -jnp.inf), jf, float(nj))
        pick = jf == jnp.min(cand, axis=0, keepdims=True)
        bias = jnp.where(pick, 0.0, bias)
        score = jnp.where(pick, -jnp.inf, score)
    bias_ref[...] = bias

    blocks_per_tile = TK_SEL // NSA_SEL_BLOCK
    n_tiles = (qs + tq - 1) // TK_SEL + 1

    def sel_body(kt, carry):
        m_run, l_run, acc = carry
        k0 = pl.multiple_of(kt * TK_SEL, TK_SEL)
        sc = _dot(ks_ref[0, 0, pl.ds(k0, TK_SEL), :], qr4)
        b = jnp.concatenate(
            [jnp.broadcast_to(bias_ref[pl.ds(kt * blocks_per_tile + i, 1), :], (NSA_SEL_BLOCK, tq))
             for i in range(blocks_per_tile)], axis=0)
        kpos = k0 + lax.broadcasted_iota(jnp.int32, (TK_SEL, 1), 0)
        b = jnp.where(kpos <= t, b, MASK_VALUE)
        sc = sc + jnp.concatenate([b] * GROUP_HEADS, axis=1)
        m_new = jnp.maximum(m_run, jnp.max(sc, axis=0, keepdims=True))
        alpha = jnp.exp(m_run - m_new)
        pe = jnp.exp(sc - m_new)
        l_new = alpha * l_run + jnp.sum(pe, axis=0, keepdims=True)
        acc_new = alpha * acc + _dot(vst_ref[0, :, pl.ds(k0, TK_SEL)], pe.astype(CDT))
        return m_new, l_new, acc_new

    init = (jnp.full((1, GROUP_HEADS * tq), MASK_VALUE, jnp.float32),
            jnp.zeros((1, GROUP_HEADS * tq), jnp.float32),
            jnp.zeros((HEAD_DIM, GROUP_HEADS * tq), jnp.float32))
    _, l_fin, acc = lax.fori_loop(0, n_tiles, sel_body, init)
    o_sel = acc * (1.0 / l_fin)

    o_win = _window_attention(qr4, t4, kw_ref, vwt_ref, qs, tq, NSA_WINDOW)

    for r in range(GROUP_HEADS):
        sl = slice(r * tq, (r + 1) * tq)
        g0 = gate_ref[0, 0, 3 * r:3 * r + 1, :]
        g1 = gate_ref[0, 0, 3 * r + 1:3 * r + 2, :]
        g2 = gate_ref[0, 0, 3 * r + 2:3 * r + 3, :]
        out = g0 * o_cmp[:, sl] + g1 * o_sel[:, sl] + g2 * o_win[:, sl]
        o_ref[0, r * HEAD_DIM:(r + 1) * HEAD_DIM, :] = out.astype(o_ref.dtype)


def _nsa(o1, cend, cmp_tok, cmp_fm, kk, gates, tq):
    B, _, S = o1.shape
    nc = cmp_tok.shape[2]
    nj = nc // 4
    n_sel = min(NSA_N_SEL, nj)
    G = NSA_KV_GROUPS
    return pl.pallas_call(
        functools.partial(_nsa_kernel, tq=tq, n_sel=n_sel),
        grid=(B, G, S // tq),
        in_specs=[pl.BlockSpec((1, GROUP_ROWS, tq), lambda b, g, i: (b, O_BQ // GROUP_ROWS + g, i)),
                  pl.BlockSpec((1, GROUP_ROWS, tq), lambda b, g, i: (b, O_BQR // GROUP_ROWS + g, i)),
                  _const_spec((nc, 1)),
                  pl.BlockSpec((1, 1, nc, HEAD_DIM), lambda b, g, i: (b, g, 0, 0)),
                  pl.BlockSpec((1, 1, HEAD_DIM, nc), lambda b, g, i: (b, G + g, 0, 0)),
                  pl.BlockSpec((1, 1, S, HEAD_DIM), lambda b, g, i: (b, 2 + g, 0, 0)),
                  pl.BlockSpec((1, HEAD_DIM, S), lambda b, g, i: (b, O_VV // HEAD_DIM + 2 + g, 0)),
                  pl.BlockSpec((1, 1, S, HEAD_DIM), lambda b, g, i: (b, 4 + g, 0, 0)),
                  pl.BlockSpec((1, HEAD_DIM, S), lambda b, g, i: (b, O_VV // HEAD_DIM + 4 + g, 0)),
                  pl.BlockSpec((1, 1, 3 * GROUP_HEADS, tq), lambda b, g, i: (b, g, 0, i))],
        out_specs=pl.BlockSpec((1, GROUP_ROWS, tq), lambda b, g, i: (b, g, i)),
        out_shape=jax.ShapeDtypeStruct((B, NSA_HEADS * HEAD_DIM, S), CDT),
        scratch_shapes=[pltpu.VMEM((nj, tq), jnp.float32)],
        **_opts("nsa_attn", "parallel", "parallel", "arbitrary"),
    )(o1, o1, cend, cmp_tok, cmp_fm, kk, o1, kk, o1, gates)


def _mla_kernel(q_ref, k_ref, vt_ref, o_ref, *, tq):
    qs = pl.program_id(2) * tq
    q = q_ref[0]
    t = qs + lax.broadcasted_iota(jnp.int32, (1, tq), 1)
    n_tiles = (qs + tq - 1) // TK_MLA + 1

    def body(kt, carry):
        m_run, l_run, acc = carry
        k0 = pl.multiple_of(kt * TK_MLA, TK_MLA)
        sc = _dot(k_ref[0, 0, pl.ds(k0, TK_MLA), :], q)
        kpos = k0 + lax.broadcasted_iota(jnp.int32, (TK_MLA, 1), 0)
        sc = jnp.where(kpos <= t, sc, MASK_VALUE)
        m_new = jnp.maximum(m_run, jnp.max(sc, axis=0, keepdims=True))
        alpha = jnp.exp(m_run - m_new)
        pe = jnp.exp(sc - m_new)
        l_new = alpha * l_run + jnp.sum(pe, axis=0, keepdims=True)
        acc_new = alpha * acc + _dot(vt_ref[0, :, pl.ds(k0, TK_MLA)], pe.astype(CDT))
        return m_new, l_new, acc_new

    init = (jnp.full((1, tq), MASK_VALUE, jnp.float32), jnp.zeros((1, tq), jnp.float32),
            jnp.zeros((MLA_V_DIM, tq), jnp.float32))
    _, l_fin, acc = lax.fori_loop(0, n_tiles, body, init)
    o_ref[0] = (acc * (1.0 / l_fin)).astype(o_ref.dtype)


def _mla(o1, kcat, tq):
    B, _, S = o1.shape
    return pl.pallas_call(
        functools.partial(_mla_kernel, tq=tq),
        grid=(B, MLA_HEADS, S // tq),
        in_specs=[pl.BlockSpec((1, MLA_QK_DIM, tq), lambda b, h, i: (b, O_QM // MLA_QK_DIM + h, i)),
                  pl.BlockSpec((1, 1, S, MLA_QK_DIM), lambda b, h, i: (b, h, 0, 0)),
                  pl.BlockSpec((1, MLA_V_DIM, S), lambda b, h, i: (b, O_KV // MLA_V_DIM + 2 * h + 1, 0))],
        out_specs=pl.BlockSpec((1, MLA_V_DIM, tq), lambda b, h, i: (b, h, i)),
        out_shape=jax.ShapeDtypeStruct((B, MLA_HEADS * MLA_V_DIM, S), CDT),
        **_opts("mla_attn", "parallel", "parallel", "parallel"),
    )(o1, kcat, o1)


def _merge_kernel(x_ref, g_ref, wg_ref, oa_ref, ob_ref, oc_ref, wa_ref, wb_ref, wc_ref, wo_ref, o_ref):
    x = x_ref[0]
    d = x.shape[0]
    h = _rms_fm(x, g_ref[...]).astype(CDT)
    merged = None
    for i, (br_ref, w_ref) in enumerate(((oa_ref, wa_ref), (ob_ref, wb_ref), (oc_ref, wc_ref))):
        gate = jax.nn.sigmoid(_dot(wg_ref[i * d:(i + 1) * d, :], h))
        term = gate * _dot(w_ref[...], br_ref[0])
        merged = term if merged is None else merged + term
    o_ref[0] = x + _dot(wo_ref[...], merged.astype(CDT))


def _merge(xT, g_col, wgT, oa, ob, oc, waT, wbT, wcT, woT, tm):
    B, D, S = xT.shape
    tok = lambda rows: pl.BlockSpec((1, rows, tm), lambda b, i: (b, 0, i))
    return pl.pallas_call(
        _merge_kernel,
        grid=(B, S // tm),
        in_specs=[tok(D), _const_spec((D, 1)), _const_spec(wgT.shape),
                  tok(oa.shape[1]), tok(ob.shape[1]), tok(oc.shape[1]),
                  _const_spec(waT.shape), _const_spec(wbT.shape), _const_spec(wcT.shape),
                  _const_spec(woT.shape)],
        out_specs=tok(D),
        out_shape=jax.ShapeDtypeStruct((B, D, S), xT.dtype),
        **_opts("merge_out", "parallel", "parallel"),
    )(xT, g_col, wgT, oa, ob, oc, waT, wbT, wcT, woT)


def _mem_kv_kernel(mem_ref, g_ref, wk_ref, wvt_ref, k_ref, vt_ref):
    m = mem_ref[0]
    ms = jnp.mean(m * m, axis=-1, keepdims=True)
    hm = (m * lax.rsqrt(ms + NORM_EPS) * g_ref[...]).astype(CDT)
    k_ref[0] = _dot(hm, wk_ref[...]).astype(k_ref.dtype)
    vt_ref[0] = _dot_nt(wvt_ref[...], hm).astype(vt_ref.dtype)


def _mem_kv(mem, g_row, wk, wvT):
    B, M, D = mem.shape
    n = wk.shape[1]
    return pl.pallas_call(
        _mem_kv_kernel,
        grid=(B,),
        in_specs=[pl.BlockSpec((1, M, D), lambda b: (b, 0, 0)), _const_spec((1, D)),
                  _const_spec(wk.shape), _const_spec(wvT.shape)],
        out_specs=[pl.BlockSpec((1, M, n), lambda b: (b, 0, 0)),
                   pl.BlockSpec((1, n, M), lambda b: (b, 0, 0))],
        out_shape=[jax.ShapeDtypeStruct((B, M, n), CDT), jax.ShapeDtypeStruct((B, n, M), CDT)],
        **_opts("mem_kv", "parallel"),
    )(mem, g_row, wk, wvT)


def _xattn_kernel(x_ref, g_ref, wq_ref, k_ref, vt_ref, wo_ref, o_ref):
    x = x_ref[0]
    h = _rms_fm(x, g_ref[...]).astype(CDT)
    q = (_dot(wq_ref[...], h) * (XATTN_HEAD_DIM ** -0.5)).astype(CDT)
    outs = []
    for hd in range(XATTN_HEADS):
        rows = slice(hd * XATTN_HEAD_DIM, (hd + 1) * XATTN_HEAD_DIM)
        s = _dot(k_ref[0, :, rows], q[rows])
        e = jnp.exp(s - jnp.max(s, axis=0, keepdims=True))
        den = jnp.sum(e, axis=0, keepdims=True)
        outs.append((_dot(vt_ref[0, rows, :], e.astype(CDT)) * (1.0 / den)).astype(CDT))
    o_ref[0] = x + _dot(wo_ref[...], jnp.concatenate(outs, axis=0))


def _xattn(xT, g_col, wqT, kmem, vmemT, woT, tm):
    B, D, S = xT.shape
    M, n = kmem.shape[1], kmem.shape[2]
    tok = lambda rows: pl.BlockSpec((1, rows, tm), lambda b, i: (b, 0, i))
    return pl.pallas_call(
        _xattn_kernel,
        grid=(B, S // tm),
        in_specs=[tok(D), _const_spec((D, 1)), _const_spec(wqT.shape),
                  pl.BlockSpec((1, M, n), lambda b, i: (b, 0, 0)),
                  pl.BlockSpec((1, n, M), lambda b, i: (b, 0, 0)),
                  _const_spec(woT.shape)],
        out_specs=tok(D),
        out_shape=jax.ShapeDtypeStruct((B, D, S), xT.dtype),
        **_opts("xattn", "parallel", "parallel"),
    )(xT, g_col, wqT, kmem, vmemT, woT)


def _ffn_kernel(x_ref, g_ref, wgu_ref, wd_ref, o_ref, *, d_ff):
    x = x_ref[0]
    h = _rms_fm(x, g_ref[...]).astype(CDT)
    acc = x
    for c in range(d_ff // FF_CHUNK):
        r = c * FF_CHUNK
        gate = _dot(wgu_ref[r:r + FF_CHUNK, :], h)
        up = _dot(wgu_ref[d_ff + r:d_ff + r + FF_CHUNK, :], h)
        act = (gate * jax.nn.sigmoid(gate) * up).astype(CDT)
        acc = acc + _dot(wd_ref[:, r:r + FF_CHUNK], act)
    o_ref[0] = acc


def _ffn(xT, g_col, wguT, wdT, tm):
    B, D, S = xT.shape
    d_ff = wdT.shape[1]
    tok = lambda rows: pl.BlockSpec((1, rows, tm), lambda b, i: (b, 0, i))
    return pl.pallas_call(
        functools.partial(_ffn_kernel, d_ff=d_ff),
        grid=(B, S // tm),
        in_specs=[tok(D), _const_spec((D, 1)), _const_spec(wguT.shape), _const_spec(wdT.shape)],
        out_specs=tok(D),
        out_shape=jax.ShapeDtypeStruct((B, D, S), xT.dtype),
        **_opts("ffn", "parallel", "parallel"),
    )(xT, g_col, wguT, wdT)


def _rope_tables(positions, dim):
    half = dim // 2
    inv_freq = ROPE_THETA ** (-jnp.arange(half, dtype=jnp.float32) / half)
    ang = positions.astype(jnp.float32)[:, None, :] * inv_freq[None, :, None]
    return jnp.cos(ang), jnp.sin(ang)


def _pack_mixer_weight(w_in):
    off = np.cumsum((0, 512, 128, 128, 512, 128, 128, 128, 128, 128, 128, 24, 384, 256, 32))
    a_q, a_k, a_v, b_q, b_kc, b_vc, b_ks, b_vs, b_kw, b_vw, b_g, c_qa, c_kv, c_kr = [
        w_in[:, off[i]:off[i + 1]] for i in range(14)]
    packed = jnp.concatenate([a_q, b_q, a_k, b_ks, b_kw, a_v, b_vs, b_vw, b_kc, b_vc,
                              c_qa, c_kv, c_kr, b_g], axis=1)
    packed = jnp.pad(packed, ((0, 0), (0, N_MIX - packed.shape[1])))
    return packed.T.astype(CDT), w_in[:, off[14]:].T.astype(CDT)


def kernel(x, mem, positions, norm_mix, w_in, swa_sinks, nsa_pe_k, nsa_pe_v, nsa_wk1, nsa_wk2,
           nsa_wv1, nsa_wv2, mla_q_norm, mla_w_q_b, mla_kv_norm, mla_w_kv_b, w_br_a, w_br_b,
           w_br_c, w_out, norm_xattn, norm_mem, w_xq, w_xkv, w_xo, norm_ffn, w_gate_up, w_down,
           norm_final):
    B, S, D = x.shape
    depth = w_in.shape[0]
    nj = S // NSA_SEL_BLOCK
    nc = 4 * nj
    assert S % TK_SEL == 0 and S % TM_PROJ == 0 and S >= NSA_WINDOW + TQ_NSA
    col = lambda v: v.reshape(-1, 1)
    wt = lambda w: w.T.astype(CDT)

    cos, sin = _rope_tables(positions, HEAD_DIM)
    cosm, sinm = _rope_tables(positions, MLA_ROPE_DIM)
    rr, jj = np.divmod(np.arange(nc), nj)
    cend = jnp.asarray(((4 * jj + rr) * NSA_CMP_STRIDE + NSA_CMP_BLOCK - 1).reshape(nc, 1), jnp.int32)

    xT = _to_feature_major(x, TM_PROJ)
    for l in range(depth):
        wmT, wgT = _pack_mixer_weight(w_in[l])
        o1, o2 = _proj(xT, col(norm_mix[l]), wmT, cos, sin, cosm, sinm,
                       col(mla_q_norm[l]), wt(mla_w_q_b[l]), col(mla_kv_norm[l]), wt(mla_w_kv_b[l]),
                       TM_PROJ)
        kk = o1[:, O_KK:O_KK + 384].reshape(B, 6, HEAD_DIM, S).transpose(0, 1, 3, 2)
        k_nope = o1[:, O_KV:O_KV + 1024].reshape(B, MLA_HEADS, 128, S)[:, :, :MLA_NOPE_DIM]
        k_pe = jnp.broadcast_to(o1[:, None, O_KPE:O_KPE + MLA_ROPE_DIM], (B, MLA_HEADS, MLA_ROPE_DIM, S))
        kcat = jnp.concatenate([k_nope, k_pe], axis=2).transpose(0, 1, 3, 2)
        kr = o2[:, P_C:P_C + 256].reshape(B, 4, HEAD_DIM, S).transpose(0, 1, 3, 2)
        kr = kr.reshape(B, 4, nj, 4, NSA_CMP_STRIDE * HEAD_DIM).transpose(0, 1, 3, 2, 4)
        kr = kr.reshape(B, 4, nc, NSA_CMP_STRIDE * HEAD_DIM)
        gates = o2[:, P_GATE:P_GATE + 24].reshape(B, NSA_KV_GROUPS, 3 * GROUP_HEADS, S)

        pe = jnp.stack([nsa_pe_k[l], nsa_pe_v[l]]).reshape(2, 2, NSA_CMP_STRIDE * HEAD_DIM)
        w1 = jnp.stack([nsa_wk1[l], nsa_wv1[l]]).astype(CDT)
        w2 = jnp.stack([nsa_wk2[l], nsa_wv2[l]]).astype(CDT)
        w2t = jnp.stack([nsa_wk2[l].T, nsa_wv2[l].T]).astype(CDT)
        cmp_tok, cmp_fm = _compress(kr, pe, w1, w2, w2t)

        o_a = _swa(swa_sinks[l], o1, kk, TQ_SWA)
        o_b = _nsa(o1, cend, cmp_tok, cmp_fm, kk, gates, TQ_NSA)
        o_c = _mla(o1, kcat, TQ_MLA)
        xT = _merge(xT, col(norm_mix[l]), wgT, o_a, o_b, o_c,
                    wt(w_br_a[l]), wt(w_br_b[l]), wt(w_br_c[l]), wt(w_out[l]), TM_PROJ)

        n_kv = XATTN_HEADS * XATTN_HEAD_DIM
        kmem, vmemT = _mem_kv(mem, norm_mem[l].reshape(1, D), w_xkv[l][:, :n_kv].astype(CDT),
                              wt(w_xkv[l][:, n_kv:]))
        xT = _xattn(xT, col(norm_xattn[l]), wt(w_xq[l]), kmem, vmemT, wt(w_xo[l]), TM_PROJ)
        d_ff = w_down.shape[1]
        xT = _ffn(xT, col(norm_ffn[l]), wt(w_gate_up[l]), wt(w_down[l]), TM_PROJ)
    return _final_norm(xT, col(norm_final), TM_PROJ)
```

```python
import functools

import numpy as np
import jax
import jax.numpy as jnp
from jax import lax
from jax.experimental import pallas as pl
from jax.experimental.pallas import tpu as pltpu

HEAD_DIM = 64
ROPE_THETA = 10000.0
NORM_EPS = 1e-6
SWA_HEADS = 8
SWA_KV_HEADS = 2
SWA_WINDOW = 128
NSA_HEADS = 8
NSA_KV_GROUPS = 2
NSA_CMP_BLOCK = 32
NSA_CMP_STRIDE = 16
NSA_SEL_BLOCK = 64
NSA_N_SEL = 16
NSA_WINDOW = 512
NSA_FORCE_BONUS = 1e4
MLA_HEADS = 8
MLA_Q_RANK = 384
MLA_KV_RANK = 256
MLA_NOPE_DIM = 64
MLA_ROPE_DIM = 32
MLA_V_DIM = 64
MLA_QK_DIM = MLA_NOPE_DIM + MLA_ROPE_DIM
XATTN_HEADS = 4
XATTN_HEAD_DIM = 128
N_BRANCH = 3
GROUP_HEADS = 4
GROUP_ROWS = GROUP_HEADS * HEAD_DIM
LOG2E = 1.4426950408889634

V7X_VMEM_LIMIT_BYTES = 56 * 1024 * 1024
LANES = 128
ONES_ROWS = 16

CDT = jnp.bfloat16

MASK_VALUE = -1e30

TM_PROJ = 512
TQ_SWA = 256
TQ_NSA = 128
TK_SEL = 512
SEL_BLOCKS_PER_TILE = TK_SEL // NSA_SEL_BLOCK
KSEL_DIM = HEAD_DIM + 16
TQ_MLA = 256
TK_MLA = 512
MLA_HEADS_PER_STEP = 4
FF_CHUNK = 256

R_AQ, R_BQ, R_K, R_V, R_C, R_CQA, R_CKV, R_CKR = 0, 512, 1024, 1408, 1792, 2048, 2432, 2688
N_MIX = 2752
O_AQ, O_BQ, O_BQR, O_KK, O_VV, O_QM, O_KV, O_KPE = 0, 512, 1024, 1536, 1920, 2304, 3072, 4096
N_OUT1 = 4128
P_C, P_GATE = 0, 256
N_OUT2 = 280


def _dot(a, b):
    return jnp.dot(a, b, preferred_element_type=jnp.float32)


def _dot_nt(a, b):
    return lax.dot_general(a, b, (((1,), (1,)), ((), ())), preferred_element_type=jnp.float32)


def _opts(name, *sem):
    return dict(name=name, compiler_params=pltpu.CompilerParams(
        dimension_semantics=sem, vmem_limit_bytes=V7X_VMEM_LIMIT_BYTES))


def _rms_fm(x, g_col):
    ms = jnp.mean(x * x, axis=0, keepdims=True)
    return x * lax.rsqrt(ms + NORM_EPS) * g_col


def _const_spec(shape):
    nd = len(shape)
    return pl.BlockSpec(shape, lambda *_: (0,) * nd)


def _to_feature_major_kernel(x_ref, o_ref):
    o_ref[0] = x_ref[0].T


def _to_feature_major(x, tm):
    B, S, D = x.shape
    return pl.pallas_call(
        _to_feature_major_kernel,
        grid=(B, S // tm),
        in_specs=[pl.BlockSpec((1, tm, D), lambda b, i: (b, i, 0))],
        out_specs=pl.BlockSpec((1, D, tm), lambda b, i: (b, 0, i)),
        out_shape=jax.ShapeDtypeStruct((B, D, S), x.dtype),
        **_opts("to_feature_major", "parallel", "parallel"),
    )(x)


def _final_norm_kernel(x_ref, g_ref, o_ref):
    o_ref[0] = _rms_fm(x_ref[0], g_ref[...]).T


def _final_norm(xT, g_col, tm):
    B, D, S = xT.shape
    return pl.pallas_call(
        _final_norm_kernel,
        grid=(B, S // tm),
        in_specs=[pl.BlockSpec((1, D, tm), lambda b, i: (b, 0, i)), _const_spec((D, 1))],
        out_specs=pl.BlockSpec((1, tm, D), lambda b, i: (b, i, 0)),
        out_shape=jax.ShapeDtypeStruct((B, S, D), xT.dtype),
        **_opts("final_norm", "parallel", "parallel"),
    )(xT, g_col)


def _rope_store(o_ref, row0, y, cos, sin, n_heads, head_dim, scale):
    half = head_dim // 2
    for h in range(n_heads):
        x1 = y[h * head_dim:h * head_dim + half]
        x2 = y[h * head_dim + half:(h + 1) * head_dim]
        r = row0 + h * head_dim
        o_ref[0, r:r + half, :] = ((x1 * cos - x2 * sin) * scale).astype(o_ref.dtype)
        o_ref[0, r + half:r + head_dim, :] = ((x2 * cos + x1 * sin) * scale).astype(o_ref.dtype)


def _proj_kernel(x_ref, g_ref, wm_ref, cos_ref, sin_ref, cosm_ref, sinm_ref,
                 qn_ref, wqb_ref, kvn_ref, wkvb_ref, o1_ref, o2_ref):
    h = _rms_fm(x_ref[0], g_ref[...]).astype(CDT)
    cos, sin = cos_ref[0], sin_ref[0]
    cosm, sinm = cosm_ref[0], sinm_ref[0]
    qk_scale = HEAD_DIM ** -0.5 * LOG2E

    y = _dot(wm_ref[R_AQ:R_AQ + 512, :], h)
    _rope_store(o1_ref, O_AQ, y, cos, sin, SWA_HEADS, HEAD_DIM, qk_scale)
    y = _dot(wm_ref[R_BQ:R_BQ + 512, :], h)
    o1_ref[0, O_BQ:O_BQ + 512, :] = (y * qk_scale).astype(o1_ref.dtype)
    _rope_store(o1_ref, O_BQR, y, cos, sin, NSA_HEADS, HEAD_DIM, qk_scale)
    y = _dot(wm_ref[R_K:R_K + 384, :], h)
    _rope_store(o1_ref, O_KK, y, cos, sin, 6, HEAD_DIM, 1.0)
    y = _dot(wm_ref[R_V:R_V + 384, :], h)
    o1_ref[0, O_VV:O_VV + 384, :] = y.astype(o1_ref.dtype)
    o2_ref[0, P_C:P_C + 256, :] = _dot(wm_ref[R_C:R_C + 256, :], h)

    lat = _dot(wm_ref[R_CQA:R_CQA + MLA_Q_RANK, :], h)
    qm = _dot(wqb_ref[...], _rms_fm(lat, qn_ref[...]).astype(CDT))
    m_scale = MLA_QK_DIM ** -0.5 * LOG2E
    hr = MLA_ROPE_DIM // 2
    for hd in range(MLA_HEADS):
        r = hd * MLA_QK_DIM
        o1_ref[0, O_QM + r:O_QM + r + MLA_NOPE_DIM, :] = (
            qm[r:r + MLA_NOPE_DIM] * m_scale).astype(o1_ref.dtype)
        x1 = qm[r + MLA_NOPE_DIM:r + MLA_NOPE_DIM + hr]
        x2 = qm[r + MLA_NOPE_DIM + hr:r + MLA_QK_DIM]
        o1_ref[0, O_QM + r + MLA_NOPE_DIM:O_QM + r + MLA_NOPE_DIM + hr, :] = (
            (x1 * cosm - x2 * sinm) * m_scale).astype(o1_ref.dtype)
        o1_ref[0, O_QM + r + MLA_NOPE_DIM + hr:O_QM + r + MLA_QK_DIM, :] = (
            (x2 * cosm + x1 * sinm) * m_scale).astype(o1_ref.dtype)
    lat = _dot(wm_ref[R_CKV:R_CKV + MLA_KV_RANK, :], h)
    kv = _dot(wkvb_ref[...], _rms_fm(lat, kvn_ref[...]).astype(CDT))
    o1_ref[0, O_KV:O_KV + 1024, :] = kv.astype(o1_ref.dtype)
    y = _dot(wm_ref[R_CKR:R_CKR + 64, :], h)
    x1, x2 = y[0:hr], y[hr:2 * hr]
    o1_ref[0, O_KPE:O_KPE + hr, :] = (x1 * cosm - x2 * sinm).astype(o1_ref.dtype)
    o1_ref[0, O_KPE + hr:O_KPE + 2 * hr, :] = (x2 * cosm + x1 * sinm).astype(o1_ref.dtype)
    o2_ref[0, P_GATE:P_GATE + 24, :] = jax.nn.sigmoid(y[MLA_ROPE_DIM:MLA_ROPE_DIM + 24])


def _proj(xT, g_col, wmT, cos, sin, cosm, sinm, qn, wqbT, kvn, wkvbT, tm):
    B, D, S = xT.shape
    tok = lambda rows: pl.BlockSpec((1, rows, tm), lambda b, i: (b, 0, i))
    return pl.pallas_call(
        _proj_kernel,
        grid=(B, S // tm),
        in_specs=[tok(D), _const_spec((D, 1)), _const_spec(wmT.shape),
                  tok(32), tok(32), tok(16), tok(16),
                  _const_spec(qn.shape), _const_spec(wqbT.shape),
                  _const_spec(kvn.shape), _const_spec(wkvbT.shape)],
        out_specs=[tok(N_OUT1), tok(N_OUT2)],
        out_shape=[jax.ShapeDtypeStruct((B, N_OUT1, S), CDT),
                   jax.ShapeDtypeStruct((B, N_OUT2, S), jnp.float32)],
        **_opts("mixer_proj", "parallel", "parallel"),
    )(xT, g_col, wmT, cos, sin, cosm, sinm, qn, wqbT, kvn, wkvbT)


def _compress_kernel(kr_ref, pe_ref, w1_ref, w2_ref, w2t_ref, tok_ref, fm_ref):
    kr = kr_ref[0, 0]
    nc = kr.shape[0]
    nj = nc // 4
    half = kr.shape[1]
    a = _dot((kr + pe_ref[0, 0:1, :]).astype(CDT), w1_ref[0, 0:half, :])
    bm = _dot((kr + pe_ref[0, 1:2, :]).astype(CDT), w1_ref[0, half:2 * half, :])
    wrap = pltpu.roll(bm[0:nj], nj - 1, 0)
    row = lax.broadcasted_iota(jnp.int32, (nj, 1), 0)
    wrap = jnp.where(row == nj - 1, 0.0, wrap)
    pre = a + jnp.concatenate([bm[nj:], wrap], axis=0)
    hid = (pre * jax.nn.sigmoid(pre)).astype(CDT)
    tok_ref[0, 0] = _dot(hid, w2_ref[0]).astype(tok_ref.dtype)
    fm_ref[0, 0] = _dot_nt(w2t_ref[0], hid).astype(fm_ref.dtype)


def _compress(kr, pe, w1, w2, w2t):
    B, _, NC, W = kr.shape
    return pl.pallas_call(
        _compress_kernel,
        grid=(B, 4),
        in_specs=[pl.BlockSpec((1, 1, NC, W), lambda b, n: (b, n, 0, 0)),
                  pl.BlockSpec((1, 2, W), lambda b, n: (n // 2, 0, 0)),
                  pl.BlockSpec((1, 2 * W, HEAD_DIM), lambda b, n: (n // 2, 0, 0)),
                  pl.BlockSpec((1, HEAD_DIM, HEAD_DIM), lambda b, n: (n // 2, 0, 0)),
                  pl.BlockSpec((1, HEAD_DIM, HEAD_DIM), lambda b, n: (n // 2, 0, 0))],
        out_specs=[pl.BlockSpec((1, 1, NC, HEAD_DIM), lambda b, n: (b, n, 0, 0)),
                   pl.BlockSpec((1, 1, HEAD_DIM, NC), lambda b, n: (b, n, 0, 0))],
        out_shape=[jax.ShapeDtypeStruct((B, 4, NC, HEAD_DIM), CDT),
                   jax.ShapeDtypeStruct((B, 4, HEAD_DIM, NC), CDT)],
        **_opts("nsa_compress", "parallel", "parallel"),
    )(kr, pe, w1, w2, w2t)


def _group_queries(q_ref, g):
    return jnp.concatenate([q_ref[0, (g * GROUP_HEADS + r) * HEAD_DIM:(g * GROUP_HEADS + r + 1) * HEAD_DIM, :]
                            for r in range(GROUP_HEADS)], axis=1)


def _with_ones(vt):
    return jnp.concatenate([vt, jnp.ones((ONES_ROWS, vt.shape[1]), vt.dtype)], axis=0)


def _softmax_probs(sc, m_run):
    m_new = jnp.maximum(m_run, jnp.max(sc, axis=0, keepdims=True))
    return m_new, jnp.exp2(sc - m_new).astype(CDT), jnp.exp2(m_run - m_new)


def _softmax_step(sc, m_run, acc, vt_aug):
    m_new, p, alpha = _softmax_probs(sc, m_run)
    return m_new, alpha * acc + _dot(vt_aug, p)


def _window_attention(q4, t4, k, vt, k0, window, sink4=None):
    span = k.shape[0]
    s = _dot(k, q4)
    kpos = k0 + lax.broadcasted_iota(jnp.int32, (span, 1), 0)
    s = jnp.where(kpos <= t4, jnp.where(kpos > t4 - window, s, -jnp.inf), -jnp.inf)
    m = jnp.max(s, axis=0, keepdims=True)
    if sink4 is not None:
        m = jnp.maximum(m, sink4)
    pv = _dot(_with_ones(vt), jnp.exp2(s - m).astype(CDT))
    den = pv[HEAD_DIM:HEAD_DIM + 1]
    if sink4 is not None:
        den = den + jnp.exp2(sink4 - m)
    return pv[0:HEAD_DIM] * (1.0 / den)


def _swa_kernel(sink_ref, q_ref, k_ref, vt_ref, o_ref, *, tq):
    g = pl.program_id(1)
    qs = pl.program_id(2) * tq
    q4 = _group_queries(q_ref, 0)
    t = qs + lax.broadcasted_iota(jnp.int32, (1, tq), 1)
    t4 = jnp.concatenate([t] * GROUP_HEADS, axis=1)
    sink4 = jnp.concatenate([jnp.full((1, tq), sink_ref[g * GROUP_HEADS + r] * LOG2E, jnp.float32)
                             for r in range(GROUP_HEADS)], axis=1)
    span = SWA_WINDOW + tq
    k0 = pl.multiple_of(jnp.maximum(qs - SWA_WINDOW, 0), LANES)
    o4 = _window_attention(q4, t4, k_ref[0, 0, pl.ds(k0, span), :], vt_ref[0, :, pl.ds(k0, span)],
                           k0, SWA_WINDOW, sink4)
    for r in range(GROUP_HEADS):
        o_ref[0, r * HEAD_DIM:(r + 1) * HEAD_DIM, :] = o4[:, r * tq:(r + 1) * tq].astype(o_ref.dtype)


def _swa(sinks, o1, kk, tq):
    B, _, S = o1.shape
    return pl.pallas_call(
        functools.partial(_swa_kernel, tq=tq),
        grid=(B, SWA_KV_HEADS, S // tq),
        in_specs=[pl.BlockSpec(memory_space=pltpu.SMEM),
                  pl.BlockSpec((1, GROUP_ROWS, tq), lambda b, g, i: (b, O_AQ // GROUP_ROWS + g, i)),
                  pl.BlockSpec((1, 1, S, HEAD_DIM), lambda b, g, i: (b, g, 0, 0)),
                  pl.BlockSpec((1, HEAD_DIM, S), lambda b, g, i: (b, O_VV // HEAD_DIM + g, 0))],
        out_specs=pl.BlockSpec((1, GROUP_ROWS, tq), lambda b, g, i: (b, g, i)),
        out_shape=jax.ShapeDtypeStruct((B, SWA_HEADS * HEAD_DIM, S), CDT),
        **_opts("swa_attn", "parallel", "parallel", "parallel"),
    )(sinks, o1, kk, o1)


def _nsa_kernel(q_ref, qr_ref, cend_ref, kc_ref, vct_ref, ks_ref, vst_ref, kw_ref, vwt_ref,
                gate_ref, o_ref, bias_ref, *, tq, n_sel):
    qs = pl.program_id(1) * tq
    nc = kc_ref.shape[2]
    nj = nc // 4
    G = NSA_KV_GROUPS
    t = qs + lax.broadcasted_iota(jnp.int32, (1, tq), 1)
    t2 = jnp.concatenate([t] * 2, axis=1)
    t4 = jnp.concatenate([t] * GROUP_HEADS, axis=1)
    jrow = lax.broadcasted_iota(jnp.int32, (nj, tq), 0)
    jf = jrow.astype(jnp.float32)
    cur = lax.shift_right_arithmetic(t, int(np.log2(NSA_SEL_BLOCK)))
    bonus = jnp.where((jrow == 0) | (jrow == cur) | (jrow == cur - 1), NSA_FORCE_BONUS, 0.0)
    span = NSA_WINDOW + tq
    kw0 = pl.multiple_of(jnp.maximum(qs - NSA_WINDOW, 0), LANES)

    qr4, fixed = [], []
    for g in range(G):
        q4 = _group_queries(q_ref, g)
        qr4.append(_group_queries(qr_ref, g))
        s = _dot(kc_ref[0, g], q4)
        s = jnp.where(cend_ref[...] <= t4, s, -jnp.inf)
        m = jnp.max(s, axis=0, keepdims=True)
        m = jnp.where(m == -jnp.inf, 0.0, m)
        e = jnp.exp2(s - m)
        den = jnp.sum(e, axis=0, keepdims=True)
        p = e * (1.0 / jnp.where(den > 0, den, 1.0))
        o_cmp = _dot(vct_ref[0, g], p.astype(CDT))

        psum = p[:, 0:tq]
        for r in range(1, GROUP_HEADS):
            psum = psum + p[:, r * tq:(r + 1) * tq]
        p3 = psum[3 * nj:4 * nj]
        prev = jnp.where(jrow == 0, 0.0, pltpu.roll(p3, 1, 0))
        imp = prev + 2.0 * (psum[0:nj] + psum[nj:2 * nj] + psum[2 * nj:3 * nj]) + p3

        score = jnp.where(jrow <= cur, imp + bonus, -jnp.inf)
        for _ in range(n_sel):
            mx = jnp.max(score, axis=0, keepdims=True)
            idx = jnp.min(jnp.where(score == mx, jf, float(nj)), axis=0, keepdims=True)
            idx = jnp.where(mx > -jnp.inf, idx, float(nj))
            score = jnp.where(jf == idx, -jnp.inf, score)
        bias_ref[g] = jnp.where(jrow <= cur, jnp.where(score == -jnp.inf, 0.0, MASK_VALUE), MASK_VALUE)

        o_win = _window_attention(qr4[g], t4, kw_ref[0, g, pl.ds(kw0, span), :],
                                  vwt_ref[0, g * HEAD_DIM:(g + 1) * HEAD_DIM, pl.ds(kw0, span)],
                                  kw0, NSA_WINDOW)
        parts = []
        for r in range(GROUP_HEADS):
            sl = slice(r * tq, (r + 1) * tq)
            g0 = gate_ref[0, g, 3 * r:3 * r + 1, :]
            g2 = gate_ref[0, g, 3 * r + 2:3 * r + 3, :]
            parts.append((g0 * o_cmp[:, sl], g2 * o_win[:, sl]))
        fixed.append(parts)

    def sel_tile(kt, carry, diagonal):
        k0 = pl.multiple_of(kt * TK_SEL, TK_SEL)
        scs = []
        for g in range(G):
            b8 = bias_ref[g, pl.ds(kt * SEL_BLOCKS_PER_TILE, SEL_BLOCKS_PER_TILE), :]
            b16 = jnp.concatenate([b8, jnp.zeros_like(b8)], axis=0)
            qa = jnp.concatenate([qr4[g], jnp.concatenate([b16] * GROUP_HEADS, axis=1).astype(CDT)],
                                 axis=0)
            k_tile = ks_ref[0, g, pl.ds(k0, TK_SEL), :]
            for hp in range(2):
                sc = _dot(k_tile, qa[:, hp * 2 * tq:(hp + 1) * 2 * tq])
                if diagonal:
                    kpos = k0 + lax.broadcasted_iota(jnp.int32, (TK_SEL, 1), 0)
                    sc = jnp.where(kpos <= t2, sc, MASK_VALUE)
                scs.append(sc)
        parts = [_softmax_probs(scs[c], carry[c][0]) for c in range(2 * G)]
        out = []
        for g in range(G):
            vt_aug = _with_ones(vst_ref[0, g * HEAD_DIM:(g + 1) * HEAD_DIM, pl.ds(k0, TK_SEL)])
            for hp in range(2):
                m_new, p, alpha = parts[2 * g + hp]
                out.append((m_new, alpha * carry[2 * g + hp][1] + _dot(vt_aug, p)))
        return tuple(out)

    init = tuple((jnp.full((1, 2 * tq), MASK_VALUE, jnp.float32),
                  jnp.zeros((HEAD_DIM + ONES_ROWS, 2 * tq), jnp.float32)) for _ in range(2 * G))
    n_full = qs // TK_SEL
    carry = lax.fori_loop(0, n_full, lambda kt, c: sel_tile(kt, c, False), init)
    carry = sel_tile(n_full, carry, True)

    for g in range(G):
        for r in range(GROUP_HEADS):
            _, acc = carry[2 * g + r // 2]
            sl = slice((r % 2) * tq, (r % 2 + 1) * tq)
            o_sel = acc[0:HEAD_DIM, sl] * (1.0 / acc[HEAD_DIM:HEAD_DIM + 1, sl])
            g1 = gate_ref[0, g, 3 * r + 1:3 * r + 2, :]
            c_part, w_part = fixed[g][r]
            row = (g * GROUP_HEADS + r) * HEAD_DIM
            o_ref[0, row:row + HEAD_DIM, :] = (c_part + g1 * o_sel + w_part).astype(o_ref.dtype)


def _nsa(o1, cend, cmp_tok, cmp_fm, ks_aug, kk, gates, tq):
    B, _, S = o1.shape
    nc = cmp_tok.shape[2]
    nj = nc // 4
    n_sel = min(NSA_N_SEL, nj)
    G = NSA_KV_GROUPS
    rows = NSA_HEADS * HEAD_DIM
    vrows = G * HEAD_DIM
    return pl.pallas_call(
        functools.partial(_nsa_kernel, tq=tq, n_sel=n_sel),
        grid=(B, S // tq),
        in_specs=[pl.BlockSpec((1, rows, tq), lambda b, i: (b, O_BQ // rows, i)),
                  pl.BlockSpec((1, rows, tq), lambda b, i: (b, O_BQR // rows, i)),
                  _const_spec((nc, 1)),
                  pl.BlockSpec((1, G, nc, HEAD_DIM), lambda b, i: (b, 0, 0, 0)),
                  pl.BlockSpec((1, G, HEAD_DIM, nc), lambda b, i: (b, 1, 0, 0)),
                  pl.BlockSpec((1, G, S, KSEL_DIM), lambda b, i: (b, 0, 0, 0)),
                  pl.BlockSpec((1, vrows, S), lambda b, i: (b, O_VV // vrows + 1, 0)),
                  pl.BlockSpec((1, G, S, HEAD_DIM), lambda b, i: (b, 2, 0, 0)),
                  pl.BlockSpec((1, vrows, S), lambda b, i: (b, O_VV // vrows + 2, 0)),
                  pl.BlockSpec((1, G, 3 * GROUP_HEADS, tq), lambda b, i: (b, 0, 0, i))],
        out_specs=pl.BlockSpec((1, rows, tq), lambda b, i: (b, 0, i)),
        out_shape=jax.ShapeDtypeStruct((B, rows, S), CDT),
        scratch_shapes=[pltpu.VMEM((G, nj, tq), jnp.float32)],
        **_opts("nsa_attn", "parallel", "arbitrary"),
    )(o1, o1, cend, cmp_tok, cmp_fm, ks_aug, o1, kk, o1, gates)


def _mla_kernel(q_ref, k_ref, *rest, tq, hps):
    vt_refs, o_ref = rest[:hps], rest[hps]
    qs = pl.program_id(2) * tq
    q = [q_ref[0, h * MLA_QK_DIM:(h + 1) * MLA_QK_DIM, :] for h in range(hps)]
    t = qs + lax.broadcasted_iota(jnp.int32, (1, tq), 1)

    def scores(kt):
        k0 = pl.multiple_of(kt * TK_MLA, TK_MLA)
        return tuple(_dot(k_ref[0, h, pl.ds(k0, TK_MLA), :], q[h]) for h in range(hps))

    def consume(kt, scs, state):
        k0 = pl.multiple_of(kt * TK_MLA, TK_MLA)
        parts = [_softmax_probs(scs[h], state[h][0]) for h in range(hps)]
        out = []
        for h in range(hps):
            m_new, p, alpha = parts[h]
            pv = _dot(_with_ones(vt_refs[h][0, :, pl.ds(k0, TK_MLA)]), p)
            out.append((m_new, alpha * state[h][1] + pv))
        return tuple(out)

    init = tuple((jnp.full((1, tq), MASK_VALUE, jnp.float32),
                  jnp.zeros((MLA_V_DIM + ONES_ROWS, tq), jnp.float32)) for _ in range(hps))
    n_full = qs // TK_MLA
    state = lax.fori_loop(0, n_full, lambda kt, st: consume(kt, scores(kt), st), init)
    kpos = n_full * TK_MLA + lax.broadcasted_iota(jnp.int32, (TK_MLA, 1), 0)
    scs = tuple(jnp.where(kpos <= t, sc, MASK_VALUE) for sc in scores(n_full))
    state = consume(n_full, scs, state)
    for h in range(hps):
        _, acc = state[h]
        o_ref[0, h * MLA_V_DIM:(h + 1) * MLA_V_DIM, :] = (
            acc[0:MLA_V_DIM] * (1.0 / acc[MLA_V_DIM:MLA_V_DIM + 1])).astype(o_ref.dtype)


def _mla(o1, kcat, tq):
    B, _, S = o1.shape
    hps = MLA_HEADS_PER_STEP
    v_spec = lambda h: pl.BlockSpec(
        (1, MLA_V_DIM, S), lambda b, hg, i: (b, O_KV // MLA_V_DIM + 2 * (hg * hps + h) + 1, 0))
    return pl.pallas_call(
        functools.partial(_mla_kernel, tq=tq, hps=hps),
        grid=(B, MLA_HEADS // hps, S // tq),
        in_specs=[pl.BlockSpec((1, hps * MLA_QK_DIM, tq),
                               lambda b, hg, i: (b, O_QM // (hps * MLA_QK_DIM) + hg, i)),
                  pl.BlockSpec((1, hps, S, MLA_QK_DIM), lambda b, hg, i: (b, hg, 0, 0))]
                 + [v_spec(h) for h in range(hps)],
        out_specs=pl.BlockSpec((1, hps * MLA_V_DIM, tq), lambda b, hg, i: (b, hg, i)),
        out_shape=jax.ShapeDtypeStruct((B, MLA_HEADS * MLA_V_DIM, S), CDT),
        **_opts("mla_attn", "parallel", "parallel", "parallel"),
    )(o1, kcat, *([o1] * hps))


def _merge_kernel(x_ref, g_ref, wg_ref, oa_ref, ob_ref, oc_ref, wa_ref, wb_ref, wc_ref, wo_ref, o_ref):
    x = x_ref[0]
    d = x.shape[0]
    h = _rms_fm(x, g_ref[...]).astype(CDT)
    merged = None
    for i, (br_ref, w_ref) in enumerate(((oa_ref, wa_ref), (ob_ref, wb_ref), (oc_ref, wc_ref))):
        gate = jax.nn.sigmoid(_dot(wg_ref[i * d:(i + 1) * d, :], h))
        term = gate * _dot(w_ref[...], br_ref[0])
        merged = term if merged is None else merged + term
    o_ref[0] = x + _dot(wo_ref[...], merged.astype(CDT))


def _merge(xT, g_col, wgT, oa, ob, oc, waT, wbT, wcT, woT, tm):
    B, D, S = xT.shape
    tok = lambda rows: pl.BlockSpec((1, rows, tm), lambda b, i: (b, 0, i))
    return pl.pallas_call(
        _merge_kernel,
        grid=(B, S // tm),
        in_specs=[tok(D), _const_spec((D, 1)), _const_spec(wgT.shape),
                  tok(oa.shape[1]), tok(ob.shape[1]), tok(oc.shape[1]),
                  _const_spec(waT.shape), _const_spec(wbT.shape), _const_spec(wcT.shape),
                  _const_spec(woT.shape)],
        out_specs=tok(D),
        out_shape=jax.ShapeDtypeStruct((B, D, S), xT.dtype),
        **_opts("merge_out", "parallel", "parallel"),
    )(xT, g_col, wgT, oa, ob, oc, waT, wbT, wcT, woT)


def _mem_kv_kernel(mem_ref, g_ref, wk_ref, wvt_ref, k_ref, vt_ref):
    m = mem_ref[0]
    ms = jnp.mean(m * m, axis=-1, keepdims=True)
    hm = (m * lax.rsqrt(ms + NORM_EPS) * g_ref[...]).astype(CDT)
    k_ref[0] = _dot(hm, wk_ref[...]).astype(k_ref.dtype)
    vt_ref[0] = _dot_nt(wvt_ref[...], hm).astype(vt_ref.dtype)


def _mem_kv(mem, g_row, wk, wvT):
    B, M, D = mem.shape
    n = wk.shape[1]
    return pl.pallas_call(
        _mem_kv_kernel,
        grid=(B,),
        in_specs=[pl.BlockSpec((1, M, D), lambda b: (b, 0, 0)), _const_spec((1, D)),
                  _const_spec(wk.shape), _const_spec(wvT.shape)],
        out_specs=[pl.BlockSpec((1, M, n), lambda b: (b, 0, 0)),
                   pl.BlockSpec((1, n, M), lambda b: (b, 0, 0))],
        out_shape=[jax.ShapeDtypeStruct((B, M, n), CDT), jax.ShapeDtypeStruct((B, n, M), CDT)],
        **_opts("mem_kv", "parallel"),
    )(mem, g_row, wk, wvT)


def _xattn_kernel(x_ref, g_ref, wq_ref, k_ref, vt_ref, wo_ref, o_ref):
    x = x_ref[0]
    h = _rms_fm(x, g_ref[...]).astype(CDT)
    q = (_dot(wq_ref[...], h) * (XATTN_HEAD_DIM ** -0.5 * LOG2E)).astype(CDT)
    outs = []
    for hd in range(XATTN_HEADS):
        rows = slice(hd * XATTN_HEAD_DIM, (hd + 1) * XATTN_HEAD_DIM)
        s = _dot(k_ref[0, :, rows], q[rows])
        e = jnp.exp2(s - jnp.max(s, axis=0, keepdims=True)).astype(CDT)
        pv = _dot(_with_ones(vt_ref[0, rows, :]), e)
        outs.append((pv[0:XATTN_HEAD_DIM] * (1.0 / pv[XATTN_HEAD_DIM:XATTN_HEAD_DIM + 1])).astype(CDT))
    o_ref[0] = x + _dot(wo_ref[...], jnp.concatenate(outs, axis=0))


def _xattn(xT, g_col, wqT, kmem, vmemT, woT, tm):
    B, D, S = xT.shape
    M, n = kmem.shape[1], kmem.shape[2]
    tok = lambda rows: pl.BlockSpec((1, rows, tm), lambda b, i: (b, 0, i))
    return pl.pallas_call(
        _xattn_kernel,
        grid=(B, S // tm),
        in_specs=[tok(D), _const_spec((D, 1)), _const_spec(wqT.shape),
                  pl.BlockSpec((1, M, n), lambda b, i: (b, 0, 0)),
                  pl.BlockSpec((1, n, M), lambda b, i: (b, 0, 0)),
                  _const_spec(woT.shape)],
        out_specs=tok(D),
        out_shape=jax.ShapeDtypeStruct((B, D, S), xT.dtype),
        **_opts("xattn", "parallel", "parallel"),
    )(xT, g_col, wqT, kmem, vmemT, woT)


def _ffn_kernel(x_ref, g_ref, wgu_ref, wd_ref, o_ref, *, d_ff):
    x = x_ref[0]
    h = _rms_fm(x, g_ref[...]).astype(CDT)
    acc = x
    for c in range(d_ff // FF_CHUNK):
        r = c * FF_CHUNK
        gate = _dot(wgu_ref[r:r + FF_CHUNK, :], h)
        up = _dot(wgu_ref[d_ff + r:d_ff + r + FF_CHUNK, :], h)
        act = (gate * jax.nn.sigmoid(gate) * up).astype(CDT)
        acc = acc + _dot(wd_ref[:, r:r + FF_CHUNK], act)
    o_ref[0] = acc


def _ffn(xT, g_col, wguT, wdT, tm):
    B, D, S = xT.shape
    d_ff = wdT.shape[1]
    tok = lambda rows: pl.BlockSpec((1, rows, tm), lambda b, i: (b, 0, i))
    return pl.pallas_call(
        functools.partial(_ffn_kernel, d_ff=d_ff),
        grid=(B, S // tm),
        in_specs=[tok(D), _const_spec((D, 1)), _const_spec(wguT.shape), _const_spec(wdT.shape)],
        out_specs=tok(D),
        out_shape=jax.ShapeDtypeStruct((B, D, S), xT.dtype),
        **_opts("ffn", "parallel", "parallel"),
    )(xT, g_col, wguT, wdT)


def _rope_tables(positions, dim):
    half = dim // 2
    inv_freq = ROPE_THETA ** (-jnp.arange(half, dtype=jnp.float32) / half)
    ang = positions.astype(jnp.float32)[:, None, :] * inv_freq[None, :, None]
    return jnp.cos(ang), jnp.sin(ang)


def _pack_mixer_weight(w_in):
    off = np.cumsum((0, 512, 128, 128, 512, 128, 128, 128, 128, 128, 128, 24, 384, 256, 32))
    a_q, a_k, a_v, b_q, b_kc, b_vc, b_ks, b_vs, b_kw, b_vw, b_g, c_qa, c_kv, c_kr = [
        w_in[:, off[i]:off[i + 1]] for i in range(14)]
    packed = jnp.concatenate([a_q, b_q, a_k, b_ks, b_kw, a_v, b_vs, b_vw, b_kc, b_vc,
                              c_qa, c_kv, c_kr, b_g], axis=1)
    packed = jnp.pad(packed, ((0, 0), (0, N_MIX - packed.shape[1])))
    return packed.T.astype(CDT), w_in[:, off[14]:].T.astype(CDT)


def kernel(x, mem, positions, norm_mix, w_in, swa_sinks, nsa_pe_k, nsa_pe_v, nsa_wk1, nsa_wk2,
           nsa_wv1, nsa_wv2, mla_q_norm, mla_w_q_b, mla_kv_norm, mla_w_kv_b, w_br_a, w_br_b,
           w_br_c, w_out, norm_xattn, norm_mem, w_xq, w_xkv, w_xo, norm_ffn, w_gate_up, w_down,
           norm_final):
    B, S, D = x.shape
    depth = w_in.shape[0]
    nj = S // NSA_SEL_BLOCK
    nc = 4 * nj
    assert S % TK_SEL == 0 and S % TM_PROJ == 0 and S >= NSA_WINDOW + TQ_NSA
    col = lambda v: v.reshape(-1, 1)
    wt = lambda w: w.T.astype(CDT)

    cos, sin = _rope_tables(positions, HEAD_DIM)
    cosm, sinm = _rope_tables(positions, MLA_ROPE_DIM)
    rr, jj = np.divmod(np.arange(nc), nj)
    cend = jnp.asarray(((4 * jj + rr) * NSA_CMP_STRIDE + NSA_CMP_BLOCK - 1).reshape(nc, 1), jnp.int32)
    blk = (np.arange(S) // NSA_SEL_BLOCK) % SEL_BLOCKS_PER_TILE
    onehot = jnp.asarray(blk[:, None] == np.arange(KSEL_DIM - HEAD_DIM)[None, :], CDT)

    xT = _to_feature_major(x, TM_PROJ)
    for l in range(depth):
        wmT, wgT = _pack_mixer_weight(w_in[l])
        o1, o2 = _proj(xT, col(norm_mix[l]), wmT, cos, sin, cosm, sinm,
                       col(mla_q_norm[l]), wt(mla_w_q_b[l]), col(mla_kv_norm[l]), wt(mla_w_kv_b[l]),
                       TM_PROJ)
        kk = o1[:, O_KK:O_KK + 384].reshape(B, 6, HEAD_DIM, S).transpose(0, 1, 3, 2)
        ks_aug = jnp.concatenate(
            [kk[:, 2:4], jnp.broadcast_to(onehot, (B, NSA_KV_GROUPS, S, KSEL_DIM - HEAD_DIM))], axis=3)
        k_nope = o1[:, O_KV:O_KV + 1024].reshape(B, MLA_HEADS, 128, S)[:, :, :MLA_NOPE_DIM]
        k_pe = jnp.broadcast_to(o1[:, None, O_KPE:O_KPE + MLA_ROPE_DIM], (B, MLA_HEADS, MLA_ROPE_DIM, S))
        kcat = jnp.concatenate([k_nope, k_pe], axis=2).transpose(0, 1, 3, 2)
        kr = o2[:, P_C:P_C + 256].reshape(B, 4, HEAD_DIM, S).transpose(0, 1, 3, 2)
        kr = kr.reshape(B, 4, nj, 4, NSA_CMP_STRIDE * HEAD_DIM).transpose(0, 1, 3, 2, 4)
        kr = kr.reshape(B, 4, nc, NSA_CMP_STRIDE * HEAD_DIM)
        gates = o2[:, P_GATE:P_GATE + 24].reshape(B, NSA_KV_GROUPS, 3 * GROUP_HEADS, S)

        pe = jnp.stack([nsa_pe_k[l], nsa_pe_v[l]]).reshape(2, 2, NSA_CMP_STRIDE * HEAD_DIM)
        w1 = jnp.stack([nsa_wk1[l], nsa_wv1[l]]).astype(CDT)
        w2 = jnp.stack([nsa_wk2[l], nsa_wv2[l]]).astype(CDT)
        w2t = jnp.stack([nsa_wk2[l].T, nsa_wv2[l].T]).astype(CDT)
        cmp_tok, cmp_fm = _compress(kr, pe, w1, w2, w2t)

        o_a = _swa(swa_sinks[l], o1, kk, TQ_SWA)
        o_b = _nsa(o1, cend, cmp_tok, cmp_fm, ks_aug, kk, gates, TQ_NSA)
        o_c = _mla(o1, kcat, TQ_MLA)
        xT = _merge(xT, col(norm_mix[l]), wgT, o_a, o_b, o_c,
                    wt(w_br_a[l]), wt(w_br_b[l]), wt(w_br_c[l]), wt(w_out[l]), TM_PROJ)

        n_kv = XATTN_HEADS * XATTN_HEAD_DIM
        kmem, vmemT = _mem_kv(mem, norm_mem[l].reshape(1, D), w_xkv[l][:, :n_kv].astype(CDT),
                              wt(w_xkv[l][:, n_kv:]))
        xT = _xattn(xT, col(norm_xattn[l]), wt(w_xq[l]), kmem, vmemT, wt(w_xo[l]), TM_PROJ)
        xT = _ffn(xT, col(norm_ffn[l]), wt(w_gate_up[l]), wt(w_down[l]), TM_PROJ)
    return _final_norm(xT, col(norm_final), TM_PROJ)
```

```python
import functools

import numpy as np
import jax
import jax.numpy as jnp
from jax import lax
from jax.experimental import pallas as pl
from jax.experimental.pallas import tpu as pltpu

HEAD_DIM = 64
ROPE_THETA = 10000.0
NORM_EPS = 1e-6
SWA_HEADS = 8
SWA_KV_HEADS = 2
SWA_WINDOW = 128
NSA_HEADS = 8
NSA_KV_GROUPS = 2
NSA_CMP_BLOCK = 32
NSA_CMP_STRIDE = 16
NSA_SEL_BLOCK = 64
NSA_N_SEL = 16
NSA_WINDOW = 512
NSA_FORCE_BONUS = 1e4
MLA_HEADS = 8
MLA_Q_RANK = 384
MLA_KV_RANK = 256
MLA_NOPE_DIM = 64
MLA_ROPE_DIM = 32
MLA_V_DIM = 64
MLA_QK_DIM = MLA_NOPE_DIM + MLA_ROPE_DIM
XATTN_HEADS = 4
XATTN_HEAD_DIM = 128
N_BRANCH = 3
GROUP_HEADS = 4
GROUP_ROWS = GROUP_HEADS * HEAD_DIM
LOG2E = 1.4426950408889634

V7X_VMEM_LIMIT_BYTES = 56 * 1024 * 1024
LANES = 128
ONES_ROWS = 16

CDT = jnp.bfloat16

MASK_VALUE = -1e30

TM_PROJ = 512
TQ_SWA = 256
TQ_NSA = 128
TK_SEL = 512
SEL_BLOCKS_PER_TILE = TK_SEL // NSA_SEL_BLOCK
KSEL_DIM = HEAD_DIM + 16
TQ_MLA = 256
TK_MLA = 512
MLA_HEADS_PER_STEP = 4
FF_CHUNK = 256

R_AQ, R_BQ, R_K, R_V, R_C, R_CQA, R_CKV, R_CKR = 0, 512, 1024, 1408, 1792, 2048, 2432, 2688
N_MIX = 2752
O_AQ, O_BQ, O_BQR, O_VV, O_QM, O_KV = 0, 512, 1024, 1536, 1920, 2688
N_OUT1 = 3712
P_C, P_GATE = 0, 256
N_OUT2 = 280


def _dot(a, b):
    return jnp.dot(a, b, preferred_element_type=jnp.float32)


def _dot_nt(a, b):
    return lax.dot_general(a, b, (((1,), (1,)), ((), ())), preferred_element_type=jnp.float32)


def _opts(name, *sem):
    return dict(name=name, compiler_params=pltpu.CompilerParams(
        dimension_semantics=sem, vmem_limit_bytes=V7X_VMEM_LIMIT_BYTES))


def _rms_fm(x, g_col):
    ms = jnp.mean(x * x, axis=0, keepdims=True)
    return x * lax.rsqrt(ms + NORM_EPS) * g_col


def _const_spec(shape):
    nd = len(shape)
    return pl.BlockSpec(shape, lambda *_: (0,) * nd)


def _to_feature_major_kernel(x_ref, o_ref):
    o_ref[0] = x_ref[0].T


def _to_feature_major(x, tm):
    B, S, D = x.shape
    return pl.pallas_call(
        _to_feature_major_kernel,
        grid=(B, S // tm),
        in_specs=[pl.BlockSpec((1, tm, D), lambda b, i: (b, i, 0))],
        out_specs=pl.BlockSpec((1, D, tm), lambda b, i: (b, 0, i)),
        out_shape=jax.ShapeDtypeStruct((B, D, S), x.dtype),
        **_opts("to_feature_major", "parallel", "parallel"),
    )(x)


def _final_norm_kernel(x_ref, g_ref, o_ref):
    o_ref[0] = _rms_fm(x_ref[0], g_ref[...]).T


def _final_norm(xT, g_col, tm):
    B, D, S = xT.shape
    return pl.pallas_call(
        _final_norm_kernel,
        grid=(B, S // tm),
        in_specs=[pl.BlockSpec((1, D, tm), lambda b, i: (b, 0, i)), _const_spec((D, 1))],
        out_specs=pl.BlockSpec((1, tm, D), lambda b, i: (b, i, 0)),
        out_shape=jax.ShapeDtypeStruct((B, S, D), xT.dtype),
        **_opts("final_norm", "parallel", "parallel"),
    )(xT, g_col)


def _rope_store(o_ref, row0, y, cos, sin, n_heads, head_dim, scale):
    half = head_dim // 2
    for h in range(n_heads):
        x1 = y[h * head_dim:h * head_dim + half]
        x2 = y[h * head_dim + half:(h + 1) * head_dim]
        r = row0 + h * head_dim
        o_ref[0, r:r + half, :] = ((x1 * cos - x2 * sin) * scale).astype(o_ref.dtype)
        o_ref[0, r + half:r + head_dim, :] = ((x2 * cos + x1 * sin) * scale).astype(o_ref.dtype)


def _proj_kernel(x_ref, g_ref, wm_ref, cos_ref, sin_ref, cosm_ref, sinm_ref,
                 qn_ref, wqb_ref, kvn_ref, wkvb_ref, o1_ref, o2_ref, kk_ref, ks_ref, kc_ref):
    h = _rms_fm(x_ref[0], g_ref[...]).astype(CDT)
    cos, sin = cos_ref[0], sin_ref[0]
    cosm, sinm = cosm_ref[0], sinm_ref[0]
    qk_scale = HEAD_DIM ** -0.5 * LOG2E

    y = _dot(wm_ref[R_AQ:R_AQ + 512, :], h)
    _rope_store(o1_ref, O_AQ, y, cos, sin, SWA_HEADS, HEAD_DIM, qk_scale)
    y = _dot(wm_ref[R_BQ:R_BQ + 512, :], h)
    o1_ref[0, O_BQ:O_BQ + 512, :] = (y * qk_scale).astype(o1_ref.dtype)
    _rope_store(o1_ref, O_BQR, y, cos, sin, NSA_HEADS, HEAD_DIM, qk_scale)
    y = _dot(wm_ref[R_K:R_K + 384, :], h)
    tm = y.shape[1]
    tok = pl.program_id(1) * tm + lax.broadcasted_iota(jnp.int32, (KSEL_DIM - HEAD_DIM, tm), 1)
    blk = lax.shift_right_arithmetic(tok, int(np.log2(NSA_SEL_BLOCK))) & (SEL_BLOCKS_PER_TILE - 1)
    row = lax.broadcasted_iota(jnp.int32, (KSEL_DIM - HEAD_DIM, tm), 0)
    onehot = jnp.where(blk == row, 1.0, 0.0)
    half = HEAD_DIM // 2
    for kh in range(6):
        x1 = y[kh * HEAD_DIM:kh * HEAD_DIM + half]
        x2 = y[kh * HEAD_DIM + half:(kh + 1) * HEAD_DIM]
        rows = [x1 * cos - x2 * sin, x2 * cos + x1 * sin]
        if kh in (2, 3):
            ks_ref[0, kh - 2] = jnp.concatenate(rows + [onehot], axis=0).T.astype(ks_ref.dtype)
        else:
            kk_ref[0, kh if kh < 2 else kh - 2] = jnp.concatenate(rows, axis=0).T.astype(kk_ref.dtype)
    y = _dot(wm_ref[R_V:R_V + 384, :], h)
    o1_ref[0, O_VV:O_VV + 384, :] = y.astype(o1_ref.dtype)
    o2_ref[0, P_C:P_C + 256, :] = _dot(wm_ref[R_C:R_C + 256, :], h)

    lat = _dot(wm_ref[R_CQA:R_CQA + MLA_Q_RANK, :], h)
    qm = _dot(wqb_ref[...], _rms_fm(lat, qn_ref[...]).astype(CDT))
    m_scale = MLA_QK_DIM ** -0.5 * LOG2E
    hr = MLA_ROPE_DIM // 2
    for hd in range(MLA_HEADS):
        r = hd * MLA_QK_DIM
        o1_ref[0, O_QM + r:O_QM + r + MLA_NOPE_DIM, :] = (
            qm[r:r + MLA_NOPE_DIM] * m_scale).astype(o1_ref.dtype)
        x1 = qm[r + MLA_NOPE_DIM:r + MLA_NOPE_DIM + hr]
        x2 = qm[r + MLA_NOPE_DIM + hr:r + MLA_QK_DIM]
        o1_ref[0, O_QM + r + MLA_NOPE_DIM:O_QM + r + MLA_NOPE_DIM + hr, :] = (
            (x1 * cosm - x2 * sinm) * m_scale).astype(o1_ref.dtype)
        o1_ref[0, O_QM + r + MLA_NOPE_DIM + hr:O_QM + r + MLA_QK_DIM, :] = (
            (x2 * cosm + x1 * sinm) * m_scale).astype(o1_ref.dtype)
    lat = _dot(wm_ref[R_CKV:R_CKV + MLA_KV_RANK, :], h)
    kv = _dot(wkvb_ref[...], _rms_fm(lat, kvn_ref[...]).astype(CDT))
    o1_ref[0, O_KV:O_KV + 1024, :] = kv.astype(o1_ref.dtype)
    y = _dot(wm_ref[R_CKR:R_CKR + 64, :], h)
    x1, x2 = y[0:hr], y[hr:2 * hr]
    k_pe = [x1 * cosm - x2 * sinm, x2 * cosm + x1 * sinm]
    o2_ref[0, P_GATE:P_GATE + 24, :] = jax.nn.sigmoid(y[MLA_ROPE_DIM:MLA_ROPE_DIM + 24])
    for hd in range(MLA_HEADS):
        k_nope = kv[hd * 2 * MLA_NOPE_DIM:hd * 2 * MLA_NOPE_DIM + MLA_NOPE_DIM]
        kc_ref[0, hd] = jnp.concatenate([k_nope] + k_pe, axis=0).T.astype(kc_ref.dtype)


def _proj(xT, g_col, wmT, cos, sin, cosm, sinm, qn, wqbT, kvn, wkvbT, tm):
    B, D, S = xT.shape
    tok = lambda rows: pl.BlockSpec((1, rows, tm), lambda b, i: (b, 0, i))
    return pl.pallas_call(
        _proj_kernel,
        grid=(B, S // tm),
        in_specs=[tok(D), _const_spec((D, 1)), _const_spec(wmT.shape),
                  tok(32), tok(32), tok(16), tok(16),
                  _const_spec(qn.shape), _const_spec(wqbT.shape),
                  _const_spec(kvn.shape), _const_spec(wkvbT.shape)],
        out_specs=[tok(N_OUT1), tok(N_OUT2),
                   pl.BlockSpec((1, 4, tm, HEAD_DIM), lambda b, i: (b, 0, i, 0)),
                   pl.BlockSpec((1, NSA_KV_GROUPS, tm, KSEL_DIM), lambda b, i: (b, 0, i, 0)),
                   pl.BlockSpec((1, MLA_HEADS, tm, MLA_QK_DIM), lambda b, i: (b, 0, i, 0))],
        out_shape=[jax.ShapeDtypeStruct((B, N_OUT1, S), CDT),
                   jax.ShapeDtypeStruct((B, N_OUT2, S), jnp.float32),
                   jax.ShapeDtypeStruct((B, 4, S, HEAD_DIM), CDT),
                   jax.ShapeDtypeStruct((B, NSA_KV_GROUPS, S, KSEL_DIM), CDT),
                   jax.ShapeDtypeStruct((B, MLA_HEADS, S, MLA_QK_DIM), CDT)],
        **_opts("mixer_proj", "parallel", "parallel"),
    )(xT, g_col, wmT, cos, sin, cosm, sinm, qn, wqbT, kvn, wkvbT)


def _compress_kernel(kr_ref, pe_ref, w1_ref, w2_ref, w2t_ref, tok_ref, fm_ref):
    kr = kr_ref[0, 0]
    nc = kr.shape[0]
    nj = nc // 4
    half = kr.shape[1]
    a = _dot((kr + pe_ref[0, 0:1, :]).astype(CDT), w1_ref[0, 0:half, :])
    bm = _dot((kr + pe_ref[0, 1:2, :]).astype(CDT), w1_ref[0, half:2 * half, :])
    wrap = pltpu.roll(bm[0:nj], nj - 1, 0)
    row = lax.broadcasted_iota(jnp.int32, (nj, 1), 0)
    wrap = jnp.where(row == nj - 1, 0.0, wrap)
    pre = a + jnp.concatenate([bm[nj:], wrap], axis=0)
    hid = (pre * jax.nn.sigmoid(pre)).astype(CDT)
    tok_ref[0, 0] = _dot(hid, w2_ref[0]).astype(tok_ref.dtype)
    fm_ref[0, 0] = _dot_nt(w2t_ref[0], hid).astype(fm_ref.dtype)


def _compress(kr, pe, w1, w2, w2t):
    B, _, NC, W = kr.shape
    return pl.pallas_call(
        _compress_kernel,
        grid=(B, 4),
        in_specs=[pl.BlockSpec((1, 1, NC, W), lambda b, n: (b, n, 0, 0)),
                  pl.BlockSpec((1, 2, W), lambda b, n: (n // 2, 0, 0)),
                  pl.BlockSpec((1, 2 * W, HEAD_DIM), lambda b, n: (n // 2, 0, 0)),
                  pl.BlockSpec((1, HEAD_DIM, HEAD_DIM), lambda b, n: (n // 2, 0, 0)),
                  pl.BlockSpec((1, HEAD_DIM, HEAD_DIM), lambda b, n: (n // 2, 0, 0))],
        out_specs=[pl.BlockSpec((1, 1, NC, HEAD_DIM), lambda b, n: (b, n, 0, 0)),
                   pl.BlockSpec((1, 1, HEAD_DIM, NC), lambda b, n: (b, n, 0, 0))],
        out_shape=[jax.ShapeDtypeStruct((B, 4, NC, HEAD_DIM), CDT),
                   jax.ShapeDtypeStruct((B, 4, HEAD_DIM, NC), CDT)],
        **_opts("nsa_compress", "parallel", "parallel"),
    )(kr, pe, w1, w2, w2t)


def _group_queries(q_ref, g):
    return jnp.concatenate([q_ref[0, (g * GROUP_HEADS + r) * HEAD_DIM:(g * GROUP_HEADS + r + 1) * HEAD_DIM, :]
                            for r in range(GROUP_HEADS)], axis=1)


def _with_ones(vt):
    return jnp.concatenate([vt, jnp.ones((ONES_ROWS, vt.shape[1]), vt.dtype)], axis=0)


def _softmax_probs(sc, m_run):
    m_new = jnp.maximum(m_run, jnp.max(sc, axis=0, keepdims=True))
    return m_new, jnp.exp2(sc - m_new).astype(CDT), jnp.exp2(m_run - m_new)


def _softmax_step(sc, m_run, acc, vt_aug):
    m_new, p, alpha = _softmax_probs(sc, m_run)
    return m_new, alpha * acc + _dot(vt_aug, p)


def _window_attention(q4, t4, k, vt, k0, window, sink4=None):
    span = k.shape[0]
    s = _dot(k, q4)
    kpos = k0 + lax.broadcasted_iota(jnp.int32, (span, 1), 0)
    s = jnp.where(kpos <= t4, jnp.where(kpos > t4 - window, s, -jnp.inf), -jnp.inf)
    m = jnp.max(s, axis=0, keepdims=True)
    if sink4 is not None:
        m = jnp.maximum(m, sink4)
    pv = _dot(_with_ones(vt), jnp.exp2(s - m).astype(CDT))
    den = pv[HEAD_DIM:HEAD_DIM + 1]
    if sink4 is not None:
        den = den + jnp.exp2(sink4 - m)
    return pv[0:HEAD_DIM] * (1.0 / den)


def _swa_kernel(sink_ref, q_ref, k_ref, vt_ref, o_ref, *, tq):
    g = pl.program_id(1)
    qs = pl.program_id(2) * tq
    q4 = _group_queries(q_ref, 0)
    t = qs + lax.broadcasted_iota(jnp.int32, (1, tq), 1)
    t4 = jnp.concatenate([t] * GROUP_HEADS, axis=1)
    sink4 = jnp.concatenate([jnp.full((1, tq), sink_ref[g * GROUP_HEADS + r] * LOG2E, jnp.float32)
                             for r in range(GROUP_HEADS)], axis=1)
    span = SWA_WINDOW + tq
    k0 = pl.multiple_of(jnp.maximum(qs - SWA_WINDOW, 0), LANES)
    o4 = _window_attention(q4, t4, k_ref[0, 0, pl.ds(k0, span), :], vt_ref[0, :, pl.ds(k0, span)],
                           k0, SWA_WINDOW, sink4)
    for r in range(GROUP_HEADS):
        o_ref[0, r * HEAD_DIM:(r + 1) * HEAD_DIM, :] = o4[:, r * tq:(r + 1) * tq].astype(o_ref.dtype)


def _swa(sinks, o1, kk, tq):
    B, _, S = o1.shape
    return pl.pallas_call(
        functools.partial(_swa_kernel, tq=tq),
        grid=(B, SWA_KV_HEADS, S // tq),
        in_specs=[pl.BlockSpec(memory_space=pltpu.SMEM),
                  pl.BlockSpec((1, GROUP_ROWS, tq), lambda b, g, i: (b, O_AQ // GROUP_ROWS + g, i)),
                  pl.BlockSpec((1, 1, S, HEAD_DIM), lambda b, g, i: (b, g, 0, 0)),
                  pl.BlockSpec((1, HEAD_DIM, S), lambda b, g, i: (b, O_VV // HEAD_DIM + g, 0))],
        out_specs=pl.BlockSpec((1, GROUP_ROWS, tq), lambda b, g, i: (b, g, i)),
        out_shape=jax.ShapeDtypeStruct((B, SWA_HEADS * HEAD_DIM, S), CDT),
        **_opts("swa_attn", "parallel", "parallel", "parallel"),
    )(sinks, o1, kk, o1)


def _nsa_kernel(q_ref, qr_ref, cend_ref, kc_ref, vct_ref, ks_ref, vst_ref, kw_ref, vwt_ref,
                gate_ref, o_ref, bias_ref, s_a, s_b, *, tq, n_sel):
    qs = pl.program_id(1) * tq
    nc = kc_ref.shape[2]
    nj = nc // 4
    G = NSA_KV_GROUPS
    t = qs + lax.broadcasted_iota(jnp.int32, (1, tq), 1)
    t2 = jnp.concatenate([t] * 2, axis=1)
    t4 = jnp.concatenate([t] * GROUP_HEADS, axis=1)
    jrow = lax.broadcasted_iota(jnp.int32, (nj, tq), 0)
    jf = jrow.astype(jnp.float32)
    cur = lax.shift_right_arithmetic(t, int(np.log2(NSA_SEL_BLOCK)))
    bonus = jnp.where((jrow == 0) | (jrow == cur) | (jrow == cur - 1), NSA_FORCE_BONUS, 0.0)
    span = NSA_WINDOW + tq
    kw0 = pl.multiple_of(jnp.maximum(qs - NSA_WINDOW, 0), LANES)

    qr4, fixed = [], []
    for g in range(G):
        q4 = _group_queries(q_ref, g)
        qr4.append(_group_queries(qr_ref, g))
        s = _dot(kc_ref[0, g], q4)
        s = jnp.where(cend_ref[...] <= t4, s, -jnp.inf)
        m = jnp.max(s, axis=0, keepdims=True)
        m = jnp.where(m == -jnp.inf, 0.0, m)
        e = jnp.exp2(s - m)
        den = jnp.sum(e, axis=0, keepdims=True)
        p = e * (1.0 / jnp.where(den > 0, den, 1.0))
        o_cmp = _dot(vct_ref[0, g], p.astype(CDT))

        psum = p[:, 0:tq]
        for r in range(1, GROUP_HEADS):
            psum = psum + p[:, r * tq:(r + 1) * tq]
        p3 = psum[3 * nj:4 * nj]
        prev = jnp.where(jrow == 0, 0.0, pltpu.roll(p3, 1, 0))
        imp = prev + 2.0 * (psum[0:nj] + psum[nj:2 * nj] + psum[2 * nj:3 * nj]) + p3

        score = jnp.where(jrow <= cur, imp + bonus, -jnp.inf)
        for _ in range(n_sel):
            mx = jnp.max(score, axis=0, keepdims=True)
            idx = jnp.min(jnp.where(score == mx, jf, float(nj)), axis=0, keepdims=True)
            idx = jnp.where(mx > -jnp.inf, idx, float(nj))
            score = jnp.where(jf == idx, -jnp.inf, score)
        bias_ref[g] = jnp.where(jrow <= cur, jnp.where(score == -jnp.inf, 0.0, MASK_VALUE), MASK_VALUE)

        o_win = _window_attention(qr4[g], t4, kw_ref[0, g, pl.ds(kw0, span), :],
                                  vwt_ref[0, g * HEAD_DIM:(g + 1) * HEAD_DIM, pl.ds(kw0, span)],
                                  kw0, NSA_WINDOW)
        parts = []
        for r in range(GROUP_HEADS):
            sl = slice(r * tq, (r + 1) * tq)
            g0 = gate_ref[0, g, 3 * r:3 * r + 1, :]
            g2 = gate_ref[0, g, 3 * r + 2:3 * r + 3, :]
            parts.append((g0 * o_cmp[:, sl], g2 * o_win[:, sl]))
        fixed.append(parts)

    def scores_to(kt, s_ref):
        k0 = pl.multiple_of(kt * TK_SEL, TK_SEL)
        for g in range(G):
            b8 = bias_ref[g, pl.ds(kt * SEL_BLOCKS_PER_TILE, SEL_BLOCKS_PER_TILE), :]
            b16 = jnp.concatenate([b8, jnp.zeros_like(b8)], axis=0)
            qa = jnp.concatenate([qr4[g], jnp.concatenate([b16] * GROUP_HEADS, axis=1).astype(CDT)],
                                 axis=0)
            k_tile = ks_ref[0, g, pl.ds(k0, TK_SEL), :]
            for hp in range(2):
                s_ref[2 * g + hp] = _dot(k_tile, qa[:, hp * 2 * tq:(hp + 1) * 2 * tq])

    def consume(kt, s_ref, state, diagonal=False):
        k0 = pl.multiple_of(kt * TK_SEL, TK_SEL)
        parts = []
        for c in range(2 * G):
            sc = s_ref[c]
            if diagonal:
                kpos = k0 + lax.broadcasted_iota(jnp.int32, (TK_SEL, 1), 0)
                sc = jnp.where(kpos <= t2, sc, MASK_VALUE)
            parts.append(_softmax_probs(sc, state[c][0]))
        out = []
        for g in range(G):
            vt_aug = _with_ones(vst_ref[0, g * HEAD_DIM:(g + 1) * HEAD_DIM, pl.ds(k0, TK_SEL)])
            for hp in range(2):
                m_new, p, alpha = parts[2 * g + hp]
                out.append((m_new, alpha * state[2 * g + hp][1] + _dot(vt_aug, p)))
        return tuple(out)

    def finish(state):
        for g in range(G):
            for r in range(GROUP_HEADS):
                _, acc = state[2 * g + r // 2]
                sl = slice((r % 2) * tq, (r % 2 + 1) * tq)
                o_sel = acc[0:HEAD_DIM, sl] * (1.0 / acc[HEAD_DIM:HEAD_DIM + 1, sl])
                g1 = gate_ref[0, g, 3 * r + 1:3 * r + 2, :]
                c_part, w_part = fixed[g][r]
                row = (g * GROUP_HEADS + r) * HEAD_DIM
                o_ref[0, row:row + HEAD_DIM, :] = (c_part + g1 * o_sel + w_part).astype(o_ref.dtype)

    def pair(j, state):
        scores_to(2 * j + 1, s_b)
        state = consume(2 * j, s_a, state)
        scores_to(2 * j + 2, s_a)
        return consume(2 * j + 1, s_b, state)

    init = tuple((jnp.full((1, 2 * tq), MASK_VALUE, jnp.float32),
                  jnp.zeros((HEAD_DIM + ONES_ROWS, 2 * tq), jnp.float32)) for _ in range(2 * G))
    n_full = qs // TK_SEL
    scores_to(0, s_a)
    state = lax.fori_loop(0, n_full // 2, pair, init)
    odd = n_full % 2 == 1

    @pl.when(odd)
    def _():
        scores_to(n_full, s_b)
        finish(consume(n_full, s_b, consume(n_full - 1, s_a, state), diagonal=True))

    @pl.when(jnp.logical_not(odd))
    def _():
        finish(consume(n_full, s_a, state, diagonal=True))


def _nsa(o1, cend, cmp_tok, cmp_fm, ks_aug, kk, gates, tq):
    B, _, S = o1.shape
    nc = cmp_tok.shape[2]
    nj = nc // 4
    n_sel = min(NSA_N_SEL, nj)
    G = NSA_KV_GROUPS
    rows = NSA_HEADS * HEAD_DIM
    vrows = G * HEAD_DIM
    return pl.pallas_call(
        functools.partial(_nsa_kernel, tq=tq, n_sel=n_sel),
        grid=(B, S // tq),
        in_specs=[pl.BlockSpec((1, rows, tq), lambda b, i: (b, O_BQ // rows, i)),
                  pl.BlockSpec((1, rows, tq), lambda b, i: (b, O_BQR // rows, i)),
                  _const_spec((nc, 1)),
                  pl.BlockSpec((1, G, nc, HEAD_DIM), lambda b, i: (b, 0, 0, 0)),
                  pl.BlockSpec((1, G, HEAD_DIM, nc), lambda b, i: (b, 1, 0, 0)),
                  pl.BlockSpec((1, G, S, KSEL_DIM), lambda b, i: (b, 0, 0, 0)),
                  pl.BlockSpec((1, vrows, S), lambda b, i: (b, O_VV // vrows + 1, 0)),
                  pl.BlockSpec((1, G, S, HEAD_DIM), lambda b, i: (b, 1, 0, 0)),
                  pl.BlockSpec((1, vrows, S), lambda b, i: (b, O_VV // vrows + 2, 0)),
                  pl.BlockSpec((1, G, 3 * GROUP_HEADS, tq), lambda b, i: (b, 0, 0, i))],
        out_specs=pl.BlockSpec((1, rows, tq), lambda b, i: (b, 0, i)),
        out_shape=jax.ShapeDtypeStruct((B, rows, S), CDT),
        scratch_shapes=[pltpu.VMEM((G, nj, tq), jnp.float32)]
                       + [pltpu.VMEM((2 * G, TK_SEL, 2 * tq), jnp.float32)] * 2,
        **_opts("nsa_attn", "parallel", "arbitrary"),
    )(o1, o1, cend, cmp_tok, cmp_fm, ks_aug, o1, kk, o1, gates)


def _mla_kernel(q_ref, k_ref, *rest, tq, hps):
    vt_refs, o_ref, s_a, s_b = rest[:hps], rest[hps], rest[hps + 1], rest[hps + 2]
    qs = pl.program_id(2) * tq
    q = [q_ref[0, h * MLA_QK_DIM:(h + 1) * MLA_QK_DIM, :] for h in range(hps)]
    t = qs + lax.broadcasted_iota(jnp.int32, (1, tq), 1)

    def scores_to(kt, s_ref):
        k0 = pl.multiple_of(kt * TK_MLA, TK_MLA)
        for h in range(hps):
            s_ref[h] = _dot(k_ref[0, h, pl.ds(k0, TK_MLA), :], q[h])

    def consume(kt, s_ref, state, diagonal=False):
        k0 = pl.multiple_of(kt * TK_MLA, TK_MLA)
        parts = []
        for h in range(hps):
            sc = s_ref[h]
            if diagonal:
                kpos = k0 + lax.broadcasted_iota(jnp.int32, (TK_MLA, 1), 0)
                sc = jnp.where(kpos <= t, sc, MASK_VALUE)
            parts.append(_softmax_probs(sc, state[h][0]))
        out = []
        for h in range(hps):
            m_new, p, alpha = parts[h]
            pv = _dot(_with_ones(vt_refs[h][0, :, pl.ds(k0, TK_MLA)]), p)
            out.append((m_new, alpha * state[h][1] + pv))
        return tuple(out)

    def finish(state):
        for h in range(hps):
            _, acc = state[h]
            o_ref[0, h * MLA_V_DIM:(h + 1) * MLA_V_DIM, :] = (
                acc[0:MLA_V_DIM] * (1.0 / acc[MLA_V_DIM:MLA_V_DIM + 1])).astype(o_ref.dtype)

    def pair(j, state):
        scores_to(2 * j + 1, s_b)
        state = consume(2 * j, s_a, state)
        scores_to(2 * j + 2, s_a)
        return consume(2 * j + 1, s_b, state)

    init = tuple((jnp.full((1, tq), MASK_VALUE, jnp.float32),
                  jnp.zeros((MLA_V_DIM + ONES_ROWS, tq), jnp.float32)) for _ in range(hps))
    n_full = qs // TK_MLA
    scores_to(0, s_a)
    state = lax.fori_loop(0, n_full // 2, pair, init)
    odd = n_full % 2 == 1

    @pl.when(odd)
    def _():
        scores_to(n_full, s_b)
        finish(consume(n_full, s_b, consume(n_full - 1, s_a, state), diagonal=True))

    @pl.when(jnp.logical_not(odd))
    def _():
        finish(consume(n_full, s_a, state, diagonal=True))


def _mla(o1, kcat, tq):
    B, _, S = o1.shape
    hps = MLA_HEADS_PER_STEP
    v_spec = lambda h: pl.BlockSpec(
        (1, MLA_V_DIM, S), lambda b, hg, i: (b, O_KV // MLA_V_DIM + 2 * (hg * hps + h) + 1, 0))
    return pl.pallas_call(
        functools.partial(_mla_kernel, tq=tq, hps=hps),
        grid=(B, MLA_HEADS // hps, S // tq),
        in_specs=[pl.BlockSpec((1, hps * MLA_QK_DIM, tq),
                               lambda b, hg, i: (b, O_QM // (hps * MLA_QK_DIM) + hg, i)),
                  pl.BlockSpec((1, hps, S, MLA_QK_DIM), lambda b, hg, i: (b, hg, 0, 0))]
                 + [v_spec(h) for h in range(hps)],
        out_specs=pl.BlockSpec((1, hps * MLA_V_DIM, tq), lambda b, hg, i: (b, hg, i)),
        out_shape=jax.ShapeDtypeStruct((B, MLA_HEADS * MLA_V_DIM, S), CDT),
        scratch_shapes=[pltpu.VMEM((hps, TK_MLA, tq), jnp.float32)] * 2,
        **_opts("mla_attn", "parallel", "parallel", "parallel"),
    )(o1, kcat, *([o1] * hps))


def _merge_kernel(x_ref, g_ref, wg_ref, oa_ref, ob_ref, oc_ref, wa_ref, wb_ref, wc_ref, wo_ref, o_ref):
    x = x_ref[0]
    d = x.shape[0]
    h = _rms_fm(x, g_ref[...]).astype(CDT)
    merged = None
    for i, (br_ref, w_ref) in enumerate(((oa_ref, wa_ref), (ob_ref, wb_ref), (oc_ref, wc_ref))):
        gate = jax.nn.sigmoid(_dot(wg_ref[i * d:(i + 1) * d, :], h))
        term = gate * _dot(w_ref[...], br_ref[0])
        merged = term if merged is None else merged + term
    o_ref[0] = x + _dot(wo_ref[...], merged.astype(CDT))


def _merge(xT, g_col, wgT, oa, ob, oc, waT, wbT, wcT, woT, tm):
    B, D, S = xT.shape
    tok = lambda rows: pl.BlockSpec((1, rows, tm), lambda b, i: (b, 0, i))
    return pl.pallas_call(
        _merge_kernel,
        grid=(B, S // tm),
        in_specs=[tok(D), _const_spec((D, 1)), _const_spec(wgT.shape),
                  tok(oa.shape[1]), tok(ob.shape[1]), tok(oc.shape[1]),
                  _const_spec(waT.shape), _const_spec(wbT.shape), _const_spec(wcT.shape),
                  _const_spec(woT.shape)],
        out_specs=tok(D),
        out_shape=jax.ShapeDtypeStruct((B, D, S), xT.dtype),
        **_opts("merge_out", "parallel", "parallel"),
    )(xT, g_col, wgT, oa, ob, oc, waT, wbT, wcT, woT)


def _mem_kv_kernel(mem_ref, g_ref, wk_ref, wvt_ref, k_ref, vt_ref):
    m = mem_ref[0]
    ms = jnp.mean(m * m, axis=-1, keepdims=True)
    hm = (m * lax.rsqrt(ms + NORM_EPS) * g_ref[...]).astype(CDT)
    k_ref[0] = _dot(hm, wk_ref[...]).astype(k_ref.dtype)
    vt_ref[0] = _dot_nt(wvt_ref[...], hm).astype(vt_ref.dtype)


def _mem_kv(mem, g_row, wk, wvT):
    B, M, D = mem.shape
    n = wk.shape[1]
    return pl.pallas_call(
        _mem_kv_kernel,
        grid=(B,),
        in_specs=[pl.BlockSpec((1, M, D), lambda b: (b, 0, 0)), _const_spec((1, D)),
                  _const_spec(wk.shape), _const_spec(wvT.shape)],
        out_specs=[pl.BlockSpec((1, M, n), lambda b: (b, 0, 0)),
                   pl.BlockSpec((1, n, M), lambda b: (b, 0, 0))],
        out_shape=[jax.ShapeDtypeStruct((B, M, n), CDT), jax.ShapeDtypeStruct((B, n, M), CDT)],
        **_opts("mem_kv", "parallel"),
    )(mem, g_row, wk, wvT)


def _xattn_kernel(x_ref, g_ref, wq_ref, k_ref, vt_ref, wo_ref, o_ref):
    x = x_ref[0]
    h = _rms_fm(x, g_ref[...]).astype(CDT)
    q = (_dot(wq_ref[...], h) * (XATTN_HEAD_DIM ** -0.5 * LOG2E)).astype(CDT)
    outs = []
    for hd in range(XATTN_HEADS):
        rows = slice(hd * XATTN_HEAD_DIM, (hd + 1) * XATTN_HEAD_DIM)
        s = _dot(k_ref[0, :, rows], q[rows])
        e = jnp.exp2(s - jnp.max(s, axis=0, keepdims=True)).astype(CDT)
        pv = _dot(_with_ones(vt_ref[0, rows, :]), e)
        outs.append((pv[0:XATTN_HEAD_DIM] * (1.0 / pv[XATTN_HEAD_DIM:XATTN_HEAD_DIM + 1])).astype(CDT))
    o_ref[0] = x + _dot(wo_ref[...], jnp.concatenate(outs, axis=0))


def _xattn(xT, g_col, wqT, kmem, vmemT, woT, tm):
    B, D, S = xT.shape
    M, n = kmem.shape[1], kmem.shape[2]
    tok = lambda rows: pl.BlockSpec((1, rows, tm), lambda b, i: (b, 0, i))
    return pl.pallas_call(
        _xattn_kernel,
        grid=(B, S // tm),
        in_specs=[tok(D), _const_spec((D, 1)), _const_spec(wqT.shape),
                  pl.BlockSpec((1, M, n), lambda b, i: (b, 0, 0)),
                  pl.BlockSpec((1, n, M), lambda b, i: (b, 0, 0)),
                  _const_spec(woT.shape)],
        out_specs=tok(D),
        out_shape=jax.ShapeDtypeStruct((B, D, S), xT.dtype),
        **_opts("xattn", "parallel", "parallel"),
    )(xT, g_col, wqT, kmem, vmemT, woT)


def _ffn_kernel(x_ref, g_ref, wgu_ref, wd_ref, o_ref, *, d_ff):
    x = x_ref[0]
    h = _rms_fm(x, g_ref[...]).astype(CDT)
    acc = x
    for c in range(d_ff // FF_CHUNK):
        r = c * FF_CHUNK
        gate = _dot(wgu_ref[r:r + FF_CHUNK, :], h)
        up = _dot(wgu_ref[d_ff + r:d_ff + r + FF_CHUNK, :], h)
        act = (gate * jax.nn.sigmoid(gate) * up).astype(CDT)
        acc = acc + _dot(wd_ref[:, r:r + FF_CHUNK], act)
    o_ref[0] = acc


def _ffn(xT, g_col, wguT, wdT, tm):
    B, D, S = xT.shape
    d_ff = wdT.shape[1]
    tok = lambda rows: pl.BlockSpec((1, rows, tm), lambda b, i: (b, 0, i))
    return pl.pallas_call(
        functools.partial(_ffn_kernel, d_ff=d_ff),
        grid=(B, S // tm),
        in_specs=[tok(D), _const_spec((D, 1)), _const_spec(wguT.shape), _const_spec(wdT.shape)],
        out_specs=tok(D),
        out_shape=jax.ShapeDtypeStruct((B, D, S), xT.dtype),
        **_opts("ffn", "parallel", "parallel"),
    )(xT, g_col, wguT, wdT)


def _rope_tables(positions, dim):
    half = dim // 2
    inv_freq = ROPE_THETA ** (-jnp.arange(half, dtype=jnp.float32) / half)
    ang = positions.astype(jnp.float32)[:, None, :] * inv_freq[None, :, None]
    return jnp.cos(ang), jnp.sin(ang)


def _pack_mixer_weight(w_in):
    off = np.cumsum((0, 512, 128, 128, 512, 128, 128, 128, 128, 128, 128, 24, 384, 256, 32))
    a_q, a_k, a_v, b_q, b_kc, b_vc, b_ks, b_vs, b_kw, b_vw, b_g, c_qa, c_kv, c_kr = [
        w_in[:, off[i]:off[i + 1]] for i in range(14)]
    packed = jnp.concatenate([a_q, b_q, a_k, b_ks, b_kw, a_v, b_vs, b_vw, b_kc, b_vc,
                              c_qa, c_kv, c_kr, b_g], axis=1)
    packed = jnp.pad(packed, ((0, 0), (0, N_MIX - packed.shape[1])))
    return packed.T.astype(CDT), w_in[:, off[14]:].T.astype(CDT)


def kernel(x, mem, positions, norm_mix, w_in, swa_sinks, nsa_pe_k, nsa_pe_v, nsa_wk1, nsa_wk2,
           nsa_wv1, nsa_wv2, mla_q_norm, mla_w_q_b, mla_kv_norm, mla_w_kv_b, w_br_a, w_br_b,
           w_br_c, w_out, norm_xattn, norm_mem, w_xq, w_xkv, w_xo, norm_ffn, w_gate_up, w_down,
           norm_final):
    B, S, D = x.shape
    depth = w_in.shape[0]
    nj = S // NSA_SEL_BLOCK
    nc = 4 * nj
    assert S % TK_SEL == 0 and S % TM_PROJ == 0 and S >= NSA_WINDOW + TQ_NSA
    col = lambda v: v.reshape(-1, 1)
    wt = lambda w: w.T.astype(CDT)

    cos, sin = _rope_tables(positions, HEAD_DIM)
    cosm, sinm = _rope_tables(positions, MLA_ROPE_DIM)
    rr, jj = np.divmod(np.arange(nc), nj)
    cend = jnp.asarray(((4 * jj + rr) * NSA_CMP_STRIDE + NSA_CMP_BLOCK - 1).reshape(nc, 1), jnp.int32)

    xT = _to_feature_major(x, TM_PROJ)
    for l in range(depth):
        wmT, wgT = _pack_mixer_weight(w_in[l])
        o1, o2, kk, ks_aug, kcat = _proj(
            xT, col(norm_mix[l]), wmT, cos, sin, cosm, sinm,
            col(mla_q_norm[l]), wt(mla_w_q_b[l]), col(mla_kv_norm[l]), wt(mla_w_kv_b[l]), TM_PROJ)
        kr = o2[:, P_C:P_C + 256].reshape(B, 4, HEAD_DIM, S).transpose(0, 1, 3, 2)
        kr = kr.reshape(B, 4, nj, 4, NSA_CMP_STRIDE * HEAD_DIM).transpose(0, 1, 3, 2, 4)
        kr = kr.reshape(B, 4, nc, NSA_CMP_STRIDE * HEAD_DIM)
        gates = o2[:, P_GATE:P_GATE + 24].reshape(B, NSA_KV_GROUPS, 3 * GROUP_HEADS, S)

        pe = jnp.stack([nsa_pe_k[l], nsa_pe_v[l]]).reshape(2, 2, NSA_CMP_STRIDE * HEAD_DIM)
        w1 = jnp.stack([nsa_wk1[l], nsa_wv1[l]]).astype(CDT)
        w2 = jnp.stack([nsa_wk2[l], nsa_wv2[l]]).astype(CDT)
        w2t = jnp.stack([nsa_wk2[l].T, nsa_wv2[l].T]).astype(CDT)
        cmp_tok, cmp_fm = _compress(kr, pe, w1, w2, w2t)

        o_a = _swa(swa_sinks[l], o1, kk, TQ_SWA)
        o_b = _nsa(o1, cend, cmp_tok, cmp_fm, ks_aug, kk, gates, TQ_NSA)
        o_c = _mla(o1, kcat, TQ_MLA)
        xT = _merge(xT, col(norm_mix[l]), wgT, o_a, o_b, o_c,
                    wt(w_br_a[l]), wt(w_br_b[l]), wt(w_br_c[l]), wt(w_out[l]), TM_PROJ)

        n_kv = XATTN_HEADS * XATTN_HEAD_DIM
        kmem, vmemT = _mem_kv(mem, norm_mem[l].reshape(1, D), w_xkv[l][:, :n_kv].astype(CDT),
                              wt(w_xkv[l][:, n_kv:]))
        xT = _xattn(xT, col(norm_xattn[l]), wt(w_xq[l]), kmem, vmemT, wt(w_xo[l]), TM_PROJ)
        xT = _ffn(xT, col(norm_ffn[l]), wt(w_gate_up[l]), wt(w_down[l]), TM_PROJ)
    return _final_norm(xT, col(norm_final), TM_PROJ)
```

```python
import functools

import numpy as np
import jax
import jax.numpy as jnp
from jax import lax
from jax.experimental import pallas as pl
from jax.experimental.pallas import tpu as pltpu

HEAD_DIM = 64
ROPE_THETA = 10000.0
NORM_EPS = 1e-6
SWA_HEADS = 8
SWA_KV_HEADS = 2
SWA_WINDOW = 128
NSA_HEADS = 8
NSA_KV_GROUPS = 2
NSA_CMP_BLOCK = 32
NSA_CMP_STRIDE = 16
NSA_SEL_BLOCK = 64
NSA_N_SEL = 16
NSA_WINDOW = 512
NSA_FORCE_BONUS = 1e4
MLA_HEADS = 8
MLA_Q_RANK = 384
MLA_KV_RANK = 256
MLA_NOPE_DIM = 64
MLA_ROPE_DIM = 32
MLA_V_DIM = 64
MLA_QK_DIM = MLA_NOPE_DIM + MLA_ROPE_DIM
XATTN_HEADS = 4
XATTN_HEAD_DIM = 128
N_BRANCH = 3
GROUP_HEADS = 4
GROUP_ROWS = GROUP_HEADS * HEAD_DIM
LOG2E = 1.4426950408889634

V7X_VMEM_LIMIT_BYTES = 56 * 1024 * 1024
LANES = 128
ONES_ROWS = 16

CDT = jnp.bfloat16

MASK_VALUE = -1e30

TM_PROJ = 512
TQ_SWA = 256
TQ_NSA = 128
TK_SEL = 512
SEL_BLOCKS_PER_TILE = TK_SEL // NSA_SEL_BLOCK
KSEL_DIM = HEAD_DIM + 16
TQ_MLA = 256
TK_MLA = 512
MLA_HEADS_PER_STEP = 4
QK_LEAD = 2
FF_CHUNK = 704

R_AQ, R_BQ, R_K, R_V, R_C, R_CQA, R_CKV, R_CKR = 0, 512, 1024, 1408, 1792, 2048, 2432, 2688
N_MIX = 2752
O_AQ, O_BQ, O_BQR, O_VV, O_QM, O_KV = 0, 512, 1024, 1536, 1920, 2688
N_OUT1 = 3712
P_C, P_GATE = 0, 256
N_OUT2 = 280


def _dot(a, b):
    return jnp.dot(a, b, preferred_element_type=jnp.float32)


def _dot_nt(a, b):
    return lax.dot_general(a, b, (((1,), (1,)), ((), ())), preferred_element_type=jnp.float32)


def _opts(name, *sem):
    return dict(name=name, compiler_params=pltpu.CompilerParams(
        dimension_semantics=sem, vmem_limit_bytes=V7X_VMEM_LIMIT_BYTES))


def _rms_fm(x, g_col):
    ms = jnp.mean(x * x, axis=0, keepdims=True)
    return x * lax.rsqrt(ms + NORM_EPS) * g_col


def _const_spec(shape):
    nd = len(shape)
    return pl.BlockSpec(shape, lambda *_: (0,) * nd)


def _to_feature_major_kernel(x_ref, o_ref):
    o_ref[0] = x_ref[0].T


def _to_feature_major(x, tm):
    B, S, D = x.shape
    return pl.pallas_call(
        _to_feature_major_kernel,
        grid=(B, S // tm),
        in_specs=[pl.BlockSpec((1, tm, D), lambda b, i: (b, i, 0))],
        out_specs=pl.BlockSpec((1, D, tm), lambda b, i: (b, 0, i)),
        out_shape=jax.ShapeDtypeStruct((B, D, S), x.dtype),
        **_opts("to_feature_major", "parallel", "parallel"),
    )(x)


def _final_norm_kernel(x_ref, g_ref, o_ref):
    o_ref[0] = _rms_fm(x_ref[0], g_ref[...]).T


def _final_norm(xT, g_col, tm):
    B, D, S = xT.shape
    return pl.pallas_call(
        _final_norm_kernel,
        grid=(B, S // tm),
        in_specs=[pl.BlockSpec((1, D, tm), lambda b, i: (b, 0, i)), _const_spec((D, 1))],
        out_specs=pl.BlockSpec((1, tm, D), lambda b, i: (b, i, 0)),
        out_shape=jax.ShapeDtypeStruct((B, S, D), xT.dtype),
        **_opts("final_norm", "parallel", "parallel"),
    )(xT, g_col)


def _rope_store(o_ref, row0, y, cos, sin, n_heads, head_dim, scale):
    half = head_dim // 2
    for h in range(n_heads):
        x1 = y[h * head_dim:h * head_dim + half]
        x2 = y[h * head_dim + half:(h + 1) * head_dim]
        r = row0 + h * head_dim
        o_ref[0, r:r + half, :] = ((x1 * cos - x2 * sin) * scale).astype(o_ref.dtype)
        o_ref[0, r + half:r + head_dim, :] = ((x2 * cos + x1 * sin) * scale).astype(o_ref.dtype)


def _proj_kernel(x_ref, g_ref, wm_ref, cos_ref, sin_ref, cosm_ref, sinm_ref,
                 qn_ref, wqb_ref, kvn_ref, wkvb_ref, o1_ref, o2_ref, kk_ref, ks_ref, kc_ref):
    h = _rms_fm(x_ref[0], g_ref[...]).astype(CDT)
    cos, sin = cos_ref[0], sin_ref[0]
    cosm, sinm = cosm_ref[0], sinm_ref[0]
    qk_scale = HEAD_DIM ** -0.5 * LOG2E

    y = _dot(wm_ref[R_AQ:R_AQ + 512, :], h)
    _rope_store(o1_ref, O_AQ, y, cos, sin, SWA_HEADS, HEAD_DIM, qk_scale)
    y = _dot(wm_ref[R_BQ:R_BQ + 512, :], h)
    o1_ref[0, O_BQ:O_BQ + 512, :] = (y * qk_scale).astype(o1_ref.dtype)
    _rope_store(o1_ref, O_BQR, y, cos, sin, NSA_HEADS, HEAD_DIM, qk_scale)
    y = _dot(wm_ref[R_K:R_K + 384, :], h)
    tm = y.shape[1]
    tok = pl.program_id(1) * tm + lax.broadcasted_iota(jnp.int32, (KSEL_DIM - HEAD_DIM, tm), 1)
    blk = lax.shift_right_arithmetic(tok, int(np.log2(NSA_SEL_BLOCK))) & (SEL_BLOCKS_PER_TILE - 1)
    row = lax.broadcasted_iota(jnp.int32, (KSEL_DIM - HEAD_DIM, tm), 0)
    onehot = jnp.where(blk == row, 1.0, 0.0)
    half = HEAD_DIM // 2
    for kh in range(6):
        x1 = y[kh * HEAD_DIM:kh * HEAD_DIM + half]
        x2 = y[kh * HEAD_DIM + half:(kh + 1) * HEAD_DIM]
        rows = [x1 * cos - x2 * sin, x2 * cos + x1 * sin]
        if kh in (2, 3):
            ks_ref[0, kh - 2] = jnp.concatenate(rows + [onehot], axis=0).T.astype(ks_ref.dtype)
        else:
            kk_ref[0, kh if kh < 2 else kh - 2] = jnp.concatenate(rows, axis=0).T.astype(kk_ref.dtype)
    y = _dot(wm_ref[R_V:R_V + 384, :], h)
    o1_ref[0, O_VV:O_VV + 384, :] = y.astype(o1_ref.dtype)
    o2_ref[0, P_C:P_C + 256, :] = _dot(wm_ref[R_C:R_C + 256, :], h)

    lat = _dot(wm_ref[R_CQA:R_CQA + MLA_Q_RANK, :], h)
    qm = _dot(wqb_ref[...], _rms_fm(lat, qn_ref[...]).astype(CDT))
    m_scale = MLA_QK_DIM ** -0.5 * LOG2E
    hr = MLA_ROPE_DIM // 2
    for hd in range(MLA_HEADS):
        r = hd * MLA_QK_DIM
        o1_ref[0, O_QM + r:O_QM + r + MLA_NOPE_DIM, :] = (
            qm[r:r + MLA_NOPE_DIM] * m_scale).astype(o1_ref.dtype)
        x1 = qm[r + MLA_NOPE_DIM:r + MLA_NOPE_DIM + hr]
        x2 = qm[r + MLA_NOPE_DIM + hr:r + MLA_QK_DIM]
        o1_ref[0, O_QM + r + MLA_NOPE_DIM:O_QM + r + MLA_NOPE_DIM + hr, :] = (
            (x1 * cosm - x2 * sinm) * m_scale).astype(o1_ref.dtype)
        o1_ref[0, O_QM + r + MLA_NOPE_DIM + hr:O_QM + r + MLA_QK_DIM, :] = (
            (x2 * cosm + x1 * sinm) * m_scale).astype(o1_ref.dtype)
    lat = _dot(wm_ref[R_CKV:R_CKV + MLA_KV_RANK, :], h)
    kv = _dot(wkvb_ref[...], _rms_fm(lat, kvn_ref[...]).astype(CDT))
    o1_ref[0, O_KV:O_KV + 1024, :] = kv.astype(o1_ref.dtype)
    y = _dot(wm_ref[R_CKR:R_CKR + 64, :], h)
    x1, x2 = y[0:hr], y[hr:2 * hr]
    k_pe = [x1 * cosm - x2 * sinm, x2 * cosm + x1 * sinm]
    o2_ref[0, P_GATE:P_GATE + 24, :] = jax.nn.sigmoid(y[MLA_ROPE_DIM:MLA_ROPE_DIM + 24])
    for hd in range(MLA_HEADS):
        k_nope = kv[hd * 2 * MLA_NOPE_DIM:hd * 2 * MLA_NOPE_DIM + MLA_NOPE_DIM]
        kc_ref[0, hd] = jnp.concatenate([k_nope] + k_pe, axis=0).T.astype(kc_ref.dtype)


def _proj(xT, g_col, wmT, cos, sin, cosm, sinm, qn, wqbT, kvn, wkvbT, tm):
    B, D, S = xT.shape
    tok = lambda rows: pl.BlockSpec((1, rows, tm), lambda b, i: (b, 0, i))
    return pl.pallas_call(
        _proj_kernel,
        grid=(B, S // tm),
        in_specs=[tok(D), _const_spec((D, 1)), _const_spec(wmT.shape),
                  tok(32), tok(32), tok(16), tok(16),
                  _const_spec(qn.shape), _const_spec(wqbT.shape),
                  _const_spec(kvn.shape), _const_spec(wkvbT.shape)],
        out_specs=[tok(N_OUT1), tok(N_OUT2),
                   pl.BlockSpec((1, 4, tm, HEAD_DIM), lambda b, i: (b, 0, i, 0)),
                   pl.BlockSpec((1, NSA_KV_GROUPS, tm, KSEL_DIM), lambda b, i: (b, 0, i, 0)),
                   pl.BlockSpec((1, MLA_HEADS, tm, MLA_QK_DIM), lambda b, i: (b, 0, i, 0))],
        out_shape=[jax.ShapeDtypeStruct((B, N_OUT1, S), CDT),
                   jax.ShapeDtypeStruct((B, N_OUT2, S), jnp.float32),
                   jax.ShapeDtypeStruct((B, 4, S, HEAD_DIM), CDT),
                   jax.ShapeDtypeStruct((B, NSA_KV_GROUPS, S, KSEL_DIM), CDT),
                   jax.ShapeDtypeStruct((B, MLA_HEADS, S, MLA_QK_DIM), CDT)],
        **_opts("mixer_proj", "parallel", "parallel"),
    )(xT, g_col, wmT, cos, sin, cosm, sinm, qn, wqbT, kvn, wkvbT)


def _compress_kernel(kr_ref, pe_ref, w1_ref, w2_ref, w2t_ref, tok_ref, fm_ref):
    kr = kr_ref[0, 0]
    nc = kr.shape[0]
    nj = nc // 4
    half = kr.shape[1]
    a = _dot((kr + pe_ref[0, 0:1, :]).astype(CDT), w1_ref[0, 0:half, :])
    bm = _dot((kr + pe_ref[0, 1:2, :]).astype(CDT), w1_ref[0, half:2 * half, :])
    wrap = pltpu.roll(bm[0:nj], nj - 1, 0)
    row = lax.broadcasted_iota(jnp.int32, (nj, 1), 0)
    wrap = jnp.where(row == nj - 1, 0.0, wrap)
    pre = a + jnp.concatenate([bm[nj:], wrap], axis=0)
    hid = (pre * jax.nn.sigmoid(pre)).astype(CDT)
    tok_ref[0, 0] = _dot(hid, w2_ref[0]).astype(tok_ref.dtype)
    fm_ref[0, 0] = _dot_nt(w2t_ref[0], hid).astype(fm_ref.dtype)


def _compress(kr, pe, w1, w2, w2t):
    B, _, NC, W = kr.shape
    return pl.pallas_call(
        _compress_kernel,
        grid=(B, 4),
        in_specs=[pl.BlockSpec((1, 1, NC, W), lambda b, n: (b, n, 0, 0)),
                  pl.BlockSpec((1, 2, W), lambda b, n: (n // 2, 0, 0)),
                  pl.BlockSpec((1, 2 * W, HEAD_DIM), lambda b, n: (n // 2, 0, 0)),
                  pl.BlockSpec((1, HEAD_DIM, HEAD_DIM), lambda b, n: (n // 2, 0, 0)),
                  pl.BlockSpec((1, HEAD_DIM, HEAD_DIM), lambda b, n: (n // 2, 0, 0))],
        out_specs=[pl.BlockSpec((1, 1, NC, HEAD_DIM), lambda b, n: (b, n, 0, 0)),
                   pl.BlockSpec((1, 1, HEAD_DIM, NC), lambda b, n: (b, n, 0, 0))],
        out_shape=[jax.ShapeDtypeStruct((B, 4, NC, HEAD_DIM), CDT),
                   jax.ShapeDtypeStruct((B, 4, HEAD_DIM, NC), CDT)],
        **_opts("nsa_compress", "parallel", "parallel"),
    )(kr, pe, w1, w2, w2t)


def _group_queries(q_ref, g):
    return jnp.concatenate([q_ref[0, (g * GROUP_HEADS + r) * HEAD_DIM:(g * GROUP_HEADS + r + 1) * HEAD_DIM, :]
                            for r in range(GROUP_HEADS)], axis=1)


def _with_ones(vt):
    return jnp.concatenate([vt, jnp.ones((ONES_ROWS, vt.shape[1]), vt.dtype)], axis=0)


def _softmax_probs(sc, m_run):
    m_new = jnp.maximum(m_run, jnp.max(sc, axis=0, keepdims=True))
    return m_new, jnp.exp2(sc - m_new).astype(CDT), jnp.exp2(m_run - m_new)


def _softmax_step(sc, m_run, acc, vt_aug):
    m_new, p, alpha = _softmax_probs(sc, m_run)
    return m_new, alpha * acc + _dot(vt_aug, p)


def _window_attention(q4, t4, k, vt, k0, window, sink4=None, unclipped=False):
    span = k.shape[0]
    tq = q4.shape[1] // GROUP_HEADS
    s = _dot(k, q4)
    chunks = []
    for c in range(span // LANES):
        sc = s[c * LANES:(c + 1) * LANES]
        kpos = k0 + c * LANES + lax.broadcasted_iota(jnp.int32, (LANES, 1), 0)
        if not unclipped or (c + 1) * LANES - 1 > window:
            sc = jnp.where(kpos <= t4, sc, -jnp.inf)
        if not unclipped or c * LANES <= tq - 1:
            sc = jnp.where(kpos > t4 - window, sc, -jnp.inf)
        chunks.append(sc)
    s = jnp.concatenate(chunks, axis=0)
    m = jnp.max(s, axis=0, keepdims=True)
    if sink4 is not None:
        m = jnp.maximum(m, sink4)
    pv = _dot(_with_ones(vt), jnp.exp2(s - m).astype(CDT))
    den = pv[HEAD_DIM:HEAD_DIM + 1]
    if sink4 is not None:
        den = den + jnp.exp2(sink4 - m)
    return pv[0:HEAD_DIM] * (1.0 / den)


def _swa_kernel(sink_ref, q_ref, k_ref, vt_ref, o_ref, *, tq):
    g = pl.program_id(1)
    qs = pl.program_id(2) * tq
    q4 = _group_queries(q_ref, 0)
    t = qs + lax.broadcasted_iota(jnp.int32, (1, tq), 1)
    t4 = jnp.concatenate([t] * GROUP_HEADS, axis=1)
    sink4 = jnp.concatenate([jnp.full((1, tq), sink_ref[g * GROUP_HEADS + r] * LOG2E, jnp.float32)
                             for r in range(GROUP_HEADS)], axis=1)
    span = SWA_WINDOW + tq
    k0 = pl.multiple_of(jnp.maximum(qs - SWA_WINDOW, 0), LANES)
    window_fn = functools.partial(_window_attention, q4, t4, k_ref[0, 0, pl.ds(k0, span), :],
                                  vt_ref[0, :, pl.ds(k0, span)], k0, SWA_WINDOW, sink4)
    o4 = lax.cond(qs >= SWA_WINDOW, functools.partial(window_fn, unclipped=True),
                  functools.partial(window_fn, unclipped=False))
    for r in range(GROUP_HEADS):
        o_ref[0, r * HEAD_DIM:(r + 1) * HEAD_DIM, :] = o4[:, r * tq:(r + 1) * tq].astype(o_ref.dtype)


def _swa(sinks, o1, kk, tq):
    B, _, S = o1.shape
    return pl.pallas_call(
        functools.partial(_swa_kernel, tq=tq),
        grid=(B, SWA_KV_HEADS, S // tq),
        in_specs=[pl.BlockSpec(memory_space=pltpu.SMEM),
                  pl.BlockSpec((1, GROUP_ROWS, tq), lambda b, g, i: (b, O_AQ // GROUP_ROWS + g, i)),
                  pl.BlockSpec((1, 1, S, HEAD_DIM), lambda b, g, i: (b, g, 0, 0)),
                  pl.BlockSpec((1, HEAD_DIM, S), lambda b, g, i: (b, O_VV // HEAD_DIM + g, 0))],
        out_specs=pl.BlockSpec((1, GROUP_ROWS, tq), lambda b, g, i: (b, g, i)),
        out_shape=jax.ShapeDtypeStruct((B, SWA_HEADS * HEAD_DIM, S), CDT),
        **_opts("swa_attn", "parallel", "parallel", "parallel"),
    )(sinks, o1, kk, o1)


def _nsa_kernel(q_ref, qr_ref, cend_ref, kc_ref, vct_ref, ks_ref, vst_ref, kw_ref, vwt_ref,
                gate_ref, o_ref, bias_ref, s_a, s_b, *, tq, n_sel):
    qs = pl.program_id(1) * tq
    nc = kc_ref.shape[2]
    nj = nc // 4
    G = NSA_KV_GROUPS
    t = qs + lax.broadcasted_iota(jnp.int32, (1, tq), 1)
    t2 = jnp.concatenate([t] * 2, axis=1)
    t4 = jnp.concatenate([t] * GROUP_HEADS, axis=1)
    jrow = lax.broadcasted_iota(jnp.int32, (nj, tq), 0)
    jf = jrow.astype(jnp.float32)
    cur = lax.shift_right_arithmetic(t, int(np.log2(NSA_SEL_BLOCK)))
    bonus = jnp.where((jrow == 0) | (jrow == cur) | (jrow == cur - 1), NSA_FORCE_BONUS, 0.0)
    span = NSA_WINDOW + tq
    kw0 = pl.multiple_of(jnp.maximum(qs - NSA_WINDOW, 0), LANES)

    qr4, fixed = [], []
    for g in range(G):
        q4 = _group_queries(q_ref, g)
        qr4.append(_group_queries(qr_ref, g))
        s = _dot(kc_ref[0, g], q4)
        s = jnp.where(cend_ref[...] <= t4, s, -jnp.inf)
        m = jnp.max(s, axis=0, keepdims=True)
        m = jnp.where(m == -jnp.inf, 0.0, m)
        e = jnp.exp2(s - m)
        den = jnp.sum(e, axis=0, keepdims=True)
        p = e * (1.0 / jnp.where(den > 0, den, 1.0))
        o_cmp = _dot(vct_ref[0, g], p.astype(CDT))

        psum = p[:, 0:tq]
        for r in range(1, GROUP_HEADS):
            psum = psum + p[:, r * tq:(r + 1) * tq]
        p3 = psum[3 * nj:4 * nj]
        prev = jnp.where(jrow == 0, 0.0, pltpu.roll(p3, 1, 0))
        imp = prev + 2.0 * (psum[0:nj] + psum[nj:2 * nj] + psum[2 * nj:3 * nj]) + p3

        score = jnp.where(jrow <= cur, imp + bonus, -jnp.inf)
        for _ in range(n_sel):
            mx = jnp.max(score, axis=0, keepdims=True)
            idx = jnp.min(jnp.where(score == mx, jf, float(nj)), axis=0, keepdims=True)
            idx = jnp.where(mx > -jnp.inf, idx, float(nj))
            score = jnp.where(jf == idx, -jnp.inf, score)
        bias_ref[g] = jnp.where(jrow <= cur, jnp.where(score == -jnp.inf, 0.0, MASK_VALUE), MASK_VALUE)

        window_fn = functools.partial(
            _window_attention, qr4[g], t4, kw_ref[0, g, pl.ds(kw0, span), :],
            vwt_ref[0, g * HEAD_DIM:(g + 1) * HEAD_DIM, pl.ds(kw0, span)], kw0, NSA_WINDOW, None)
        o_win = lax.cond(qs >= NSA_WINDOW, functools.partial(window_fn, unclipped=True),
                         functools.partial(window_fn, unclipped=False))
        parts = []
        for r in range(GROUP_HEADS):
            sl = slice(r * tq, (r + 1) * tq)
            g0 = gate_ref[0, g, 3 * r:3 * r + 1, :]
            g2 = gate_ref[0, g, 3 * r + 2:3 * r + 3, :]
            parts.append((g0 * o_cmp[:, sl], g2 * o_win[:, sl]))
        fixed.append(parts)

    def augmented_queries(kt):
        out = []
        for g in range(G):
            b8 = bias_ref[g, pl.ds(kt * SEL_BLOCKS_PER_TILE, SEL_BLOCKS_PER_TILE), :]
            b16 = jnp.concatenate([b8, jnp.zeros_like(b8)], axis=0)
            out.append(jnp.concatenate(
                [qr4[g], jnp.concatenate([b16] * GROUP_HEADS, axis=1).astype(CDT)], axis=0))
        return out

    def score_chain(kt, s_ref, qa, c):
        g, hp = divmod(c, 2)
        k0 = pl.multiple_of(kt * TK_SEL, TK_SEL)
        s_ref[c] = _dot(ks_ref[0, g, pl.ds(k0, TK_SEL), :],
                        qa[g][:, hp * 2 * tq:(hp + 1) * 2 * tq])

    def consume_chain(kt, s_ref, c, st, diagonal):
        g = c // 2
        k0 = pl.multiple_of(kt * TK_SEL, TK_SEL)
        sc = s_ref[c]
        if diagonal:
            kpos = k0 + lax.broadcasted_iota(jnp.int32, (TK_SEL, 1), 0)
            sc = jnp.where(kpos <= t2, sc, MASK_VALUE)
        vt_aug = _with_ones(vst_ref[0, g * HEAD_DIM:(g + 1) * HEAD_DIM, pl.ds(k0, TK_SEL)])
        return _softmax_step(sc, st[0], st[1], vt_aug)

    def step(kt, s_cur, s_nxt, state, diagonal=False):
        n_chain = 2 * G
        if s_nxt is not None:
            qa = augmented_queries(kt + 1)
            for c in range(QK_LEAD):
                score_chain(kt + 1, s_nxt, qa, c)
        out = []
        for c in range(n_chain):
            out.append(consume_chain(kt, s_cur, c, state[c], diagonal))
            if s_nxt is not None and c + QK_LEAD < n_chain:
                score_chain(kt + 1, s_nxt, qa, c + QK_LEAD)
        return tuple(out)

    def finish(state):
        for g in range(G):
            for r in range(GROUP_HEADS):
                _, acc = state[2 * g + r // 2]
                sl = slice((r % 2) * tq, (r % 2 + 1) * tq)
                o_sel = acc[0:HEAD_DIM, sl] * (1.0 / acc[HEAD_DIM:HEAD_DIM + 1, sl])
                g1 = gate_ref[0, g, 3 * r + 1:3 * r + 2, :]
                c_part, w_part = fixed[g][r]
                row = (g * GROUP_HEADS + r) * HEAD_DIM
                o_ref[0, row:row + HEAD_DIM, :] = (c_part + g1 * o_sel + w_part).astype(o_ref.dtype)

    def pair(j, state):
        return step(2 * j + 1, s_b, s_a, step(2 * j, s_a, s_b, state))

    init = tuple((jnp.full((1, 2 * tq), MASK_VALUE, jnp.float32),
                  jnp.zeros((HEAD_DIM + ONES_ROWS, 2 * tq), jnp.float32)) for _ in range(2 * G))
    n_full = qs // TK_SEL
    qa0 = augmented_queries(0)
    for c in range(2 * G):
        score_chain(0, s_a, qa0, c)
    state = lax.fori_loop(0, n_full // 2, pair, init)
    odd = n_full % 2 == 1

    @pl.when(odd)
    def _():
        finish(step(n_full, s_b, None, step(n_full - 1, s_a, s_b, state), diagonal=True))

    @pl.when(jnp.logical_not(odd))
    def _():
        finish(step(n_full, s_a, None, state, diagonal=True))


def _nsa(o1, cend, cmp_tok, cmp_fm, ks_aug, kk, gates, tq):
    B, _, S = o1.shape
    nc = cmp_tok.shape[2]
    nj = nc // 4
    n_sel = min(NSA_N_SEL, nj)
    G = NSA_KV_GROUPS
    rows = NSA_HEADS * HEAD_DIM
    vrows = G * HEAD_DIM
    return pl.pallas_call(
        functools.partial(_nsa_kernel, tq=tq, n_sel=n_sel),
        grid=(B, S // tq),
        in_specs=[pl.BlockSpec((1, rows, tq), lambda b, i: (b, O_BQ // rows, i)),
                  pl.BlockSpec((1, rows, tq), lambda b, i: (b, O_BQR // rows, i)),
                  _const_spec((nc, 1)),
                  pl.BlockSpec((1, G, nc, HEAD_DIM), lambda b, i: (b, 0, 0, 0)),
                  pl.BlockSpec((1, G, HEAD_DIM, nc), lambda b, i: (b, 1, 0, 0)),
                  pl.BlockSpec((1, G, S, KSEL_DIM), lambda b, i: (b, 0, 0, 0)),
                  pl.BlockSpec((1, vrows, S), lambda b, i: (b, O_VV // vrows + 1, 0)),
                  pl.BlockSpec((1, G, S, HEAD_DIM), lambda b, i: (b, 1, 0, 0)),
                  pl.BlockSpec((1, vrows, S), lambda b, i: (b, O_VV // vrows + 2, 0)),
                  pl.BlockSpec((1, G, 3 * GROUP_HEADS, tq), lambda b, i: (b, 0, 0, i))],
        out_specs=pl.BlockSpec((1, rows, tq), lambda b, i: (b, 0, i)),
        out_shape=jax.ShapeDtypeStruct((B, rows, S), CDT),
        scratch_shapes=[pltpu.VMEM((G, nj, tq), jnp.float32)]
                       + [pltpu.VMEM((2 * G, TK_SEL, 2 * tq), jnp.float32)] * 2,
        **_opts("nsa_attn", "parallel", "arbitrary"),
    )(o1, o1, cend, cmp_tok, cmp_fm, ks_aug, o1, kk, o1, gates)


def _mla_kernel(q_ref, k_ref, *rest, tq, hps):
    vt_refs, o_ref, s_a, s_b = rest[:hps], rest[hps], rest[hps + 1], rest[hps + 2]
    qs = pl.program_id(2) * tq
    q = [q_ref[0, h * MLA_QK_DIM:(h + 1) * MLA_QK_DIM, :] for h in range(hps)]
    t = qs + lax.broadcasted_iota(jnp.int32, (1, tq), 1)

    def score_chain(kt, s_ref, h):
        k0 = pl.multiple_of(kt * TK_MLA, TK_MLA)
        s_ref[h] = _dot(k_ref[0, h, pl.ds(k0, TK_MLA), :], q[h])

    def consume_chain(kt, s_ref, h, st, diagonal):
        k0 = pl.multiple_of(kt * TK_MLA, TK_MLA)
        sc = s_ref[h]
        if diagonal:
            kpos = k0 + lax.broadcasted_iota(jnp.int32, (TK_MLA, 1), 0)
            sc = jnp.where(kpos <= t, sc, MASK_VALUE)
        return _softmax_step(sc, st[0], st[1], _with_ones(vt_refs[h][0, :, pl.ds(k0, TK_MLA)]))

    def step(kt, s_cur, s_nxt, state, diagonal=False):
        lead = QK_LEAD
        if s_nxt is not None:
            for h in range(lead):
                score_chain(kt + 1, s_nxt, h)
        out = []
        for h in range(hps):
            out.append(consume_chain(kt, s_cur, h, state[h], diagonal))
            if s_nxt is not None and h + lead < hps:
                score_chain(kt + 1, s_nxt, h + lead)
        return tuple(out)

    def finish(state):
        for h in range(hps):
            _, acc = state[h]
            o_ref[0, h * MLA_V_DIM:(h + 1) * MLA_V_DIM, :] = (
                acc[0:MLA_V_DIM] * (1.0 / acc[MLA_V_DIM:MLA_V_DIM + 1])).astype(o_ref.dtype)

    def pair(j, state):
        return step(2 * j + 1, s_b, s_a, step(2 * j, s_a, s_b, state))

    init = tuple((jnp.full((1, tq), MASK_VALUE, jnp.float32),
                  jnp.zeros((MLA_V_DIM + ONES_ROWS, tq), jnp.float32)) for _ in range(hps))
    n_full = qs // TK_MLA
    for h in range(hps):
        score_chain(0, s_a, h)
    state = lax.fori_loop(0, n_full // 2, pair, init)
    odd = n_full % 2 == 1

    @pl.when(odd)
    def _():
        finish(step(n_full, s_b, None, step(n_full - 1, s_a, s_b, state), diagonal=True))

    @pl.when(jnp.logical_not(odd))
    def _():
        finish(step(n_full, s_a, None, state, diagonal=True))


def _mla(o1, kcat, tq):
    B, _, S = o1.shape
    hps = MLA_HEADS_PER_STEP
    v_spec = lambda h: pl.BlockSpec(
        (1, MLA_V_DIM, S), lambda b, hg, i: (b, O_KV // MLA_V_DIM + 2 * (hg * hps + h) + 1, 0))
    return pl.pallas_call(
        functools.partial(_mla_kernel, tq=tq, hps=hps),
        grid=(B, MLA_HEADS // hps, S // tq),
        in_specs=[pl.BlockSpec((1, hps * MLA_QK_DIM, tq),
                               lambda b, hg, i: (b, O_QM // (hps * MLA_QK_DIM) + hg, i)),
                  pl.BlockSpec((1, hps, S, MLA_QK_DIM), lambda b, hg, i: (b, hg, 0, 0))]
                 + [v_spec(h) for h in range(hps)],
        out_specs=pl.BlockSpec((1, hps * MLA_V_DIM, tq), lambda b, hg, i: (b, hg, i)),
        out_shape=jax.ShapeDtypeStruct((B, MLA_HEADS * MLA_V_DIM, S), CDT),
        scratch_shapes=[pltpu.VMEM((hps, TK_MLA, tq), jnp.float32)] * 2,
        **_opts("mla_attn", "parallel", "parallel", "parallel"),
    )(o1, kcat, *([o1] * hps))


def _merge_kernel(x_ref, g_ref, wg_ref, oa_ref, ob_ref, oc_ref, wa_ref, wb_ref, wc_ref, wo_ref, o_ref):
    x = x_ref[0]
    d = x.shape[0]
    h = _rms_fm(x, g_ref[...]).astype(CDT)
    merged = None
    for i, (br_ref, w_ref) in enumerate(((oa_ref, wa_ref), (ob_ref, wb_ref), (oc_ref, wc_ref))):
        gate = jax.nn.sigmoid(_dot(wg_ref[i * d:(i + 1) * d, :], h))
        term = gate * _dot(w_ref[...], br_ref[0])
        merged = term if merged is None else merged + term
    o_ref[0] = x + _dot(wo_ref[...], merged.astype(CDT))


def _merge(xT, g_col, wgT, oa, ob, oc, waT, wbT, wcT, woT, tm):
    B, D, S = xT.shape
    tok = lambda rows: pl.BlockSpec((1, rows, tm), lambda b, i: (b, 0, i))
    return pl.pallas_call(
        _merge_kernel,
        grid=(B, S // tm),
        in_specs=[tok(D), _const_spec((D, 1)), _const_spec(wgT.shape),
                  tok(oa.shape[1]), tok(ob.shape[1]), tok(oc.shape[1]),
                  _const_spec(waT.shape), _const_spec(wbT.shape), _const_spec(wcT.shape),
                  _const_spec(woT.shape)],
        out_specs=tok(D),
        out_shape=jax.ShapeDtypeStruct((B, D, S), xT.dtype),
        **_opts("merge_out", "parallel", "parallel"),
    )(xT, g_col, wgT, oa, ob, oc, waT, wbT, wcT, woT)


def _mem_kv_kernel(mem_ref, g_ref, wk_ref, wvt_ref, k_ref, vt_ref):
    m = mem_ref[0]
    ms = jnp.mean(m * m, axis=-1, keepdims=True)
    hm = (m * lax.rsqrt(ms + NORM_EPS) * g_ref[...]).astype(CDT)
    k_ref[0] = _dot(hm, wk_ref[...]).astype(k_ref.dtype)
    vt_ref[0] = _dot_nt(wvt_ref[...], hm).astype(vt_ref.dtype)


def _mem_kv(mem, g_row, wk, wvT):
    B, M, D = mem.shape
    n = wk.shape[1]
    return pl.pallas_call(
        _mem_kv_kernel,
        grid=(B,),
        in_specs=[pl.BlockSpec((1, M, D), lambda b: (b, 0, 0)), _const_spec((1, D)),
                  _const_spec(wk.shape), _const_spec(wvT.shape)],
        out_specs=[pl.BlockSpec((1, M, n), lambda b: (b, 0, 0)),
                   pl.BlockSpec((1, n, M), lambda b: (b, 0, 0))],
        out_shape=[jax.ShapeDtypeStruct((B, M, n), CDT), jax.ShapeDtypeStruct((B, n, M), CDT)],
        **_opts("mem_kv", "parallel"),
    )(mem, g_row, wk, wvT)


def _xattn_kernel(x_ref, g_ref, wq_ref, k_ref, vt_ref, wo_ref, o_ref):
    x = x_ref[0]
    h = _rms_fm(x, g_ref[...]).astype(CDT)
    q = (_dot(wq_ref[...], h) * (XATTN_HEAD_DIM ** -0.5 * LOG2E)).astype(CDT)
    outs = []
    for hd in range(XATTN_HEADS):
        rows = slice(hd * XATTN_HEAD_DIM, (hd + 1) * XATTN_HEAD_DIM)
        s = _dot(k_ref[0, :, rows], q[rows])
        e = jnp.exp2(s - jnp.max(s, axis=0, keepdims=True)).astype(CDT)
        pv = _dot(_with_ones(vt_ref[0, rows, :]), e)
        outs.append((pv[0:XATTN_HEAD_DIM] * (1.0 / pv[XATTN_HEAD_DIM:XATTN_HEAD_DIM + 1])).astype(CDT))
    o_ref[0] = x + _dot(wo_ref[...], jnp.concatenate(outs, axis=0))


def _xattn(xT, g_col, wqT, kmem, vmemT, woT, tm):
    B, D, S = xT.shape
    M, n = kmem.shape[1], kmem.shape[2]
    tok = lambda rows: pl.BlockSpec((1, rows, tm), lambda b, i: (b, 0, i))
    return pl.pallas_call(
        _xattn_kernel,
        grid=(B, S // tm),
        in_specs=[tok(D), _const_spec((D, 1)), _const_spec(wqT.shape),
                  pl.BlockSpec((1, M, n), lambda b, i: (b, 0, 0)),
                  pl.BlockSpec((1, n, M), lambda b, i: (b, 0, 0)),
                  _const_spec(woT.shape)],
        out_specs=tok(D),
        out_shape=jax.ShapeDtypeStruct((B, D, S), xT.dtype),
        **_opts("xattn", "parallel", "parallel"),
    )(xT, g_col, wqT, kmem, vmemT, woT)


def _ffn_kernel(x_ref, g_ref, wgu_ref, wd_ref, o_ref, *, d_ff):
    x = x_ref[0]
    h = _rms_fm(x, g_ref[...]).astype(CDT)
    acc = x
    for c in range(d_ff // FF_CHUNK):
        r = c * FF_CHUNK
        gate = _dot(wgu_ref[r:r + FF_CHUNK, :], h)
        up = _dot(wgu_ref[d_ff + r:d_ff + r + FF_CHUNK, :], h)
        act = (gate * jax.nn.sigmoid(gate) * up).astype(CDT)
        acc = acc + _dot(wd_ref[:, r:r + FF_CHUNK], act)
    o_ref[0] = acc


def _ffn(xT, g_col, wguT, wdT, tm):
    B, D, S = xT.shape
    d_ff = wdT.shape[1]
    tok = lambda rows: pl.BlockSpec((1, rows, tm), lambda b, i: (b, 0, i))
    return pl.pallas_call(
        functools.partial(_ffn_kernel, d_ff=d_ff),
        grid=(B, S // tm),
        in_specs=[tok(D), _const_spec((D, 1)), _const_spec(wguT.shape), _const_spec(wdT.shape)],
        out_specs=tok(D),
        out_shape=jax.ShapeDtypeStruct((B, D, S), xT.dtype),
        **_opts("ffn", "parallel", "parallel"),
    )(xT, g_col, wguT, wdT)


def _rope_tables(positions, dim):
    half = dim // 2
    inv_freq = ROPE_THETA ** (-jnp.arange(half, dtype=jnp.float32) / half)
    ang = positions.astype(jnp.float32)[:, None, :] * inv_freq[None, :, None]
    return jnp.cos(ang), jnp.sin(ang)


def _pack_mixer_weight(w_in):
    off = np.cumsum((0, 512, 128, 128, 512, 128, 128, 128, 128, 128, 128, 24, 384, 256, 32))
    a_q, a_k, a_v, b_q, b_kc, b_vc, b_ks, b_vs, b_kw, b_vw, b_g, c_qa, c_kv, c_kr = [
        w_in[:, off[i]:off[i + 1]] for i in range(14)]
    packed = jnp.concatenate([a_q, b_q, a_k, b_ks, b_kw, a_v, b_vs, b_vw, b_kc, b_vc,
                              c_qa, c_kv, c_kr, b_g], axis=1)
    packed = jnp.pad(packed, ((0, 0), (0, N_MIX - packed.shape[1])))
    return packed.T.astype(CDT), w_in[:, off[14]:].T.astype(CDT)


def kernel(x, mem, positions, norm_mix, w_in, swa_sinks, nsa_pe_k, nsa_pe_v, nsa_wk1, nsa_wk2,
           nsa_wv1, nsa_wv2, mla_q_norm, mla_w_q_b, mla_kv_norm, mla_w_kv_b, w_br_a, w_br_b,
           w_br_c, w_out, norm_xattn, norm_mem, w_xq, w_xkv, w_xo, norm_ffn, w_gate_up, w_down,
           norm_final):
    B, S, D = x.shape
    depth = w_in.shape[0]
    nj = S // NSA_SEL_BLOCK
    nc = 4 * nj
    assert S % TK_SEL == 0 and S % TM_PROJ == 0 and S >= NSA_WINDOW + TQ_NSA
    col = lambda v: v.reshape(-1, 1)
    wt = lambda w: w.T.astype(CDT)

    cos, sin = _rope_tables(positions, HEAD_DIM)
    cosm, sinm = _rope_tables(positions, MLA_ROPE_DIM)
    rr, jj = np.divmod(np.arange(nc), nj)
    cend = jnp.asarray(((4 * jj + rr) * NSA_CMP_STRIDE + NSA_CMP_BLOCK - 1).reshape(nc, 1), jnp.int32)

    xT = _to_feature_major(x, TM_PROJ)
    for l in range(depth):
        wmT, wgT = _pack_mixer_weight(w_in[l])
        o1, o2, kk, ks_aug, kcat = _proj(
            xT, col(norm_mix[l]), wmT, cos, sin, cosm, sinm,
            col(mla_q_norm[l]), wt(mla_w_q_b[l]), col(mla_kv_norm[l]), wt(mla_w_kv_b[l]), TM_PROJ)
        kr = o2[:, P_C:P_C + 256].reshape(B, 4, HEAD_DIM, S).transpose(0, 1, 3, 2)
        kr = kr.reshape(B, 4, nj, 4, NSA_CMP_STRIDE * HEAD_DIM).transpose(0, 1, 3, 2, 4)
        kr = kr.reshape(B, 4, nc, NSA_CMP_STRIDE * HEAD_DIM)
        gates = o2[:, P_GATE:P_GATE + 24].reshape(B, NSA_KV_GROUPS, 3 * GROUP_HEADS, S)

        pe = jnp.stack([nsa_pe_k[l], nsa_pe_v[l]]).reshape(2, 2, NSA_CMP_STRIDE * HEAD_DIM)
        w1 = jnp.stack([nsa_wk1[l], nsa_wv1[l]]).astype(CDT)
        w2 = jnp.stack([nsa_wk2[l], nsa_wv2[l]]).astype(CDT)
        w2t = jnp.stack([nsa_wk2[l].T, nsa_wv2[l].T]).astype(CDT)
        cmp_tok, cmp_fm = _compress(kr, pe, w1, w2, w2t)

        o_a = _swa(swa_sinks[l], o1, kk, TQ_SWA)
        o_b = _nsa(o1, cend, cmp_tok, cmp_fm, ks_aug, kk, gates, TQ_NSA)
        o_c = _mla(o1, kcat, TQ_MLA)
        xT = _merge(xT, col(norm_mix[l]), wgT, o_a, o_b, o_c,
                    wt(w_br_a[l]), wt(w_br_b[l]), wt(w_br_c[l]), wt(w_out[l]), TM_PROJ)

        n_kv = XATTN_HEADS * XATTN_HEAD_DIM
        kmem, vmemT = _mem_kv(mem, norm_mem[l].reshape(1, D), w_xkv[l][:, :n_kv].astype(CDT),
                              wt(w_xkv[l][:, n_kv:]))
        xT = _xattn(xT, col(norm_xattn[l]), wt(w_xq[l]), kmem, vmemT, wt(w_xo[l]), TM_PROJ)
        xT = _ffn(xT, col(norm_ffn[l]), wt(w_gate_up[l]), wt(w_down[l]), TM_PROJ)
    return _final_norm(xT, col(norm_final), TM_PROJ)
```

```python
import functools

import numpy as np
import jax
import jax.numpy as jnp
from jax import lax
from jax.experimental import pallas as pl
from jax.experimental.pallas import tpu as pltpu

HEAD_DIM = 64
ROPE_THETA = 10000.0
NORM_EPS = 1e-6
SWA_HEADS = 8
SWA_KV_HEADS = 2
SWA_WINDOW = 128
NSA_HEADS = 8
NSA_KV_GROUPS = 2
NSA_CMP_BLOCK = 32
NSA_CMP_STRIDE = 16
NSA_SEL_BLOCK = 64
NSA_N_SEL = 16
NSA_WINDOW = 512
NSA_FORCE_BONUS = 1e4
MLA_HEADS = 8
MLA_Q_RANK = 384
MLA_KV_RANK = 256
MLA_NOPE_DIM = 64
MLA_ROPE_DIM = 32
MLA_V_DIM = 64
MLA_QK_DIM = MLA_NOPE_DIM + MLA_ROPE_DIM
XATTN_HEADS = 4
XATTN_HEAD_DIM = 128
N_BRANCH = 3
GROUP_HEADS = 4
GROUP_ROWS = GROUP_HEADS * HEAD_DIM
LOG2E = 1.4426950408889634

V7X_VMEM_LIMIT_BYTES = 56 * 1024 * 1024
LANES = 128
ONES_ROWS = 16

CDT = jnp.bfloat16

MASK_VALUE = -1e30

TM_PROJ = 512
TQ_SWA = 256
TQ_NSA = 128
TK_SEL = 512
SEL_BLOCKS_PER_TILE = TK_SEL // NSA_SEL_BLOCK
KSEL_DIM = HEAD_DIM + 16
TQ_MLA = 256
TK_MLA = 512
MLA_HEADS_PER_STEP = 4
QK_LEAD = 2
FF_CHUNK = 704

R_AQ, R_BQ, R_K, R_V, R_C, R_CQA, R_CKV, R_CKR = 0, 512, 1024, 1408, 1792, 2048, 2432, 2688
N_MIX = 2752
O_AQ, O_BQ, O_BQR, O_VV, O_QM, O_KV = 0, 512, 1024, 1536, 1920, 2688
N_OUT1 = 3712
P_C, P_GATE = 0, 256
N_OUT2 = 280


def _dot(a, b):
    return jnp.dot(a, b, preferred_element_type=jnp.float32)


def _dot_nt(a, b):
    return lax.dot_general(a, b, (((1,), (1,)), ((), ())), preferred_element_type=jnp.float32)


def _opts(name, *sem):
    return dict(name=name, compiler_params=pltpu.CompilerParams(
        dimension_semantics=sem, vmem_limit_bytes=V7X_VMEM_LIMIT_BYTES))


def _rms_fm(x, g_col):
    ms = jnp.mean(x * x, axis=0, keepdims=True)
    return x * lax.rsqrt(ms + NORM_EPS) * g_col


def _const_spec(shape):
    nd = len(shape)
    return pl.BlockSpec(shape, lambda *_: (0,) * nd)


def _to_feature_major_kernel(x_ref, o_ref):
    o_ref[0] = x_ref[0].T


def _to_feature_major(x, tm):
    B, S, D = x.shape
    return pl.pallas_call(
        _to_feature_major_kernel,
        grid=(B, S // tm),
        in_specs=[pl.BlockSpec((1, tm, D), lambda b, i: (b, i, 0))],
        out_specs=pl.BlockSpec((1, D, tm), lambda b, i: (b, 0, i)),
        out_shape=jax.ShapeDtypeStruct((B, D, S), x.dtype),
        **_opts("to_feature_major", "parallel", "parallel"),
    )(x)


def _final_norm_kernel(x_ref, g_ref, o_ref):
    o_ref[0] = _rms_fm(x_ref[0], g_ref[...]).T


def _final_norm(xT, g_col, tm):
    B, D, S = xT.shape
    return pl.pallas_call(
        _final_norm_kernel,
        grid=(B, S // tm),
        in_specs=[pl.BlockSpec((1, D, tm), lambda b, i: (b, 0, i)), _const_spec((D, 1))],
        out_specs=pl.BlockSpec((1, tm, D), lambda b, i: (b, i, 0)),
        out_shape=jax.ShapeDtypeStruct((B, S, D), xT.dtype),
        **_opts("final_norm", "parallel", "parallel"),
    )(xT, g_col)


def _rope_store(o_ref, row0, y, cos, sin, n_heads, head_dim, scale):
    half = head_dim // 2
    for h in range(n_heads):
        x1 = y[h * head_dim:h * head_dim + half]
        x2 = y[h * head_dim + half:(h + 1) * head_dim]
        r = row0 + h * head_dim
        o_ref[0, r:r + half, :] = ((x1 * cos - x2 * sin) * scale).astype(o_ref.dtype)
        o_ref[0, r + half:r + head_dim, :] = ((x2 * cos + x1 * sin) * scale).astype(o_ref.dtype)


def _proj_kernel(x_ref, g_ref, wm_ref, cos_ref, sin_ref, cosm_ref, sinm_ref,
                 qn_ref, wqb_ref, kvn_ref, wkvb_ref, o1_ref, o2_ref, kk_ref, ks_ref, kc_ref):
    h = _rms_fm(x_ref[0], g_ref[...]).astype(CDT)
    cos, sin = cos_ref[0], sin_ref[0]
    cosm, sinm = cosm_ref[0], sinm_ref[0]
    qk_scale = HEAD_DIM ** -0.5 * LOG2E

    y = _dot(wm_ref[R_AQ:R_AQ + 512, :], h)
    _rope_store(o1_ref, O_AQ, y, cos, sin, SWA_HEADS, HEAD_DIM, qk_scale)
    y = _dot(wm_ref[R_BQ:R_BQ + 512, :], h)
    o1_ref[0, O_BQ:O_BQ + 512, :] = (y * qk_scale).astype(o1_ref.dtype)
    _rope_store(o1_ref, O_BQR, y, cos, sin, NSA_HEADS, HEAD_DIM, qk_scale)
    y = _dot(wm_ref[R_K:R_K + 384, :], h)
    tm = y.shape[1]
    tok = pl.program_id(1) * tm + lax.broadcasted_iota(jnp.int32, (KSEL_DIM - HEAD_DIM, tm), 1)
    blk = lax.shift_right_arithmetic(tok, int(np.log2(NSA_SEL_BLOCK))) & (SEL_BLOCKS_PER_TILE - 1)
    row = lax.broadcasted_iota(jnp.int32, (KSEL_DIM - HEAD_DIM, tm), 0)
    onehot = jnp.where(blk == row, 1.0, 0.0)
    half = HEAD_DIM // 2
    for kh in range(6):
        x1 = y[kh * HEAD_DIM:kh * HEAD_DIM + half]
        x2 = y[kh * HEAD_DIM + half:(kh + 1) * HEAD_DIM]
        rows = [x1 * cos - x2 * sin, x2 * cos + x1 * sin]
        if kh in (2, 3):
            ks_ref[0, kh - 2] = jnp.concatenate(rows + [onehot], axis=0).T.astype(ks_ref.dtype)
        else:
            kk_ref[0, kh if kh < 2 else kh - 2] = jnp.concatenate(rows, axis=0).T.astype(kk_ref.dtype)
    y = _dot(wm_ref[R_V:R_V + 384, :], h)
    o1_ref[0, O_VV:O_VV + 384, :] = y.astype(o1_ref.dtype)
    o2_ref[0, P_C:P_C + 256, :] = _dot(wm_ref[R_C:R_C + 256, :], h)

    lat = _dot(wm_ref[R_CQA:R_CQA + MLA_Q_RANK, :], h)
    qm = _dot(wqb_ref[...], _rms_fm(lat, qn_ref[...]).astype(CDT))
    m_scale = MLA_QK_DIM ** -0.5 * LOG2E
    hr = MLA_ROPE_DIM // 2
    for hd in range(MLA_HEADS):
        r = hd * MLA_QK_DIM
        o1_ref[0, O_QM + r:O_QM + r + MLA_NOPE_DIM, :] = (
            qm[r:r + MLA_NOPE_DIM] * m_scale).astype(o1_ref.dtype)
        x1 = qm[r + MLA_NOPE_DIM:r + MLA_NOPE_DIM + hr]
        x2 = qm[r + MLA_NOPE_DIM + hr:r + MLA_QK_DIM]
        o1_ref[0, O_QM + r + MLA_NOPE_DIM:O_QM + r + MLA_NOPE_DIM + hr, :] = (
            (x1 * cosm - x2 * sinm) * m_scale).astype(o1_ref.dtype)
        o1_ref[0, O_QM + r + MLA_NOPE_DIM + hr:O_QM + r + MLA_QK_DIM, :] = (
            (x2 * cosm + x1 * sinm) * m_scale).astype(o1_ref.dtype)
    lat = _dot(wm_ref[R_CKV:R_CKV + MLA_KV_RANK, :], h)
    kv = _dot(wkvb_ref[...], _rms_fm(lat, kvn_ref[...]).astype(CDT))
    o1_ref[0, O_KV:O_KV + 1024, :] = kv.astype(o1_ref.dtype)
    y = _dot(wm_ref[R_CKR:R_CKR + 64, :], h)
    x1, x2 = y[0:hr], y[hr:2 * hr]
    k_pe = [x1 * cosm - x2 * sinm, x2 * cosm + x1 * sinm]
    o2_ref[0, P_GATE:P_GATE + 24, :] = jax.nn.sigmoid(y[MLA_ROPE_DIM:MLA_ROPE_DIM + 24])
    for hd in range(MLA_HEADS):
        k_nope = kv[hd * 2 * MLA_NOPE_DIM:hd * 2 * MLA_NOPE_DIM + MLA_NOPE_DIM]
        kc_ref[0, hd] = jnp.concatenate([k_nope] + k_pe, axis=0).T.astype(kc_ref.dtype)


def _proj(xT, g_col, wmT, cos, sin, cosm, sinm, qn, wqbT, kvn, wkvbT, tm):
    B, D, S = xT.shape
    tok = lambda rows: pl.BlockSpec((1, rows, tm), lambda b, i: (b, 0, i))
    return pl.pallas_call(
        _proj_kernel,
        grid=(B, S // tm),
        in_specs=[tok(D), _const_spec((D, 1)), _const_spec(wmT.shape),
                  tok(32), tok(32), tok(16), tok(16),
                  _const_spec(qn.shape), _const_spec(wqbT.shape),
                  _const_spec(kvn.shape), _const_spec(wkvbT.shape)],
        out_specs=[tok(N_OUT1), tok(N_OUT2),
                   pl.BlockSpec((1, 4, tm, HEAD_DIM), lambda b, i: (b, 0, i, 0)),
                   pl.BlockSpec((1, NSA_KV_GROUPS, tm, KSEL_DIM), lambda b, i: (b, 0, i, 0)),
                   pl.BlockSpec((1, MLA_HEADS, tm, MLA_QK_DIM), lambda b, i: (b, 0, i, 0))],
        out_shape=[jax.ShapeDtypeStruct((B, N_OUT1, S), CDT),
                   jax.ShapeDtypeStruct((B, N_OUT2, S), jnp.float32),
                   jax.ShapeDtypeStruct((B, 4, S, HEAD_DIM), CDT),
                   jax.ShapeDtypeStruct((B, NSA_KV_GROUPS, S, KSEL_DIM), CDT),
                   jax.ShapeDtypeStruct((B, MLA_HEADS, S, MLA_QK_DIM), CDT)],
        **_opts("mixer_proj", "parallel", "parallel"),
    )(xT, g_col, wmT, cos, sin, cosm, sinm, qn, wqbT, kvn, wkvbT)


def _compress_kernel(kr_ref, pe_ref, w1_ref, w2_ref, w2t_ref, tok_ref, fm_ref):
    kr = kr_ref[0, 0]
    nc = kr.shape[0]
    nj = nc // 4
    half = kr.shape[1]
    a = _dot((kr + pe_ref[0, 0:1, :]).astype(CDT), w1_ref[0, 0:half, :])
    bm = _dot((kr + pe_ref[0, 1:2, :]).astype(CDT), w1_ref[0, half:2 * half, :])
    wrap = pltpu.roll(bm[0:nj], nj - 1, 0)
    row = lax.broadcasted_iota(jnp.int32, (nj, 1), 0)
    wrap = jnp.where(row == nj - 1, 0.0, wrap)
    pre = a + jnp.concatenate([bm[nj:], wrap], axis=0)
    hid = (pre * jax.nn.sigmoid(pre)).astype(CDT)
    tok_ref[0, 0] = _dot(hid, w2_ref[0]).astype(tok_ref.dtype)
    fm_ref[0, 0] = _dot_nt(w2t_ref[0], hid).astype(fm_ref.dtype)


def _compress(kr, pe, w1, w2, w2t):
    B, _, NC, W = kr.shape
    return pl.pallas_call(
        _compress_kernel,
        grid=(B, 4),
        in_specs=[pl.BlockSpec((1, 1, NC, W), lambda b, n: (b, n, 0, 0)),
                  pl.BlockSpec((1, 2, W), lambda b, n: (n // 2, 0, 0)),
                  pl.BlockSpec((1, 2 * W, HEAD_DIM), lambda b, n: (n // 2, 0, 0)),
                  pl.BlockSpec((1, HEAD_DIM, HEAD_DIM), lambda b, n: (n // 2, 0, 0)),
                  pl.BlockSpec((1, HEAD_DIM, HEAD_DIM), lambda b, n: (n // 2, 0, 0))],
        out_specs=[pl.BlockSpec((1, 1, NC, HEAD_DIM), lambda b, n: (b, n, 0, 0)),
                   pl.BlockSpec((1, 1, HEAD_DIM, NC), lambda b, n: (b, n, 0, 0))],
        out_shape=[jax.ShapeDtypeStruct((B, 4, NC, HEAD_DIM), CDT),
                   jax.ShapeDtypeStruct((B, 4, HEAD_DIM, NC), CDT)],
        **_opts("nsa_compress", "parallel", "parallel"),
    )(kr, pe, w1, w2, w2t)


def _group_queries(q_ref, g):
    return jnp.concatenate([q_ref[0, (g * GROUP_HEADS + r) * HEAD_DIM:(g * GROUP_HEADS + r + 1) * HEAD_DIM, :]
                            for r in range(GROUP_HEADS)], axis=1)


def _with_ones(vt):
    return jnp.concatenate([vt, jnp.ones((ONES_ROWS, vt.shape[1]), vt.dtype)], axis=0)


def _softmax_probs(sc, m_run):
    m_new = jnp.maximum(m_run, jnp.max(sc, axis=0, keepdims=True))
    return m_new, jnp.exp2(sc - m_new).astype(CDT), jnp.exp2(m_run - m_new)


def _softmax_step(sc, m_run, acc, vt_aug):
    m_new, p, alpha = _softmax_probs(sc, m_run)
    return m_new, alpha * acc + _dot(vt_aug, p)


def _window_attention(q4, t4, k, vt, k0, window, sink4=None):
    span = k.shape[0]
    s = _dot(k, q4)
    kpos = k0 + lax.broadcasted_iota(jnp.int32, (span, 1), 0)
    s = jnp.where(kpos <= t4, jnp.where(kpos > t4 - window, s, -jnp.inf), -jnp.inf)
    m = jnp.max(s, axis=0, keepdims=True)
    if sink4 is not None:
        m = jnp.maximum(m, sink4)
    pv = _dot(_with_ones(vt), jnp.exp2(s - m).astype(CDT))
    den = pv[HEAD_DIM:HEAD_DIM + 1]
    if sink4 is not None:
        den = den + jnp.exp2(sink4 - m)
    return pv[0:HEAD_DIM] * (1.0 / den)


def _swa_kernel(sink_ref, q_ref, k_ref, vt_ref, o_ref, *, tq):
    g = pl.program_id(1)
    qs = pl.program_id(2) * tq
    q4 = _group_queries(q_ref, 0)
    t = qs + lax.broadcasted_iota(jnp.int32, (1, tq), 1)
    t4 = jnp.concatenate([t] * GROUP_HEADS, axis=1)
    sink4 = jnp.concatenate([jnp.full((1, tq), sink_ref[g * GROUP_HEADS + r] * LOG2E, jnp.float32)
                             for r in range(GROUP_HEADS)], axis=1)
    span = SWA_WINDOW + tq
    k0 = pl.multiple_of(jnp.maximum(qs - SWA_WINDOW, 0), LANES)
    o4 = _window_attention(q4, t4, k_ref[0, 0, pl.ds(k0, span), :], vt_ref[0, :, pl.ds(k0, span)],
                           k0, SWA_WINDOW, sink4)
    for r in range(GROUP_HEADS):
        o_ref[0, r * HEAD_DIM:(r + 1) * HEAD_DIM, :] = o4[:, r * tq:(r + 1) * tq].astype(o_ref.dtype)


def _swa(sinks, o1, kk, tq):
    B, _, S = o1.shape
    return pl.pallas_call(
        functools.partial(_swa_kernel, tq=tq),
        grid=(B, SWA_KV_HEADS, S // tq),
        in_specs=[pl.BlockSpec(memory_space=pltpu.SMEM),
                  pl.BlockSpec((1, GROUP_ROWS, tq), lambda b, g, i: (b, O_AQ // GROUP_ROWS + g, i)),
                  pl.BlockSpec((1, 1, S, HEAD_DIM), lambda b, g, i: (b, g, 0, 0)),
                  pl.BlockSpec((1, HEAD_DIM, S), lambda b, g, i: (b, O_VV // HEAD_DIM + g, 0))],
        out_specs=pl.BlockSpec((1, GROUP_ROWS, tq), lambda b, g, i: (b, g, i)),
        out_shape=jax.ShapeDtypeStruct((B, SWA_HEADS * HEAD_DIM, S), CDT),
        **_opts("swa_attn", "parallel", "parallel", "parallel"),
    )(sinks, o1, kk, o1)


def _nsa_kernel(q_ref, qr_ref, cend_ref, kc_ref, vct_ref, ks_ref, vst_ref, kw_ref, vwt_ref,
                gate_ref, o_ref, bias_ref, s_a, s_b, *, tq, n_sel):
    qs = pl.program_id(1) * tq
    nc = kc_ref.shape[2]
    nj = nc // 4
    G = NSA_KV_GROUPS
    t = qs + lax.broadcasted_iota(jnp.int32, (1, tq), 1)
    t2 = jnp.concatenate([t] * 2, axis=1)
    t4 = jnp.concatenate([t] * GROUP_HEADS, axis=1)
    jrow = lax.broadcasted_iota(jnp.int32, (nj, tq), 0)
    jf = jrow.astype(jnp.float32)
    cur = lax.shift_right_arithmetic(t, int(np.log2(NSA_SEL_BLOCK)))
    bonus = jnp.where((jrow == 0) | (jrow == cur) | (jrow == cur - 1), NSA_FORCE_BONUS, 0.0)
    span = NSA_WINDOW + tq
    kw0 = pl.multiple_of(jnp.maximum(qs - NSA_WINDOW, 0), LANES)

    qr4, fixed = [], []
    for g in range(G):
        q4 = _group_queries(q_ref, g)
        qr4.append(_group_queries(qr_ref, g))
        s = _dot(kc_ref[0, g], q4)
        s = jnp.where(cend_ref[...] <= t4, s, -jnp.inf)
        m = jnp.max(s, axis=0, keepdims=True)
        m = jnp.where(m == -jnp.inf, 0.0, m)
        e = jnp.exp2(s - m)
        den = jnp.sum(e, axis=0, keepdims=True)
        p = e * (1.0 / jnp.where(den > 0, den, 1.0))
        o_cmp = _dot(vct_ref[0, g], p.astype(CDT))

        psum = p[:, 0:tq]
        for r in range(1, GROUP_HEADS):
            psum = psum + p[:, r * tq:(r + 1) * tq]
        p3 = psum[3 * nj:4 * nj]
        prev = jnp.where(jrow == 0, 0.0, pltpu.roll(p3, 1, 0))
        imp = prev + 2.0 * (psum[0:nj] + psum[nj:2 * nj] + psum[2 * nj:3 * nj]) + p3

        score = jnp.where(jrow <= cur, imp + bonus, -jnp.inf)
        for _ in range(n_sel):
            mx = jnp.max(score, axis=0, keepdims=True)
            idx = jnp.min(jnp.where(score == mx, jf, float(nj)), axis=0, keepdims=True)
            idx = jnp.where(mx > -jnp.inf, idx, float(nj))
            score = jnp.where(jf == idx, -jnp.inf, score)
        bias_ref[g] = jnp.where(jrow <= cur, jnp.where(score == -jnp.inf, 0.0, MASK_VALUE), MASK_VALUE)

        o_win = _window_attention(qr4[g], t4, kw_ref[0, g, pl.ds(kw0, span), :],
                                  vwt_ref[0, g * HEAD_DIM:(g + 1) * HEAD_DIM, pl.ds(kw0, span)],
                                  kw0, NSA_WINDOW)
        parts = []
        for r in range(GROUP_HEADS):
            sl = slice(r * tq, (r + 1) * tq)
            g0 = gate_ref[0, g, 3 * r:3 * r + 1, :]
            g2 = gate_ref[0, g, 3 * r + 2:3 * r + 3, :]
            parts.append((g0 * o_cmp[:, sl], g2 * o_win[:, sl]))
        fixed.append(parts)

    def augmented_queries(kt):
        out = []
        for g in range(G):
            b8 = bias_ref[g, pl.ds(kt * SEL_BLOCKS_PER_TILE, SEL_BLOCKS_PER_TILE), :]
            b16 = jnp.concatenate([b8, jnp.zeros_like(b8)], axis=0)
            out.append(jnp.concatenate(
                [qr4[g], jnp.concatenate([b16] * GROUP_HEADS, axis=1).astype(CDT)], axis=0))
        return out

    def score_chain(kt, s_ref, qa, c):
        g, hp = divmod(c, 2)
        k0 = pl.multiple_of(kt * TK_SEL, TK_SEL)
        s_ref[c] = _dot(ks_ref[0, g, pl.ds(k0, TK_SEL), :],
                        qa[g][:, hp * 2 * tq:(hp + 1) * 2 * tq])

    def consume_chain(kt, s_ref, c, st, diagonal):
        g = c // 2
        k0 = pl.multiple_of(kt * TK_SEL, TK_SEL)
        sc = s_ref[c]
        if diagonal:
            kpos = k0 + lax.broadcasted_iota(jnp.int32, (TK_SEL, 1), 0)
            sc = jnp.where(kpos <= t2, sc, MASK_VALUE)
        vt_aug = _with_ones(vst_ref[0, g * HEAD_DIM:(g + 1) * HEAD_DIM, pl.ds(k0, TK_SEL)])
        return _softmax_step(sc, st[0], st[1], vt_aug)

    def step(kt, s_cur, s_nxt, state, diagonal=False):
        n_chain = 2 * G
        if s_nxt is not None:
            qa = augmented_queries(kt + 1)
            for c in range(QK_LEAD):
                score_chain(kt + 1, s_nxt, qa, c)
        out = []
        for c in range(n_chain):
            out.append(consume_chain(kt, s_cur, c, state[c], diagonal))
            if s_nxt is not None and c + QK_LEAD < n_chain:
                score_chain(kt + 1, s_nxt, qa, c + QK_LEAD)
        return tuple(out)

    def finish(state):
        for g in range(G):
            for r in range(GROUP_HEADS):
                _, acc = state[2 * g + r // 2]
                sl = slice((r % 2) * tq, (r % 2 + 1) * tq)
                o_sel = acc[0:HEAD_DIM, sl] * (1.0 / acc[HEAD_DIM:HEAD_DIM + 1, sl])
                g1 = gate_ref[0, g, 3 * r + 1:3 * r + 2, :]
                c_part, w_part = fixed[g][r]
                row = (g * GROUP_HEADS + r) * HEAD_DIM
                o_ref[0, row:row + HEAD_DIM, :] = (c_part + g1 * o_sel + w_part).astype(o_ref.dtype)

    def pair(j, state):
        return step(2 * j + 1, s_b, s_a, step(2 * j, s_a, s_b, state))

    init = tuple((jnp.full((1, 2 * tq), MASK_VALUE, jnp.float32),
                  jnp.zeros((HEAD_DIM + ONES_ROWS, 2 * tq), jnp.float32)) for _ in range(2 * G))
    n_full = qs // TK_SEL
    qa0 = augmented_queries(0)
    for c in range(2 * G):
        score_chain(0, s_a, qa0, c)
    state = lax.fori_loop(0, n_full // 2, pair, init)
    odd = n_full % 2 == 1

    @pl.when(odd)
    def _():
        finish(step(n_full, s_b, None, step(n_full - 1, s_a, s_b, state), diagonal=True))

    @pl.when(jnp.logical_not(odd))
    def _():
        finish(step(n_full, s_a, None, state, diagonal=True))


def _nsa(o1, cend, cmp_tok, cmp_fm, ks_aug, kk, gates, tq):
    B, _, S = o1.shape
    nc = cmp_tok.shape[2]
    nj = nc // 4
    n_sel = min(NSA_N_SEL, nj)
    G = NSA_KV_GROUPS
    rows = NSA_HEADS * HEAD_DIM
    vrows = G * HEAD_DIM
    return pl.pallas_call(
        functools.partial(_nsa_kernel, tq=tq, n_sel=n_sel),
        grid=(B, S // tq),
        in_specs=[pl.BlockSpec((1, rows, tq), lambda b, i: (b, O_BQ // rows, i)),
                  pl.BlockSpec((1, rows, tq), lambda b, i: (b, O_BQR // rows, i)),
                  _const_spec((nc, 1)),
                  pl.BlockSpec((1, G, nc, HEAD_DIM), lambda b, i: (b, 0, 0, 0)),
                  pl.BlockSpec((1, G, HEAD_DIM, nc), lambda b, i: (b, 1, 0, 0)),
                  pl.BlockSpec((1, G, S, KSEL_DIM), lambda b, i: (b, 0, 0, 0)),
                  pl.BlockSpec((1, vrows, S), lambda b, i: (b, O_VV // vrows + 1, 0)),
                  pl.BlockSpec((1, G, S, HEAD_DIM), lambda b, i: (b, 1, 0, 0)),
                  pl.BlockSpec((1, vrows, S), lambda b, i: (b, O_VV // vrows + 2, 0)),
                  pl.BlockSpec((1, G, 3 * GROUP_HEADS, tq), lambda b, i: (b, 0, 0, i))],
        out_specs=pl.BlockSpec((1, rows, tq), lambda b, i: (b, 0, i)),
        out_shape=jax.ShapeDtypeStruct((B, rows, S), CDT),
        scratch_shapes=[pltpu.VMEM((G, nj, tq), jnp.float32)]
                       + [pltpu.VMEM((2 * G, TK_SEL, 2 * tq), jnp.float32)] * 2,
        **_opts("nsa_attn", "parallel", "arbitrary"),
    )(o1, o1, cend, cmp_tok, cmp_fm, ks_aug, o1, kk, o1, gates)


def _mla_kernel(q_ref, k_ref, *rest, tq, hps):
    vt_refs, o_ref, s_a, s_b = rest[:hps], rest[hps], rest[hps + 1], rest[hps + 2]
    qs = pl.program_id(2) * tq
    q = [q_ref[0, h * MLA_QK_DIM:(h + 1) * MLA_QK_DIM, :] for h in range(hps)]
    t = qs + lax.broadcasted_iota(jnp.int32, (1, tq), 1)

    def score_chain(kt, s_ref, h):
        k0 = pl.multiple_of(kt * TK_MLA, TK_MLA)
        s_ref[h] = _dot(k_ref[0, h, pl.ds(k0, TK_MLA), :], q[h])

    def consume_chain(kt, s_ref, h, st, diagonal):
        k0 = pl.multiple_of(kt * TK_MLA, TK_MLA)
        sc = s_ref[h]
        if diagonal:
            kpos = k0 + lax.broadcasted_iota(jnp.int32, (TK_MLA, 1), 0)
            sc = jnp.where(kpos <= t, sc, MASK_VALUE)
        return _softmax_step(sc, st[0], st[1], _with_ones(vt_refs[h][0, :, pl.ds(k0, TK_MLA)]))

    def step(kt, s_cur, s_nxt, state, diagonal=False):
        lead = QK_LEAD
        if s_nxt is not None:
            for h in range(lead):
                score_chain(kt + 1, s_nxt, h)
        out = []
        for h in range(hps):
            out.append(consume_chain(kt, s_cur, h, state[h], diagonal))
            if s_nxt is not None and h + lead < hps:
                score_chain(kt + 1, s_nxt, h + lead)
        return tuple(out)

    def finish(state):
        for h in range(hps):
            _, acc = state[h]
            o_ref[0, h * MLA_V_DIM:(h + 1) * MLA_V_DIM, :] = (
                acc[0:MLA_V_DIM] * (1.0 / acc[MLA_V_DIM:MLA_V_DIM + 1])).astype(o_ref.dtype)

    def pair(j, state):
        return step(2 * j + 1, s_b, s_a, step(2 * j, s_a, s_b, state))

    init = tuple((jnp.full((1, tq), MASK_VALUE, jnp.float32),
                  jnp.zeros((MLA_V_DIM + ONES_ROWS, tq), jnp.float32)) for _ in range(hps))
    n_full = qs // TK_MLA
    for h in range(hps):
        score_chain(0, s_a, h)
    state = lax.fori_loop(0, n_full // 2, pair, init)
    odd = n_full % 2 == 1

    @pl.when(odd)
    def _():
        finish(step(n_full, s_b, None, step(n_full - 1, s_a, s_b, state), diagonal=True))

    @pl.when(jnp.logical_not(odd))
    def _():
        finish(step(n_full, s_a, None, state, diagonal=True))


def _mla(o1, kcat, tq):
    B, _, S = o1.shape
    hps = MLA_HEADS_PER_STEP
    v_spec = lambda h: pl.BlockSpec(
        (1, MLA_V_DIM, S), lambda b, hg, i: (b, O_KV // MLA_V_DIM + 2 * (hg * hps + h) + 1, 0))
    return pl.pallas_call(
        functools.partial(_mla_kernel, tq=tq, hps=hps),
        grid=(B, MLA_HEADS // hps, S // tq),
        in_specs=[pl.BlockSpec((1, hps * MLA_QK_DIM, tq),
                               lambda b, hg, i: (b, O_QM // (hps * MLA_QK_DIM) + hg, i)),
                  pl.BlockSpec((1, hps, S, MLA_QK_DIM), lambda b, hg, i: (b, hg, 0, 0))]
                 + [v_spec(h) for h in range(hps)],
        out_specs=pl.BlockSpec((1, hps * MLA_V_DIM, tq), lambda b, hg, i: (b, hg, i)),
        out_shape=jax.ShapeDtypeStruct((B, MLA_HEADS * MLA_V_DIM, S), CDT),
        scratch_shapes=[pltpu.VMEM((hps, TK_MLA, tq), jnp.float32)] * 2,
        **_opts("mla_attn", "parallel", "parallel", "parallel"),
    )(o1, kcat, *([o1] * hps))


def _merge_kernel(x_ref, g_ref, wg_ref, oa_ref, ob_ref, oc_ref, wa_ref, wb_ref, wc_ref, wo_ref, o_ref):
    x = x_ref[0]
    d = x.shape[0]
    h = _rms_fm(x, g_ref[...]).astype(CDT)
    merged = None
    for i, (br_ref, w_ref) in enumerate(((oa_ref, wa_ref), (ob_ref, wb_ref), (oc_ref, wc_ref))):
        gate = jax.nn.sigmoid(_dot(wg_ref[i * d:(i + 1) * d, :], h))
        term = gate * _dot(w_ref[...], br_ref[0])
        merged = term if merged is None else merged + term
    o_ref[0] = x + _dot(wo_ref[...], merged.astype(CDT))


def _merge(xT, g_col, wgT, oa, ob, oc, waT, wbT, wcT, woT, tm):
    B, D, S = xT.shape
    tok = lambda rows: pl.BlockSpec((1, rows, tm), lambda b, i: (b, 0, i))
    return pl.pallas_call(
        _merge_kernel,
        grid=(B, S // tm),
        in_specs=[tok(D), _const_spec((D, 1)), _const_spec(wgT.shape),
                  tok(oa.shape[1]), tok(ob.shape[1]), tok(oc.shape[1]),
                  _const_spec(waT.shape), _const_spec(wbT.shape), _const_spec(wcT.shape),
                  _const_spec(woT.shape)],
        out_specs=tok(D),
        out_shape=jax.ShapeDtypeStruct((B, D, S), xT.dtype),
        **_opts("merge_out", "parallel", "parallel"),
    )(xT, g_col, wgT, oa, ob, oc, waT, wbT, wcT, woT)


def _mem_kv_kernel(mem_ref, g_ref, wk_ref, wvt_ref, k_ref, vt_ref):
    m = mem_ref[0]
    ms = jnp.mean(m * m, axis=-1, keepdims=True)
    hm = (m * lax.rsqrt(ms + NORM_EPS) * g_ref[...]).astype(CDT)
    k_ref[0] = _dot(hm, wk_ref[...]).astype(k_ref.dtype)
    vt_ref[0] = _dot_nt(wvt_ref[...], hm).astype(vt_ref.dtype)


def _mem_kv(mem, g_row, wk, wvT):
    B, M, D = mem.shape
    n = wk.shape[1]
    return pl.pallas_call(
        _mem_kv_kernel,
        grid=(B,),
        in_specs=[pl.BlockSpec((1, M, D), lambda b: (b, 0, 0)), _const_spec((1, D)),
                  _const_spec(wk.shape), _const_spec(wvT.shape)],
        out_specs=[pl.BlockSpec((1, M, n), lambda b: (b, 0, 0)),
                   pl.BlockSpec((1, n, M), lambda b: (b, 0, 0))],
        out_shape=[jax.ShapeDtypeStruct((B, M, n), CDT), jax.ShapeDtypeStruct((B, n, M), CDT)],
        **_opts("mem_kv", "parallel"),
    )(mem, g_row, wk, wvT)


def _xattn_kernel(x_ref, g_ref, wq_ref, k_ref, vt_ref, wo_ref, o_ref):
    x = x_ref[0]
    h = _rms_fm(x, g_ref[...]).astype(CDT)
    q = (_dot(wq_ref[...], h) * (XATTN_HEAD_DIM ** -0.5 * LOG2E)).astype(CDT)
    outs = []
    for hd in range(XATTN_HEADS):
        rows = slice(hd * XATTN_HEAD_DIM, (hd + 1) * XATTN_HEAD_DIM)
        s = _dot(k_ref[0, :, rows], q[rows])
        e = jnp.exp2(s - jnp.max(s, axis=0, keepdims=True)).astype(CDT)
        pv = _dot(_with_ones(vt_ref[0, rows, :]), e)
        outs.append((pv[0:XATTN_HEAD_DIM] * (1.0 / pv[XATTN_HEAD_DIM:XATTN_HEAD_DIM + 1])).astype(CDT))
    o_ref[0] = x + _dot(wo_ref[...], jnp.concatenate(outs, axis=0))


def _xattn(xT, g_col, wqT, kmem, vmemT, woT, tm):
    B, D, S = xT.shape
    M, n = kmem.shape[1], kmem.shape[2]
    tok = lambda rows: pl.BlockSpec((1, rows, tm), lambda b, i: (b, 0, i))
    return pl.pallas_call(
        _xattn_kernel,
        grid=(B, S // tm),
        in_specs=[tok(D), _const_spec((D, 1)), _const_spec(wqT.shape),
                  pl.BlockSpec((1, M, n), lambda b, i: (b, 0, 0)),
                  pl.BlockSpec((1, n, M), lambda b, i: (b, 0, 0)),
                  _const_spec(woT.shape)],
        out_specs=tok(D),
        out_shape=jax.ShapeDtypeStruct((B, D, S), xT.dtype),
        **_opts("xattn", "parallel", "parallel"),
    )(xT, g_col, wqT, kmem, vmemT, woT)


def _ffn_kernel(x_ref, g_ref, wgu_ref, wd_ref, o_ref, *, d_ff):
    x = x_ref[0]
    h = _rms_fm(x, g_ref[...]).astype(CDT)
    acc = x
    for c in range(d_ff // FF_CHUNK):
        r = c * FF_CHUNK
        gate = _dot(wgu_ref[r:r + FF_CHUNK, :], h)
        up = _dot(wgu_ref[d_ff + r:d_ff + r + FF_CHUNK, :], h)
        act = (gate * jax.nn.sigmoid(gate) * up).astype(CDT)
        acc = acc + _dot(wd_ref[:, r:r + FF_CHUNK], act)
    o_ref[0] = acc


def _ffn(xT, g_col, wguT, wdT, tm):
    B, D, S = xT.shape
    d_ff = wdT.shape[1]
    tok = lambda rows: pl.BlockSpec((1, rows, tm), lambda b, i: (b, 0, i))
    return pl.pallas_call(
        functools.partial(_ffn_kernel, d_ff=d_ff),
        grid=(B, S // tm),
        in_specs=[tok(D), _const_spec((D, 1)), _const_spec(wguT.shape), _const_spec(wdT.shape)],
        out_specs=tok(D),
        out_shape=jax.ShapeDtypeStruct((B, D, S), xT.dtype),
        **_opts("ffn", "parallel", "parallel"),
    )(xT, g_col, wguT, wdT)


def _rope_tables(positions, dim):
    half = dim // 2
    inv_freq = ROPE_THETA ** (-jnp.arange(half, dtype=jnp.float32) / half)
    ang = positions.astype(jnp.float32)[:, None, :] * inv_freq[None, :, None]
    return jnp.cos(ang), jnp.sin(ang)


def _pack_mixer_weight(w_in):
    off = np.cumsum((0, 512, 128, 128, 512, 128, 128, 128, 128, 128, 128, 24, 384, 256, 32))
    a_q, a_k, a_v, b_q, b_kc, b_vc, b_ks, b_vs, b_kw, b_vw, b_g, c_qa, c_kv, c_kr = [
        w_in[:, off[i]:off[i + 1]] for i in range(14)]
    packed = jnp.concatenate([a_q, b_q, a_k, b_ks, b_kw, a_v, b_vs, b_vw, b_kc, b_vc,
                              c_qa, c_kv, c_kr, b_g], axis=1)
    packed = jnp.pad(packed, ((0, 0), (0, N_MIX - packed.shape[1])))
    return packed.T.astype(CDT), w_in[:, off[14]:].T.astype(CDT)


def kernel(x, mem, positions, norm_mix, w_in, swa_sinks, nsa_pe_k, nsa_pe_v, nsa_wk1, nsa_wk2,
           nsa_wv1, nsa_wv2, mla_q_norm, mla_w_q_b, mla_kv_norm, mla_w_kv_b, w_br_a, w_br_b,
           w_br_c, w_out, norm_xattn, norm_mem, w_xq, w_xkv, w_xo, norm_ffn, w_gate_up, w_down,
           norm_final):
    B, S, D = x.shape
    depth = w_in.shape[0]
    nj = S // NSA_SEL_BLOCK
    nc = 4 * nj
    assert S % TK_SEL == 0 and S % TM_PROJ == 0 and S >= NSA_WINDOW + TQ_NSA
    col = lambda v: v.reshape(-1, 1)
    wt = lambda w: w.T.astype(CDT)

    cos, sin = _rope_tables(positions, HEAD_DIM)
    cosm, sinm = _rope_tables(positions, MLA_ROPE_DIM)
    rr, jj = np.divmod(np.arange(nc), nj)
    cend = jnp.asarray(((4 * jj + rr) * NSA_CMP_STRIDE + NSA_CMP_BLOCK - 1).reshape(nc, 1), jnp.int32)

    xT = _to_feature_major(x, TM_PROJ)
    for l in range(depth):
        wmT, wgT = _pack_mixer_weight(w_in[l])
        o1, o2, kk, ks_aug, kcat = _proj(
            xT, col(norm_mix[l]), wmT, cos, sin, cosm, sinm,
            col(mla_q_norm[l]), wt(mla_w_q_b[l]), col(mla_kv_norm[l]), wt(mla_w_kv_b[l]), TM_PROJ)
        kr = o2[:, P_C:P_C + 256].reshape(B, 4, HEAD_DIM, S).transpose(0, 1, 3, 2)
        kr = kr.reshape(B, 4, nj, 4, NSA_CMP_STRIDE * HEAD_DIM).transpose(0, 1, 3, 2, 4)
        kr = kr.reshape(B, 4, nc, NSA_CMP_STRIDE * HEAD_DIM)
        gates = o2[:, P_GATE:P_GATE + 24].reshape(B, NSA_KV_GROUPS, 3 * GROUP_HEADS, S)

        pe = jnp.stack([nsa_pe_k[l], nsa_pe_v[l]]).reshape(2, 2, NSA_CMP_STRIDE * HEAD_DIM)
        w1 = jnp.stack([nsa_wk1[l], nsa_wv1[l]]).astype(CDT)
        w2 = jnp.stack([nsa_wk2[l], nsa_wv2[l]]).astype(CDT)
        w2t = jnp.stack([nsa_wk2[l].T, nsa_wv2[l].T]).astype(CDT)
        cmp_tok, cmp_fm = _compress(kr, pe, w1, w2, w2t)

        o_a = _swa(swa_sinks[l], o1, kk, TQ_SWA)
        o_b = _nsa(o1, cend, cmp_tok, cmp_fm, ks_aug, kk, gates, TQ_NSA)
        o_c = _mla(o1, kcat, TQ_MLA)
        xT = _merge(xT, col(norm_mix[l]), wgT, o_a, o_b, o_c,
                    wt(w_br_a[l]), wt(w_br_b[l]), wt(w_br_c[l]), wt(w_out[l]), TM_PROJ)

        n_kv = XATTN_HEADS * XATTN_HEAD_DIM
        kmem, vmemT = _mem_kv(mem, norm_mem[l].reshape(1, D), w_xkv[l][:, :n_kv].astype(CDT),
                              wt(w_xkv[l][:, n_kv:]))
        xT = _xattn(xT, col(norm_xattn[l]), wt(w_xq[l]), kmem, vmemT, wt(w_xo[l]), TM_PROJ)
        xT = _ffn(xT, col(norm_ffn[l]), wt(w_gate_up[l]), wt(w_down[l]), TM_PROJ)
    return _final_norm(xT, col(norm_final), TM_PROJ)
```

```python
import functools

import numpy as np
import jax
import jax.numpy as jnp
from jax import lax
from jax.experimental import pallas as pl
from jax.experimental.pallas import tpu as pltpu

HEAD_DIM = 64
ROPE_THETA = 10000.0
NORM_EPS = 1e-6
SWA_HEADS = 8
SWA_KV_HEADS = 2
SWA_WINDOW = 128
NSA_HEADS = 8
NSA_KV_GROUPS = 2
NSA_CMP_BLOCK = 32
NSA_CMP_STRIDE = 16
NSA_SEL_BLOCK = 64
NSA_N_SEL = 16
NSA_WINDOW = 512
NSA_FORCE_BONUS = 1e4
MLA_HEADS = 8
MLA_Q_RANK = 384
MLA_KV_RANK = 256
MLA_NOPE_DIM = 64
MLA_ROPE_DIM = 32
MLA_V_DIM = 64
MLA_QK_DIM = MLA_NOPE_DIM + MLA_ROPE_DIM
XATTN_HEADS = 4
XATTN_HEAD_DIM = 128
N_BRANCH = 3
GROUP_HEADS = 4
GROUP_ROWS = GROUP_HEADS * HEAD_DIM
LOG2E = 1.4426950408889634

V7X_VMEM_LIMIT_BYTES = 56 * 1024 * 1024
LANES = 128
ONES_ROWS = 16

CDT = jnp.bfloat16

MASK_VALUE = -1e30

TM_PROJ = 512
TQ_SWA = 256
TQ_NSA = 128
TK_SEL = 512
SEL_BLOCKS_PER_TILE = TK_SEL // NSA_SEL_BLOCK
KSEL_DIM = HEAD_DIM + 16
TQ_MLA = 256
TK_MLA = 512
MLA_HEADS_PER_STEP = 4
QK_LEAD = 2
FF_CHUNK = 704

R_AQ, R_BQ, R_K, R_V, R_C, R_CQA, R_CKV, R_CKR = 0, 512, 1024, 1408, 1792, 2048, 2432, 2688
N_MIX = 2752
O_AQ, O_BQ, O_BQR, O_VV, O_QM, O_KV = 0, 512, 1024, 1536, 1920, 2688
N_OUT1 = 3712
P_C, P_GATE = 0, 256
N_OUT2 = 280


def _dot(a, b):
    return jnp.dot(a, b, preferred_element_type=jnp.float32)


def _dot_nt(a, b):
    return lax.dot_general(a, b, (((1,), (1,)), ((), ())), preferred_element_type=jnp.float32)


def _opts(name, *sem):
    return dict(name=name, compiler_params=pltpu.CompilerParams(
        dimension_semantics=sem, vmem_limit_bytes=V7X_VMEM_LIMIT_BYTES))


def _rms_fm(x, g_col):
    ms = jnp.mean(x * x, axis=0, keepdims=True)
    return x * lax.rsqrt(ms + NORM_EPS) * g_col


def _const_spec(shape):
    nd = len(shape)
    return pl.BlockSpec(shape, lambda *_: (0,) * nd)


def _residual_spec(d, tm, token_major):
    if token_major:
        return pl.BlockSpec((1, tm, d), lambda b, i: (b, i, 0))
    return pl.BlockSpec((1, d, tm), lambda b, i: (b, 0, i))


def _load_residual(x_ref, token_major):
    return x_ref[0].T if token_major else x_ref[0]


def _rope_store(o_ref, row0, y, cos, sin, n_heads, head_dim, scale):
    half = head_dim // 2
    for h in range(n_heads):
        x1 = y[h * head_dim:h * head_dim + half]
        x2 = y[h * head_dim + half:(h + 1) * head_dim]
        r = row0 + h * head_dim
        o_ref[0, r:r + half, :] = ((x1 * cos - x2 * sin) * scale).astype(o_ref.dtype)
        o_ref[0, r + half:r + head_dim, :] = ((x2 * cos + x1 * sin) * scale).astype(o_ref.dtype)


def _proj_kernel(x_ref, g_ref, wm_ref, cos_ref, sin_ref, cosm_ref, sinm_ref,
                 qn_ref, wqb_ref, kvn_ref, wkvb_ref, o1_ref, o2_ref, kk_ref, ks_ref, kc_ref,
                 *, token_major):
    h = _rms_fm(_load_residual(x_ref, token_major), g_ref[...]).astype(CDT)
    cos, sin = cos_ref[0], sin_ref[0]
    cosm, sinm = cosm_ref[0], sinm_ref[0]
    qk_scale = HEAD_DIM ** -0.5 * LOG2E

    y = _dot(wm_ref[R_AQ:R_AQ + 512, :], h)
    _rope_store(o1_ref, O_AQ, y, cos, sin, SWA_HEADS, HEAD_DIM, qk_scale)
    y = _dot(wm_ref[R_BQ:R_BQ + 512, :], h)
    o1_ref[0, O_BQ:O_BQ + 512, :] = (y * qk_scale).astype(o1_ref.dtype)
    _rope_store(o1_ref, O_BQR, y, cos, sin, NSA_HEADS, HEAD_DIM, qk_scale)
    y = _dot(wm_ref[R_K:R_K + 384, :], h)
    tm = y.shape[1]
    tok = pl.program_id(1) * tm + lax.broadcasted_iota(jnp.int32, (KSEL_DIM - HEAD_DIM, tm), 1)
    blk = lax.shift_right_arithmetic(tok, int(np.log2(NSA_SEL_BLOCK))) & (SEL_BLOCKS_PER_TILE - 1)
    row = lax.broadcasted_iota(jnp.int32, (KSEL_DIM - HEAD_DIM, tm), 0)
    onehot = jnp.where(blk == row, 1.0, 0.0)
    half = HEAD_DIM // 2
    for kh in range(6):
        x1 = y[kh * HEAD_DIM:kh * HEAD_DIM + half]
        x2 = y[kh * HEAD_DIM + half:(kh + 1) * HEAD_DIM]
        rows = [x1 * cos - x2 * sin, x2 * cos + x1 * sin]
        if kh in (2, 3):
            ks_ref[0, kh - 2] = jnp.concatenate(rows + [onehot], axis=0).T.astype(ks_ref.dtype)
        else:
            kk_ref[0, kh if kh < 2 else kh - 2] = jnp.concatenate(rows, axis=0).T.astype(kk_ref.dtype)
    y = _dot(wm_ref[R_V:R_V + 384, :], h)
    o1_ref[0, O_VV:O_VV + 384, :] = y.astype(o1_ref.dtype)
    o2_ref[0, P_C:P_C + 256, :] = _dot(wm_ref[R_C:R_C + 256, :], h)

    lat = _dot(wm_ref[R_CQA:R_CQA + MLA_Q_RANK, :], h)
    qm = _dot(wqb_ref[...], _rms_fm(lat, qn_ref[...]).astype(CDT))
    m_scale = MLA_QK_DIM ** -0.5 * LOG2E
    hr = MLA_ROPE_DIM // 2
    for hd in range(MLA_HEADS):
        r = hd * MLA_QK_DIM
        o1_ref[0, O_QM + r:O_QM + r + MLA_NOPE_DIM, :] = (
            qm[r:r + MLA_NOPE_DIM] * m_scale).astype(o1_ref.dtype)
        x1 = qm[r + MLA_NOPE_DIM:r + MLA_NOPE_DIM + hr]
        x2 = qm[r + MLA_NOPE_DIM + hr:r + MLA_QK_DIM]
        o1_ref[0, O_QM + r + MLA_NOPE_DIM:O_QM + r + MLA_NOPE_DIM + hr, :] = (
            (x1 * cosm - x2 * sinm) * m_scale).astype(o1_ref.dtype)
        o1_ref[0, O_QM + r + MLA_NOPE_DIM + hr:O_QM + r + MLA_QK_DIM, :] = (
            (x2 * cosm + x1 * sinm) * m_scale).astype(o1_ref.dtype)
    lat = _dot(wm_ref[R_CKV:R_CKV + MLA_KV_RANK, :], h)
    kv = _dot(wkvb_ref[...], _rms_fm(lat, kvn_ref[...]).astype(CDT))
    o1_ref[0, O_KV:O_KV + 1024, :] = kv.astype(o1_ref.dtype)
    y = _dot(wm_ref[R_CKR:R_CKR + 64, :], h)
    x1, x2 = y[0:hr], y[hr:2 * hr]
    k_pe = [x1 * cosm - x2 * sinm, x2 * cosm + x1 * sinm]
    o2_ref[0, P_GATE:P_GATE + 24, :] = jax.nn.sigmoid(y[MLA_ROPE_DIM:MLA_ROPE_DIM + 24])
    for hd in range(MLA_HEADS):
        k_nope = kv[hd * 2 * MLA_NOPE_DIM:hd * 2 * MLA_NOPE_DIM + MLA_NOPE_DIM]
        kc_ref[0, hd] = jnp.concatenate([k_nope] + k_pe, axis=0).T.astype(kc_ref.dtype)


def _proj(x, g_col, wmT, cos, sin, cosm, sinm, qn, wqbT, kvn, wkvbT, tm, token_major):
    B, D, S = (x.shape[0], x.shape[2], x.shape[1]) if token_major else x.shape
    tok = lambda rows: pl.BlockSpec((1, rows, tm), lambda b, i: (b, 0, i))
    return pl.pallas_call(
        functools.partial(_proj_kernel, token_major=token_major),
        grid=(B, S // tm),
        in_specs=[_residual_spec(D, tm, token_major), _const_spec((D, 1)), _const_spec(wmT.shape),
                  tok(32), tok(32), tok(16), tok(16),
                  _const_spec(qn.shape), _const_spec(wqbT.shape),
                  _const_spec(kvn.shape), _const_spec(wkvbT.shape)],
        out_specs=[tok(N_OUT1), tok(N_OUT2),
                   pl.BlockSpec((1, 4, tm, HEAD_DIM), lambda b, i: (b, 0, i, 0)),
                   pl.BlockSpec((1, NSA_KV_GROUPS, tm, KSEL_DIM), lambda b, i: (b, 0, i, 0)),
                   pl.BlockSpec((1, MLA_HEADS, tm, MLA_QK_DIM), lambda b, i: (b, 0, i, 0))],
        out_shape=[jax.ShapeDtypeStruct((B, N_OUT1, S), CDT),
                   jax.ShapeDtypeStruct((B, N_OUT2, S), jnp.float32),
                   jax.ShapeDtypeStruct((B, 4, S, HEAD_DIM), CDT),
                   jax.ShapeDtypeStruct((B, NSA_KV_GROUPS, S, KSEL_DIM), CDT),
                   jax.ShapeDtypeStruct((B, MLA_HEADS, S, MLA_QK_DIM), CDT)],
        **_opts("mixer_proj", "parallel", "parallel"),
    )(x, g_col, wmT, cos, sin, cosm, sinm, qn, wqbT, kvn, wkvbT)


def _compress_kernel(kr_ref, pe_ref, w1_ref, w2_ref, w2t_ref, tok_ref, fm_ref):
    kr = kr_ref[0, 0]
    nc = kr.shape[0]
    nj = nc // 4
    half = kr.shape[1]
    a = _dot((kr + pe_ref[0, 0:1, :]).astype(CDT), w1_ref[0, 0:half, :])
    bm = _dot((kr + pe_ref[0, 1:2, :]).astype(CDT), w1_ref[0, half:2 * half, :])
    wrap = pltpu.roll(bm[0:nj], nj - 1, 0)
    row = lax.broadcasted_iota(jnp.int32, (nj, 1), 0)
    wrap = jnp.where(row == nj - 1, 0.0, wrap)
    pre = a + jnp.concatenate([bm[nj:], wrap], axis=0)
    hid = (pre * jax.nn.sigmoid(pre)).astype(CDT)
    tok_ref[0, 0] = _dot(hid, w2_ref[0]).astype(tok_ref.dtype)
    fm_ref[0, 0] = _dot_nt(w2t_ref[0], hid).astype(fm_ref.dtype)


def _compress(kr, pe, w1, w2, w2t):
    B, _, NC, W = kr.shape
    return pl.pallas_call(
        _compress_kernel,
        grid=(B, 4),
        in_specs=[pl.BlockSpec((1, 1, NC, W), lambda b, n: (b, n, 0, 0)),
                  pl.BlockSpec((1, 2, W), lambda b, n: (n // 2, 0, 0)),
                  pl.BlockSpec((1, 2 * W, HEAD_DIM), lambda b, n: (n // 2, 0, 0)),
                  pl.BlockSpec((1, HEAD_DIM, HEAD_DIM), lambda b, n: (n // 2, 0, 0)),
                  pl.BlockSpec((1, HEAD_DIM, HEAD_DIM), lambda b, n: (n // 2, 0, 0))],
        out_specs=[pl.BlockSpec((1, 1, NC, HEAD_DIM), lambda b, n: (b, n, 0, 0)),
                   pl.BlockSpec((1, 1, HEAD_DIM, NC), lambda b, n: (b, n, 0, 0))],
        out_shape=[jax.ShapeDtypeStruct((B, 4, NC, HEAD_DIM), CDT),
                   jax.ShapeDtypeStruct((B, 4, HEAD_DIM, NC), CDT)],
        **_opts("nsa_compress", "parallel", "parallel"),
    )(kr, pe, w1, w2, w2t)


def _group_queries(q_ref, g):
    return jnp.concatenate([q_ref[0, (g * GROUP_HEADS + r) * HEAD_DIM:(g * GROUP_HEADS + r + 1) * HEAD_DIM, :]
                            for r in range(GROUP_HEADS)], axis=1)


def _with_ones(vt):
    return jnp.concatenate([vt, jnp.ones((ONES_ROWS, vt.shape[1]), vt.dtype)], axis=0)


def _softmax_probs(sc, m_run):
    m_new = jnp.maximum(m_run, jnp.max(sc, axis=0, keepdims=True))
    return m_new, jnp.exp2(sc - m_new).astype(CDT), jnp.exp2(m_run - m_new)


def _softmax_step(sc, m_run, acc, vt_aug):
    m_new, p, alpha = _softmax_probs(sc, m_run)
    return m_new, alpha * acc + _dot(vt_aug, p)


def _window_scores(q4, t4, k, k0, window):
    span = k.shape[0]
    s = _dot(k, q4)
    kpos = k0 + lax.broadcasted_iota(jnp.int32, (span, 1), 0)
    return jnp.where(kpos <= t4, jnp.where(kpos > t4 - window, s, -jnp.inf), -jnp.inf)


def _window_output(s, vt, sink4=None):
    m = jnp.max(s, axis=0, keepdims=True)
    if sink4 is not None:
        m = jnp.maximum(m, sink4)
    pv = _dot(_with_ones(vt), jnp.exp2(s - m).astype(CDT))
    den = pv[HEAD_DIM:HEAD_DIM + 1]
    if sink4 is not None:
        den = den + jnp.exp2(sink4 - m)
    return pv[0:HEAD_DIM] * (1.0 / den)


def _swa_kernel(sink_ref, q_ref, k_ref, vt_ref, o_ref, *, tq):
    qs = pl.program_id(1) * tq
    t = qs + lax.broadcasted_iota(jnp.int32, (1, tq), 1)
    t4 = jnp.concatenate([t] * GROUP_HEADS, axis=1)
    span = SWA_WINDOW + tq
    k0 = pl.multiple_of(jnp.maximum(qs - SWA_WINDOW, 0), LANES)
    scores = [_window_scores(_group_queries(q_ref, g), t4, k_ref[0, g, pl.ds(k0, span), :],
                             k0, SWA_WINDOW) for g in range(SWA_KV_HEADS)]
    for g in range(SWA_KV_HEADS):
        sink4 = jnp.concatenate([jnp.full((1, tq), sink_ref[g * GROUP_HEADS + r] * LOG2E, jnp.float32)
                                 for r in range(GROUP_HEADS)], axis=1)
        o4 = _window_output(scores[g], vt_ref[0, g * HEAD_DIM:(g + 1) * HEAD_DIM, pl.ds(k0, span)], sink4)
        for r in range(GROUP_HEADS):
            row = (g * GROUP_HEADS + r) * HEAD_DIM
            o_ref[0, row:row + HEAD_DIM, :] = o4[:, r * tq:(r + 1) * tq].astype(o_ref.dtype)


def _swa(sinks, o1, kk, tq):
    B, _, S = o1.shape
    rows = SWA_HEADS * HEAD_DIM
    vrows = SWA_KV_HEADS * HEAD_DIM
    return pl.pallas_call(
        functools.partial(_swa_kernel, tq=tq),
        grid=(B, S // tq),
        in_specs=[pl.BlockSpec(memory_space=pltpu.SMEM),
                  pl.BlockSpec((1, rows, tq), lambda b, i: (b, O_AQ // rows, i)),
                  pl.BlockSpec((1, SWA_KV_HEADS, S, HEAD_DIM), lambda b, i: (b, 0, 0, 0)),
                  pl.BlockSpec((1, vrows, S), lambda b, i: (b, O_VV // vrows, 0))],
        out_specs=pl.BlockSpec((1, rows, tq), lambda b, i: (b, 0, i)),
        out_shape=jax.ShapeDtypeStruct((B, rows, S), CDT),
        **_opts("swa_attn", "parallel", "parallel"),
    )(sinks, o1, kk, o1)


def _nsa_kernel(q_ref, qr_ref, cend_ref, kc_ref, vct_ref, ks_ref, vst_ref, kw_ref, vwt_ref,
                gate_ref, o_ref, bias_ref, s_a, s_b, *, tq, n_sel):
    qs = pl.program_id(1) * tq
    nc = kc_ref.shape[2]
    nj = nc // 4
    G = NSA_KV_GROUPS
    t = qs + lax.broadcasted_iota(jnp.int32, (1, tq), 1)
    t2 = jnp.concatenate([t] * 2, axis=1)
    t4 = jnp.concatenate([t] * GROUP_HEADS, axis=1)
    jrow = lax.broadcasted_iota(jnp.int32, (nj, tq), 0)
    jf = jrow.astype(jnp.float32)
    cur = lax.shift_right_arithmetic(t, int(np.log2(NSA_SEL_BLOCK)))
    bonus = jnp.where((jrow == 0) | (jrow == cur) | (jrow == cur - 1), NSA_FORCE_BONUS, 0.0)
    span = NSA_WINDOW + tq
    kw0 = pl.multiple_of(jnp.maximum(qs - NSA_WINDOW, 0), LANES)

    qr4 = [_group_queries(qr_ref, g) for g in range(G)]
    s_cmp = [_dot(kc_ref[0, g], _group_queries(q_ref, g)) for g in range(G)]
    s_win = [_window_scores(qr4[g], t4, kw_ref[0, g, pl.ds(kw0, span), :], kw0, NSA_WINDOW)
             for g in range(G)]
    p_cmp = []
    for g in range(G):
        s = jnp.where(cend_ref[...] <= t4, s_cmp[g], -jnp.inf)
        m = jnp.max(s, axis=0, keepdims=True)
        m = jnp.where(m == -jnp.inf, 0.0, m)
        e = jnp.exp2(s - m)
        den = jnp.sum(e, axis=0, keepdims=True)
        p_cmp.append(e * (1.0 / jnp.where(den > 0, den, 1.0)))
    o_cmps = [_dot(vct_ref[0, g], p_cmp[g].astype(CDT)) for g in range(G)]
    o_wins = [_window_output(s_win[g], vwt_ref[0, g * HEAD_DIM:(g + 1) * HEAD_DIM, pl.ds(kw0, span)])
              for g in range(G)]

    fixed = []
    for g in range(G):
        p, o_cmp, o_win = p_cmp[g], o_cmps[g], o_wins[g]
        psum = p[:, 0:tq]
        for r in range(1, GROUP_HEADS):
            psum = psum + p[:, r * tq:(r + 1) * tq]
        p3 = psum[3 * nj:4 * nj]
        prev = jnp.where(jrow == 0, 0.0, pltpu.roll(p3, 1, 0))
        imp = prev + 2.0 * (psum[0:nj] + psum[nj:2 * nj] + psum[2 * nj:3 * nj]) + p3

        score = jnp.where(jrow <= cur, imp + bonus, -jnp.inf)
        for _ in range(n_sel):
            mx = jnp.max(score, axis=0, keepdims=True)
            idx = jnp.min(jnp.where(score == mx, jf, float(nj)), axis=0, keepdims=True)
            idx = jnp.where(mx > -jnp.inf, idx, float(nj))
            score = jnp.where(jf == idx, -jnp.inf, score)
        bias_ref[g] = jnp.where(jrow <= cur, jnp.where(score == -jnp.inf, 0.0, MASK_VALUE), MASK_VALUE)

        parts = []
        for r in range(GROUP_HEADS):
            sl = slice(r * tq, (r + 1) * tq)
            g0 = gate_ref[0, g, 3 * r:3 * r + 1, :]
            g2 = gate_ref[0, g, 3 * r + 2:3 * r + 3, :]
            parts.append((g0 * o_cmp[:, sl], g2 * o_win[:, sl]))
        fixed.append(parts)

    def augmented_queries(kt):
        out = []
        for g in range(G):
            b8 = bias_ref[g, pl.ds(kt * SEL_BLOCKS_PER_TILE, SEL_BLOCKS_PER_TILE), :]
            b16 = jnp.concatenate([b8, jnp.zeros_like(b8)], axis=0)
            out.append(jnp.concatenate(
                [qr4[g], jnp.concatenate([b16] * GROUP_HEADS, axis=1).astype(CDT)], axis=0))
        return out

    def score_chain(kt, s_ref, qa, c):
        g, hp = divmod(c, 2)
        k0 = pl.multiple_of(kt * TK_SEL, TK_SEL)
        s_ref[c] = _dot(ks_ref[0, g, pl.ds(k0, TK_SEL), :],
                        qa[g][:, hp * 2 * tq:(hp + 1) * 2 * tq])

    def consume_chain(kt, s_ref, c, st, diagonal):
        g = c // 2
        k0 = pl.multiple_of(kt * TK_SEL, TK_SEL)
        sc = s_ref[c]
        if diagonal:
            kpos = k0 + lax.broadcasted_iota(jnp.int32, (TK_SEL, 1), 0)
            sc = jnp.where(kpos <= t2, sc, MASK_VALUE)
        vt_aug = _with_ones(vst_ref[0, g * HEAD_DIM:(g + 1) * HEAD_DIM, pl.ds(k0, TK_SEL)])
        return _softmax_step(sc, st[0], st[1], vt_aug)

    def step(kt, s_cur, s_nxt, state, diagonal=False):
        n_chain = 2 * G
        if s_nxt is not None:
            qa = augmented_queries(kt + 1)
            for c in range(QK_LEAD):
                score_chain(kt + 1, s_nxt, qa, c)
        out = []
        for c in range(n_chain):
            out.append(consume_chain(kt, s_cur, c, state[c], diagonal))
            if s_nxt is not None and c + QK_LEAD < n_chain:
                score_chain(kt + 1, s_nxt, qa, c + QK_LEAD)
        return tuple(out)

    def finish(state):
        for g in range(G):
            for r in range(GROUP_HEADS):
                _, acc = state[2 * g + r // 2]
                sl = slice((r % 2) * tq, (r % 2 + 1) * tq)
                o_sel = acc[0:HEAD_DIM, sl] * (1.0 / acc[HEAD_DIM:HEAD_DIM + 1, sl])
                g1 = gate_ref[0, g, 3 * r + 1:3 * r + 2, :]
                c_part, w_part = fixed[g][r]
                row = (g * GROUP_HEADS + r) * HEAD_DIM
                o_ref[0, row:row + HEAD_DIM, :] = (c_part + g1 * o_sel + w_part).astype(o_ref.dtype)

    def pair(j, state):
        return step(2 * j + 1, s_b, s_a, step(2 * j, s_a, s_b, state))

    init = tuple((jnp.full((1, 2 * tq), MASK_VALUE, jnp.float32),
                  jnp.zeros((HEAD_DIM + ONES_ROWS, 2 * tq), jnp.float32)) for _ in range(2 * G))
    n_full = qs // TK_SEL
    qa0 = augmented_queries(0)
    for c in range(2 * G):
        score_chain(0, s_a, qa0, c)
    state = lax.fori_loop(0, n_full // 2, pair, init)
    odd = n_full % 2 == 1

    @pl.when(odd)
    def _():
        finish(step(n_full, s_b, None, step(n_full - 1, s_a, s_b, state), diagonal=True))

    @pl.when(jnp.logical_not(odd))
    def _():
        finish(step(n_full, s_a, None, state, diagonal=True))


def _nsa(o1, cend, cmp_tok, cmp_fm, ks_aug, kk, gates, tq):
    B, _, S = o1.shape
    nc = cmp_tok.shape[2]
    nj = nc // 4
    n_sel = min(NSA_N_SEL, nj)
    G = NSA_KV_GROUPS
    rows = NSA_HEADS * HEAD_DIM
    vrows = G * HEAD_DIM
    return pl.pallas_call(
        functools.partial(_nsa_kernel, tq=tq, n_sel=n_sel),
        grid=(B, S // tq),
        in_specs=[pl.BlockSpec((1, rows, tq), lambda b, i: (b, O_BQ // rows, i)),
                  pl.BlockSpec((1, rows, tq), lambda b, i: (b, O_BQR // rows, i)),
                  _const_spec((nc, 1)),
                  pl.BlockSpec((1, G, nc, HEAD_DIM), lambda b, i: (b, 0, 0, 0)),
                  pl.BlockSpec((1, G, HEAD_DIM, nc), lambda b, i: (b, 1, 0, 0)),
                  pl.BlockSpec((1, G, S, KSEL_DIM), lambda b, i: (b, 0, 0, 0)),
                  pl.BlockSpec((1, vrows, S), lambda b, i: (b, O_VV // vrows + 1, 0)),
                  pl.BlockSpec((1, G, S, HEAD_DIM), lambda b, i: (b, 1, 0, 0)),
                  pl.BlockSpec((1, vrows, S), lambda b, i: (b, O_VV // vrows + 2, 0)),
                  pl.BlockSpec((1, G, 3 * GROUP_HEADS, tq), lambda b, i: (b, 0, 0, i))],
        out_specs=pl.BlockSpec((1, rows, tq), lambda b, i: (b, 0, i)),
        out_shape=jax.ShapeDtypeStruct((B, rows, S), CDT),
        scratch_shapes=[pltpu.VMEM((G, nj, tq), jnp.float32)]
                       + [pltpu.VMEM((2 * G, TK_SEL, 2 * tq), jnp.float32)] * 2,
        **_opts("nsa_attn", "parallel", "arbitrary"),
    )(o1, o1, cend, cmp_tok, cmp_fm, ks_aug, o1, kk, o1, gates)


def _mla_kernel(q_ref, k_ref, *rest, tq, hps):
    vt_refs, o_ref, s_a, s_b = rest[:hps], rest[hps], rest[hps + 1], rest[hps + 2]
    qs = pl.program_id(2) * tq
    q = [q_ref[0, h * MLA_QK_DIM:(h + 1) * MLA_QK_DIM, :] for h in range(hps)]
    t = qs + lax.broadcasted_iota(jnp.int32, (1, tq), 1)

    def score_chain(kt, s_ref, h):
        k0 = pl.multiple_of(kt * TK_MLA, TK_MLA)
        s_ref[h] = _dot(k_ref[0, h, pl.ds(k0, TK_MLA), :], q[h])

    def consume_chain(kt, s_ref, h, st, diagonal):
        k0 = pl.multiple_of(kt * TK_MLA, TK_MLA)
        sc = s_ref[h]
        if diagonal:
            kpos = k0 + lax.broadcasted_iota(jnp.int32, (TK_MLA, 1), 0)
            sc = jnp.where(kpos <= t, sc, MASK_VALUE)
        return _softmax_step(sc, st[0], st[1], _with_ones(vt_refs[h][0, :, pl.ds(k0, TK_MLA)]))

    def step(kt, s_cur, s_nxt, state, diagonal=False):
        lead = QK_LEAD
        if s_nxt is not None:
            for h in range(lead):
                score_chain(kt + 1, s_nxt, h)
        out = []
        for h in range(hps):
            out.append(consume_chain(kt, s_cur, h, state[h], diagonal))
            if s_nxt is not None and h + lead < hps:
                score_chain(kt + 1, s_nxt, h + lead)
        return tuple(out)

    def finish(state):
        for h in range(hps):
            _, acc = state[h]
            o_ref[0, h * MLA_V_DIM:(h + 1) * MLA_V_DIM, :] = (
                acc[0:MLA_V_DIM] * (1.0 / acc[MLA_V_DIM:MLA_V_DIM + 1])).astype(o_ref.dtype)

    def pair(j, state):
        return step(2 * j + 1, s_b, s_a, step(2 * j, s_a, s_b, state))

    init = tuple((jnp.full((1, tq), MASK_VALUE, jnp.float32),
                  jnp.zeros((MLA_V_DIM + ONES_ROWS, tq), jnp.float32)) for _ in range(hps))
    n_full = qs // TK_MLA
    for h in range(hps):
        score_chain(0, s_a, h)
    state = lax.fori_loop(0, n_full // 2, pair, init)
    odd = n_full % 2 == 1

    @pl.when(odd)
    def _():
        finish(step(n_full, s_b, None, step(n_full - 1, s_a, s_b, state), diagonal=True))

    @pl.when(jnp.logical_not(odd))
    def _():
        finish(step(n_full, s_a, None, state, diagonal=True))


def _mla(o1, kcat, tq):
    B, _, S = o1.shape
    hps = MLA_HEADS_PER_STEP
    v_spec = lambda h: pl.BlockSpec(
        (1, MLA_V_DIM, S), lambda b, hg, i: (b, O_KV // MLA_V_DIM + 2 * (hg * hps + h) + 1, 0))
    return pl.pallas_call(
        functools.partial(_mla_kernel, tq=tq, hps=hps),
        grid=(B, MLA_HEADS // hps, S // tq),
        in_specs=[pl.BlockSpec((1, hps * MLA_QK_DIM, tq),
                               lambda b, hg, i: (b, O_QM // (hps * MLA_QK_DIM) + hg, i)),
                  pl.BlockSpec((1, hps, S, MLA_QK_DIM), lambda b, hg, i: (b, hg, 0, 0))]
                 + [v_spec(h) for h in range(hps)],
        out_specs=pl.BlockSpec((1, hps * MLA_V_DIM, tq), lambda b, hg, i: (b, hg, i)),
        out_shape=jax.ShapeDtypeStruct((B, MLA_HEADS * MLA_V_DIM, S), CDT),
        scratch_shapes=[pltpu.VMEM((hps, TK_MLA, tq), jnp.float32)] * 2,
        **_opts("mla_attn", "parallel", "parallel", "parallel"),
    )(o1, kcat, *([o1] * hps))


def _merge_kernel(x_ref, g_ref, wg_ref, oa_ref, ob_ref, oc_ref, wa_ref, wb_ref, wc_ref, wo_ref, o_ref,
                  *, token_major):
    x = _load_residual(x_ref, token_major)
    d = x.shape[0]
    h = _rms_fm(x, g_ref[...]).astype(CDT)
    merged = None
    for i, (br_ref, w_ref) in enumerate(((oa_ref, wa_ref), (ob_ref, wb_ref), (oc_ref, wc_ref))):
        gate = jax.nn.sigmoid(_dot(wg_ref[i * d:(i + 1) * d, :], h))
        term = gate * _dot(w_ref[...], br_ref[0])
        merged = term if merged is None else merged + term
    o_ref[0] = x + _dot(wo_ref[...], merged.astype(CDT))


def _merge(x, g_col, wgT, oa, ob, oc, waT, wbT, wcT, woT, tm, token_major):
    B, D, S = (x.shape[0], x.shape[2], x.shape[1]) if token_major else x.shape
    tok = lambda rows: pl.BlockSpec((1, rows, tm), lambda b, i: (b, 0, i))
    return pl.pallas_call(
        functools.partial(_merge_kernel, token_major=token_major),
        grid=(B, S // tm),
        in_specs=[_residual_spec(D, tm, token_major), _const_spec((D, 1)), _const_spec(wgT.shape),
                  tok(oa.shape[1]), tok(ob.shape[1]), tok(oc.shape[1]),
                  _const_spec(waT.shape), _const_spec(wbT.shape), _const_spec(wcT.shape),
                  _const_spec(woT.shape)],
        out_specs=tok(D),
        out_shape=jax.ShapeDtypeStruct((B, D, S), x.dtype),
        **_opts("merge_out", "parallel", "parallel"),
    )(x, g_col, wgT, oa, ob, oc, waT, wbT, wcT, woT)


def _mem_kv_kernel(mem_ref, g_ref, wk_ref, wvt_ref, k_ref, vt_ref):
    m = mem_ref[0]
    ms = jnp.mean(m * m, axis=-1, keepdims=True)
    hm = (m * lax.rsqrt(ms + NORM_EPS) * g_ref[...]).astype(CDT)
    k_ref[0] = _dot(hm, wk_ref[...]).astype(k_ref.dtype)
    vt_ref[0] = _dot_nt(wvt_ref[...], hm).astype(vt_ref.dtype)


def _mem_kv(mem, g_row, wk, wvT):
    B, M, D = mem.shape
    n = wk.shape[1]
    return pl.pallas_call(
        _mem_kv_kernel,
        grid=(B,),
        in_specs=[pl.BlockSpec((1, M, D), lambda b: (b, 0, 0)), _const_spec((1, D)),
                  _const_spec(wk.shape), _const_spec(wvT.shape)],
        out_specs=[pl.BlockSpec((1, M, n), lambda b: (b, 0, 0)),
                   pl.BlockSpec((1, n, M), lambda b: (b, 0, 0))],
        out_shape=[jax.ShapeDtypeStruct((B, M, n), CDT), jax.ShapeDtypeStruct((B, n, M), CDT)],
        **_opts("mem_kv", "parallel"),
    )(mem, g_row, wk, wvT)


def _xattn_kernel(x_ref, g_ref, wq_ref, k_ref, vt_ref, wo_ref, o_ref):
    x = x_ref[0]
    h = _rms_fm(x, g_ref[...]).astype(CDT)
    q = (_dot(wq_ref[...], h) * (XATTN_HEAD_DIM ** -0.5 * LOG2E)).astype(CDT)
    head_rows = [slice(hd * XATTN_HEAD_DIM, (hd + 1) * XATTN_HEAD_DIM) for hd in range(XATTN_HEADS)]
    scores = [_dot(k_ref[0, :, rows], q[rows]) for rows in head_rows]
    probs = [jnp.exp2(s - jnp.max(s, axis=0, keepdims=True)).astype(CDT) for s in scores]
    outs = []
    for rows, e in zip(head_rows, probs):
        pv = _dot(_with_ones(vt_ref[0, rows, :]), e)
        outs.append((pv[0:XATTN_HEAD_DIM] * (1.0 / pv[XATTN_HEAD_DIM:XATTN_HEAD_DIM + 1])).astype(CDT))
    o_ref[0] = x + _dot(wo_ref[...], jnp.concatenate(outs, axis=0))


def _xattn(xT, g_col, wqT, kmem, vmemT, woT, tm):
    B, D, S = xT.shape
    M, n = kmem.shape[1], kmem.shape[2]
    tok = lambda rows: pl.BlockSpec((1, rows, tm), lambda b, i: (b, 0, i))
    return pl.pallas_call(
        _xattn_kernel,
        grid=(B, S // tm),
        in_specs=[tok(D), _const_spec((D, 1)), _const_spec(wqT.shape),
                  pl.BlockSpec((1, M, n), lambda b, i: (b, 0, 0)),
                  pl.BlockSpec((1, n, M), lambda b, i: (b, 0, 0)),
                  _const_spec(woT.shape)],
        out_specs=tok(D),
        out_shape=jax.ShapeDtypeStruct((B, D, S), xT.dtype),
        **_opts("xattn", "parallel", "parallel"),
    )(xT, g_col, wqT, kmem, vmemT, woT)


def _ffn_kernel(x_ref, g_ref, wgu_ref, wd_ref, *rest, d_ff, final_norm):
    o_ref = rest[-1]
    x = x_ref[0]
    h = _rms_fm(x, g_ref[...]).astype(CDT)
    acc = x
    for c in range(d_ff // FF_CHUNK):
        r = c * FF_CHUNK
        gate = _dot(wgu_ref[r:r + FF_CHUNK, :], h)
        up = _dot(wgu_ref[d_ff + r:d_ff + r + FF_CHUNK, :], h)
        act = (gate * jax.nn.sigmoid(gate) * up).astype(CDT)
        acc = acc + _dot(wd_ref[:, r:r + FF_CHUNK], act)
    if final_norm:
        o_ref[0] = _rms_fm(acc, rest[0][...]).T
    else:
        o_ref[0] = acc


def _ffn(xT, g_col, wguT, wdT, tm, final_g_col=None):
    B, D, S = xT.shape
    d_ff = wdT.shape[1]
    final_norm = final_g_col is not None
    tok = lambda rows: pl.BlockSpec((1, rows, tm), lambda b, i: (b, 0, i))
    extra = [final_g_col] if final_norm else []
    return pl.pallas_call(
        functools.partial(_ffn_kernel, d_ff=d_ff, final_norm=final_norm),
        grid=(B, S // tm),
        in_specs=[tok(D), _const_spec((D, 1)), _const_spec(wguT.shape), _const_spec(wdT.shape)]
                 + [_const_spec((D, 1))] * len(extra),
        out_specs=_residual_spec(D, tm, final_norm),
        out_shape=jax.ShapeDtypeStruct((B, S, D) if final_norm else (B, D, S), xT.dtype),
        **_opts("ffn", "parallel", "parallel"),
    )(xT, g_col, wguT, wdT, *extra)


def _rope_tables(positions, dim):
    half = dim // 2
    inv_freq = ROPE_THETA ** (-jnp.arange(half, dtype=jnp.float32) / half)
    ang = positions.astype(jnp.float32)[:, None, :] * inv_freq[None, :, None]
    return jnp.cos(ang), jnp.sin(ang)


def _pack_mixer_weight(w_in):
    off = np.cumsum((0, 512, 128, 128, 512, 128, 128, 128, 128, 128, 128, 24, 384, 256, 32))
    a_q, a_k, a_v, b_q, b_kc, b_vc, b_ks, b_vs, b_kw, b_vw, b_g, c_qa, c_kv, c_kr = [
        w_in[:, off[i]:off[i + 1]] for i in range(14)]
    packed = jnp.concatenate([a_q, b_q, a_k, b_ks, b_kw, a_v, b_vs, b_vw, b_kc, b_vc,
                              c_qa, c_kv, c_kr, b_g], axis=1)
    packed = jnp.pad(packed, ((0, 0), (0, N_MIX - packed.shape[1])))
    return packed.T.astype(CDT), w_in[:, off[14]:].T.astype(CDT)


def kernel(x, mem, positions, norm_mix, w_in, swa_sinks, nsa_pe_k, nsa_pe_v, nsa_wk1, nsa_wk2,
           nsa_wv1, nsa_wv2, mla_q_norm, mla_w_q_b, mla_kv_norm, mla_w_kv_b, w_br_a, w_br_b,
           w_br_c, w_out, norm_xattn, norm_mem, w_xq, w_xkv, w_xo, norm_ffn, w_gate_up, w_down,
           norm_final):
    B, S, D = x.shape
    depth = w_in.shape[0]
    nj = S // NSA_SEL_BLOCK
    nc = 4 * nj
    assert S % TK_SEL == 0 and S % TM_PROJ == 0 and S >= NSA_WINDOW + TQ_NSA
    col = lambda v: v.reshape(-1, 1)
    wt = lambda w: w.T.astype(CDT)

    cos, sin = _rope_tables(positions, HEAD_DIM)
    cosm, sinm = _rope_tables(positions, MLA_ROPE_DIM)
    rr, jj = np.divmod(np.arange(nc), nj)
    cend = jnp.asarray(((4 * jj + rr) * NSA_CMP_STRIDE + NSA_CMP_BLOCK - 1).reshape(nc, 1), jnp.int32)

    res = x
    for l in range(depth):
        token_major = l == 0
        wmT, wgT = _pack_mixer_weight(w_in[l])
        o1, o2, kk, ks_aug, kcat = _proj(
            res, col(norm_mix[l]), wmT, cos, sin, cosm, sinm,
            col(mla_q_norm[l]), wt(mla_w_q_b[l]), col(mla_kv_norm[l]), wt(mla_w_kv_b[l]), TM_PROJ,
            token_major)
        kr = o2[:, P_C:P_C + 256].reshape(B, 4, HEAD_DIM, S).transpose(0, 1, 3, 2)
        kr = kr.reshape(B, 4, nj, 4, NSA_CMP_STRIDE * HEAD_DIM).transpose(0, 1, 3, 2, 4)
        kr = kr.reshape(B, 4, nc, NSA_CMP_STRIDE * HEAD_DIM)
        gates = o2[:, P_GATE:P_GATE + 24].reshape(B, NSA_KV_GROUPS, 3 * GROUP_HEADS, S)

        pe = jnp.stack([nsa_pe_k[l], nsa_pe_v[l]]).reshape(2, 2, NSA_CMP_STRIDE * HEAD_DIM)
        w1 = jnp.stack([nsa_wk1[l], nsa_wv1[l]]).astype(CDT)
        w2 = jnp.stack([nsa_wk2[l], nsa_wv2[l]]).astype(CDT)
        w2t = jnp.stack([nsa_wk2[l].T, nsa_wv2[l].T]).astype(CDT)
        cmp_tok, cmp_fm = _compress(kr, pe, w1, w2, w2t)

        o_a = _swa(swa_sinks[l], o1, kk, TQ_SWA)
        o_b = _nsa(o1, cend, cmp_tok, cmp_fm, ks_aug, kk, gates, TQ_NSA)
        o_c = _mla(o1, kcat, TQ_MLA)
        xT = _merge(res, col(norm_mix[l]), wgT, o_a, o_b, o_c,
                    wt(w_br_a[l]), wt(w_br_b[l]), wt(w_br_c[l]), wt(w_out[l]), TM_PROJ, token_major)

        n_kv = XATTN_HEADS * XATTN_HEAD_DIM
        kmem, vmemT = _mem_kv(mem, norm_mem[l].reshape(1, D), w_xkv[l][:, :n_kv].astype(CDT),
                              wt(w_xkv[l][:, n_kv:]))
        xT = _xattn(xT, col(norm_xattn[l]), wt(w_xq[l]), kmem, vmemT, wt(w_xo[l]), TM_PROJ)
        res = _ffn(xT, col(norm_ffn[l]), wt(w_gate_up[l]), wt(w_down[l]), TM_PROJ,
                   col(norm_final) if l == depth - 1 else None)
    return res
```

```python
import functools

import numpy as np
import jax
import jax.numpy as jnp
from jax import lax
from jax.experimental import pallas as pl
from jax.experimental.pallas import tpu as pltpu

HEAD_DIM = 64
ROPE_THETA = 10000.0
NORM_EPS = 1e-6
SWA_HEADS = 8
SWA_KV_HEADS = 2
SWA_WINDOW = 128
NSA_HEADS = 8
NSA_KV_GROUPS = 2
NSA_CMP_BLOCK = 32
NSA_CMP_STRIDE = 16
NSA_SEL_BLOCK = 64
NSA_N_SEL = 16
NSA_WINDOW = 512
NSA_FORCE_BONUS = 1e4
MLA_HEADS = 8
MLA_Q_RANK = 384
MLA_KV_RANK = 256
MLA_NOPE_DIM = 64
MLA_ROPE_DIM = 32
MLA_V_DIM = 64
MLA_QK_DIM = MLA_NOPE_DIM + MLA_ROPE_DIM
XATTN_HEADS = 4
XATTN_HEAD_DIM = 128
N_BRANCH = 3
GROUP_HEADS = 4
GROUP_ROWS = GROUP_HEADS * HEAD_DIM
LOG2E = 1.4426950408889634

V7X_VMEM_LIMIT_BYTES = 56 * 1024 * 1024
LANES = 128
ONES_ROWS = 16

CDT = jnp.bfloat16

MASK_VALUE = -1e30

TM_PROJ = 512
TQ_SWA = 256
TQ_NSA = 256
TK_SEL = 512
SEL_BLOCKS_PER_TILE = TK_SEL // NSA_SEL_BLOCK
SEL_CHAIN_LANES = 256
KSEL_DIM = HEAD_DIM + 16
TQ_MLA = 256
TK_MLA = 512
MLA_HEADS_PER_STEP = 8
QK_LEAD = 2
FF_CHUNK = 704

R_AQ, R_BQ, R_K, R_V, R_C, R_CQA, R_CKV, R_CKR = 0, 512, 1024, 1408, 1792, 2048, 2432, 2688
N_MIX = 2752
O_AQ, O_BQ, O_BQR, O_QM, O_VV, O_KV = 0, 512, 1024, 1536, 2304, 2688
N_OUT1 = 3712
N_GATE = NSA_HEADS * N_BRANCH


def _dot(a, b):
    return jnp.dot(a, b, preferred_element_type=jnp.float32)


def _dot_nt(a, b):
    return lax.dot_general(a, b, (((1,), (1,)), ((), ())), preferred_element_type=jnp.float32)


def _opts(name, *sem):
    return dict(name=name, compiler_params=pltpu.CompilerParams(
        dimension_semantics=sem, vmem_limit_bytes=V7X_VMEM_LIMIT_BYTES))


def _rms_fm(x, g_col):
    ms = jnp.mean(x * x, axis=0, keepdims=True)
    return x * lax.rsqrt(ms + NORM_EPS) * g_col


def _const_spec(shape):
    nd = len(shape)
    return pl.BlockSpec(shape, lambda *_: (0,) * nd)


def _residual_spec(d, tm, token_major):
    if token_major:
        return pl.BlockSpec((1, tm, d), lambda b, i: (b, i, 0))
    return pl.BlockSpec((1, d, tm), lambda b, i: (b, 0, i))


def _load_residual(x_ref, token_major):
    return x_ref[0].T if token_major else x_ref[0]


def _rope_store(o_ref, row0, y, cos, sin, n_heads, head_dim, scale):
    half = head_dim // 2
    for h in range(n_heads):
        x1 = y[h * head_dim:h * head_dim + half]
        x2 = y[h * head_dim + half:(h + 1) * head_dim]
        r = row0 + h * head_dim
        o_ref[0, r:r + half, :] = ((x1 * cos - x2 * sin) * scale).astype(o_ref.dtype)
        o_ref[0, r + half:r + head_dim, :] = ((x2 * cos + x1 * sin) * scale).astype(o_ref.dtype)


def _proj_kernel(x_ref, g_ref, wm_ref, cos_ref, sin_ref, cosm_ref, sinm_ref,
                 qn_ref, wqb_ref, kvn_ref, wkvb_ref, o1_ref, gate_ref, kcv_ref, kk_ref, ks_ref, kc_ref,
                 *, token_major):
    h = _rms_fm(_load_residual(x_ref, token_major), g_ref[...]).astype(CDT)
    cos, sin = cos_ref[0], sin_ref[0]
    cosm, sinm = cosm_ref[0], sinm_ref[0]
    qk_scale = HEAD_DIM ** -0.5 * LOG2E

    y = _dot(wm_ref[R_AQ:R_AQ + 512, :], h)
    _rope_store(o1_ref, O_AQ, y, cos, sin, SWA_HEADS, HEAD_DIM, qk_scale)
    y = _dot(wm_ref[R_BQ:R_BQ + 512, :], h)
    o1_ref[0, O_BQ:O_BQ + 512, :] = (y * qk_scale).astype(o1_ref.dtype)
    _rope_store(o1_ref, O_BQR, y, cos, sin, NSA_HEADS, HEAD_DIM, qk_scale)
    y = _dot(wm_ref[R_K:R_K + 384, :], h)
    tm = y.shape[1]
    tok = pl.program_id(1) * tm + lax.broadcasted_iota(jnp.int32, (KSEL_DIM - HEAD_DIM, tm), 1)
    blk = lax.shift_right_arithmetic(tok, int(np.log2(NSA_SEL_BLOCK))) & (SEL_BLOCKS_PER_TILE - 1)
    row = lax.broadcasted_iota(jnp.int32, (KSEL_DIM - HEAD_DIM, tm), 0)
    onehot = jnp.where(blk == row, 1.0, 0.0)
    half = HEAD_DIM // 2
    for kh in range(6):
        x1 = y[kh * HEAD_DIM:kh * HEAD_DIM + half]
        x2 = y[kh * HEAD_DIM + half:(kh + 1) * HEAD_DIM]
        rows = [x1 * cos - x2 * sin, x2 * cos + x1 * sin]
        if kh in (2, 3):
            ks_ref[0, kh - 2] = jnp.concatenate(rows + [onehot], axis=0).T.astype(ks_ref.dtype)
        else:
            kk_ref[0, kh if kh < 2 else kh - 2] = jnp.concatenate(rows, axis=0).T.astype(kk_ref.dtype)
    y = _dot(wm_ref[R_V:R_V + 384, :], h)
    o1_ref[0, O_VV:O_VV + 384, :] = y.astype(o1_ref.dtype)
    y = _dot(wm_ref[R_C:R_C + 256, :], h)
    for n in range(4):
        kcv_ref[0, n] = y[n * HEAD_DIM:(n + 1) * HEAD_DIM].T

    lat = _dot(wm_ref[R_CQA:R_CQA + MLA_Q_RANK, :], h)
    qm = _dot(wqb_ref[...], _rms_fm(lat, qn_ref[...]).astype(CDT))
    m_scale = MLA_QK_DIM ** -0.5 * LOG2E
    hr = MLA_ROPE_DIM // 2
    for hd in range(MLA_HEADS):
        r = hd * MLA_QK_DIM
        o1_ref[0, O_QM + r:O_QM + r + MLA_NOPE_DIM, :] = (
            qm[r:r + MLA_NOPE_DIM] * m_scale).astype(o1_ref.dtype)
        x1 = qm[r + MLA_NOPE_DIM:r + MLA_NOPE_DIM + hr]
        x2 = qm[r + MLA_NOPE_DIM + hr:r + MLA_QK_DIM]
        o1_ref[0, O_QM + r + MLA_NOPE_DIM:O_QM + r + MLA_NOPE_DIM + hr, :] = (
            (x1 * cosm - x2 * sinm) * m_scale).astype(o1_ref.dtype)
        o1_ref[0, O_QM + r + MLA_NOPE_DIM + hr:O_QM + r + MLA_QK_DIM, :] = (
            (x2 * cosm + x1 * sinm) * m_scale).astype(o1_ref.dtype)
    lat = _dot(wm_ref[R_CKV:R_CKV + MLA_KV_RANK, :], h)
    kv = _dot(wkvb_ref[...], _rms_fm(lat, kvn_ref[...]).astype(CDT))
    o1_ref[0, O_KV:O_KV + 1024, :] = kv.astype(o1_ref.dtype)
    y = _dot(wm_ref[R_CKR:R_CKR + 64, :], h)
    x1, x2 = y[0:hr], y[hr:2 * hr]
    k_pe = [x1 * cosm - x2 * sinm, x2 * cosm + x1 * sinm]
    gate_ref[0] = jax.nn.sigmoid(y[MLA_ROPE_DIM:MLA_ROPE_DIM + N_GATE])
    for hd in range(MLA_HEADS):
        k_nope = kv[hd * 2 * MLA_NOPE_DIM:hd * 2 * MLA_NOPE_DIM + MLA_NOPE_DIM]
        kc_ref[0, hd] = jnp.concatenate([k_nope] + k_pe, axis=0).T.astype(kc_ref.dtype)


def _proj(x, g_col, wmT, cos, sin, cosm, sinm, qn, wqbT, kvn, wkvbT, tm, token_major):
    B, D, S = (x.shape[0], x.shape[2], x.shape[1]) if token_major else x.shape
    tok = lambda rows: pl.BlockSpec((1, rows, tm), lambda b, i: (b, 0, i))
    return pl.pallas_call(
        functools.partial(_proj_kernel, token_major=token_major),
        grid=(B, S // tm),
        in_specs=[_residual_spec(D, tm, token_major), _const_spec((D, 1)), _const_spec(wmT.shape),
                  tok(32), tok(32), tok(16), tok(16),
                  _const_spec(qn.shape), _const_spec(wqbT.shape),
                  _const_spec(kvn.shape), _const_spec(wkvbT.shape)],
        out_specs=[tok(N_OUT1), tok(N_GATE),
                   pl.BlockSpec((1, 4, tm, HEAD_DIM), lambda b, i: (b, 0, i, 0)),
                   pl.BlockSpec((1, 4, tm, HEAD_DIM), lambda b, i: (b, 0, i, 0)),
                   pl.BlockSpec((1, NSA_KV_GROUPS, tm, KSEL_DIM), lambda b, i: (b, 0, i, 0)),
                   pl.BlockSpec((1, MLA_HEADS, tm, MLA_QK_DIM), lambda b, i: (b, 0, i, 0))],
        out_shape=[jax.ShapeDtypeStruct((B, N_OUT1, S), CDT),
                   jax.ShapeDtypeStruct((B, N_GATE, S), jnp.float32),
                   jax.ShapeDtypeStruct((B, 4, S, HEAD_DIM), jnp.float32),
                   jax.ShapeDtypeStruct((B, 4, S, HEAD_DIM), CDT),
                   jax.ShapeDtypeStruct((B, NSA_KV_GROUPS, S, KSEL_DIM), CDT),
                   jax.ShapeDtypeStruct((B, MLA_HEADS, S, MLA_QK_DIM), CDT)],
        **_opts("mixer_proj", "parallel", "parallel"),
    )(x, g_col, wmT, cos, sin, cosm, sinm, qn, wqbT, kvn, wkvbT)


def _compress_kernel(kr_ref, pe_ref, w1_ref, w2_ref, w2t_ref, tok_ref, fm_ref, hid_ref):
    kr = kr_ref[0, 0]
    nc, half = kr.shape
    nj = nc // 4
    a = _dot((kr + pe_ref[0, 0:1, :]).astype(CDT), w1_ref[0, 0:half, :])
    bm = _dot((kr + pe_ref[0, 1:2, :]).astype(CDT), w1_ref[0, half:2 * half, :])
    nxt = pltpu.roll(bm, nc - 1, 0)
    row = lax.broadcasted_iota(jnp.int32, (nc, 1), 0)
    pre = a + jnp.where(row == nc - 1, 0.0, nxt)
    hid_ref[...] = pre * jax.nn.sigmoid(pre)
    hid = jnp.concatenate([hid_ref[pl.ds(r, nj, stride=4), :] for r in range(4)], axis=0).astype(CDT)
    tok_ref[0, 0] = _dot(hid, w2_ref[0]).astype(tok_ref.dtype)
    fm_ref[0, 0] = _dot_nt(w2t_ref[0], hid).astype(fm_ref.dtype)


def _compress(kr, pe, w1, w2, w2t):
    B, _, NC, W = kr.shape
    return pl.pallas_call(
        _compress_kernel,
        grid=(B, 4),
        in_specs=[pl.BlockSpec((1, 1, NC, W), lambda b, n: (b, n, 0, 0)),
                  pl.BlockSpec((1, 2, W), lambda b, n: (n // 2, 0, 0)),
                  pl.BlockSpec((1, 2 * W, LANES), lambda b, n: (n // 2, 0, 0)),
                  pl.BlockSpec((1, LANES, HEAD_DIM), lambda b, n: (n // 2, 0, 0)),
                  pl.BlockSpec((1, HEAD_DIM, LANES), lambda b, n: (n // 2, 0, 0))],
        out_specs=[pl.BlockSpec((1, 1, NC, HEAD_DIM), lambda b, n: (b, n, 0, 0)),
                   pl.BlockSpec((1, 1, HEAD_DIM, NC), lambda b, n: (b, n, 0, 0))],
        out_shape=[jax.ShapeDtypeStruct((B, 4, NC, HEAD_DIM), CDT),
                   jax.ShapeDtypeStruct((B, 4, HEAD_DIM, NC), CDT)],
        scratch_shapes=[pltpu.VMEM((NC, LANES), jnp.float32)],
        **_opts("nsa_compress", "parallel", "parallel"),
    )(kr, pe, w1, w2, w2t)


def _group_queries(q_ref, g):
    return jnp.concatenate([q_ref[0, (g * GROUP_HEADS + r) * HEAD_DIM:(g * GROUP_HEADS + r + 1) * HEAD_DIM, :]
                            for r in range(GROUP_HEADS)], axis=1)


def _with_ones(vt):
    return jnp.concatenate([vt, jnp.ones((ONES_ROWS, vt.shape[1]), vt.dtype)], axis=0)


def _softmax_probs(sc, m_run):
    m_new = jnp.maximum(m_run, jnp.max(sc, axis=0, keepdims=True))
    return m_new, jnp.exp2(sc - m_new).astype(CDT), jnp.exp2(m_run - m_new)


def _softmax_step(sc, m_run, acc, vt_aug):
    m_new, p, alpha = _softmax_probs(sc, m_run)
    return m_new, alpha * acc + _dot(vt_aug, p)


def _window_scores(q4, t4, k, k0, window):
    span = k.shape[0]
    s = _dot(k, q4)
    kpos = k0 + lax.broadcasted_iota(jnp.int32, (span, 1), 0)
    return jnp.where(kpos <= t4, jnp.where(kpos > t4 - window, s, -jnp.inf), -jnp.inf)


def _window_output(s, vt, sink4=None):
    m = jnp.max(s, axis=0, keepdims=True)
    if sink4 is not None:
        m = jnp.maximum(m, sink4)
    pv = _dot(_with_ones(vt), jnp.exp2(s - m).astype(CDT))
    den = pv[HEAD_DIM:HEAD_DIM + 1]
    if sink4 is not None:
        den = den + jnp.exp2(sink4 - m)
    return pv[0:HEAD_DIM] * (1.0 / den)


def _swa_kernel(sink_ref, q_ref, k_ref, vt_ref, o_ref, *, tq):
    qs = pl.program_id(1) * tq
    t = qs + lax.broadcasted_iota(jnp.int32, (1, tq), 1)
    t4 = jnp.concatenate([t] * GROUP_HEADS, axis=1)
    span = SWA_WINDOW + tq
    k0 = pl.multiple_of(jnp.maximum(qs - SWA_WINDOW, 0), LANES)
    scores = [_window_scores(_group_queries(q_ref, g), t4, k_ref[0, g, pl.ds(k0, span), :],
                             k0, SWA_WINDOW) for g in range(SWA_KV_HEADS)]
    for g in range(SWA_KV_HEADS):
        sink4 = jnp.concatenate([jnp.full((1, tq), sink_ref[g * GROUP_HEADS + r] * LOG2E, jnp.float32)
                                 for r in range(GROUP_HEADS)], axis=1)
        o4 = _window_output(scores[g], vt_ref[0, g * HEAD_DIM:(g + 1) * HEAD_DIM, pl.ds(k0, span)], sink4)
        for r in range(GROUP_HEADS):
            row = (g * GROUP_HEADS + r) * HEAD_DIM
            o_ref[0, row:row + HEAD_DIM, :] = o4[:, r * tq:(r + 1) * tq].astype(o_ref.dtype)


def _swa(sinks, o1, kk, tq):
    B, _, S = o1.shape
    rows = SWA_HEADS * HEAD_DIM
    vrows = SWA_KV_HEADS * HEAD_DIM
    assert O_AQ % rows == 0 and O_VV % vrows == 0
    return pl.pallas_call(
        functools.partial(_swa_kernel, tq=tq),
        grid=(B, S // tq),
        in_specs=[pl.BlockSpec(memory_space=pltpu.SMEM),
                  pl.BlockSpec((1, rows, tq), lambda b, i: (b, O_AQ // rows, i)),
                  pl.BlockSpec((1, SWA_KV_HEADS, S, HEAD_DIM), lambda b, i: (b, 0, 0, 0)),
                  pl.BlockSpec((1, vrows, S), lambda b, i: (b, O_VV // vrows, 0))],
        out_specs=pl.BlockSpec((1, rows, tq), lambda b, i: (b, 0, i)),
        out_shape=jax.ShapeDtypeStruct((B, rows, S), CDT),
        **_opts("swa_attn", "parallel", "parallel"),
    )(sinks, o1, kk, o1)


def _nsa_kernel(q_ref, qr_ref, cend_ref, kc_ref, vct_ref, ks_ref, vst_ref, kw_ref, vwt_ref,
                gate_ref, o_ref, bias_ref, s_a, s_b, *, tq, n_sel):
    qs = pl.program_id(1) * tq
    nc = kc_ref.shape[2]
    nj = nc // 4
    G = NSA_KV_GROUPS
    t = qs + lax.broadcasted_iota(jnp.int32, (1, tq), 1)
    hpc = SEL_CHAIN_LANES // tq
    cpg = GROUP_HEADS // hpc
    tc = jnp.concatenate([t] * hpc, axis=1)
    t4 = jnp.concatenate([t] * GROUP_HEADS, axis=1)
    jrow = lax.broadcasted_iota(jnp.int32, (nj, tq), 0)
    jf = jrow.astype(jnp.float32)
    cur = lax.shift_right_arithmetic(t, int(np.log2(NSA_SEL_BLOCK)))
    bonus = jnp.where((jrow == 0) | (jrow == cur) | (jrow == cur - 1), NSA_FORCE_BONUS, 0.0)
    span = NSA_WINDOW + tq
    kw0 = pl.multiple_of(jnp.maximum(qs - NSA_WINDOW, 0), LANES)

    qr4 = [_group_queries(qr_ref, g) for g in range(G)]
    s_cmp = [_dot(kc_ref[0, g], _group_queries(q_ref, g)) for g in range(G)]
    s_win = [_window_scores(qr4[g], t4, kw_ref[0, g, pl.ds(kw0, span), :], kw0, NSA_WINDOW)
             for g in range(G)]
    p_cmp = []
    for g in range(G):
        s = jnp.where(cend_ref[...] <= t4, s_cmp[g], -jnp.inf)
        m = jnp.max(s, axis=0, keepdims=True)
        m = jnp.where(m == -jnp.inf, 0.0, m)
        e = jnp.exp2(s - m)
        den = jnp.sum(e, axis=0, keepdims=True)
        p_cmp.append(e * (1.0 / jnp.where(den > 0, den, 1.0)))
    o_cmps = [_dot(vct_ref[0, g], p_cmp[g].astype(CDT)) for g in range(G)]
    o_wins = [_window_output(s_win[g], vwt_ref[0, g * HEAD_DIM:(g + 1) * HEAD_DIM, pl.ds(kw0, span)])
              for g in range(G)]

    fixed = []
    for g in range(G):
        p, o_cmp, o_win = p_cmp[g], o_cmps[g], o_wins[g]
        psum = p[:, 0:tq]
        for r in range(1, GROUP_HEADS):
            psum = psum + p[:, r * tq:(r + 1) * tq]
        p3 = psum[3 * nj:4 * nj]
        prev = jnp.where(jrow == 0, 0.0, pltpu.roll(p3, 1, 0))
        imp = prev + 2.0 * (psum[0:nj] + psum[nj:2 * nj] + psum[2 * nj:3 * nj]) + p3

        score = jnp.where(jrow <= cur, imp + bonus, -jnp.inf)
        for _ in range(n_sel):
            mx = jnp.max(score, axis=0, keepdims=True)
            idx = jnp.min(jnp.where(score == mx, jf, float(nj)), axis=0, keepdims=True)
            idx = jnp.where(mx > -jnp.inf, idx, float(nj))
            score = jnp.where(jf == idx, -jnp.inf, score)
        bias_ref[g] = jnp.where(jrow <= cur, jnp.where(score == -jnp.inf, 0.0, MASK_VALUE), MASK_VALUE)

        parts = []
        for r in range(GROUP_HEADS):
            sl = slice(r * tq, (r + 1) * tq)
            g0 = gate_ref[0, g, 3 * r:3 * r + 1, :]
            g2 = gate_ref[0, g, 3 * r + 2:3 * r + 3, :]
            parts.append((g0 * o_cmp[:, sl], g2 * o_win[:, sl]))
        fixed.append(parts)

    def augmented_queries(kt):
        out = []
        for g in range(G):
            b8 = bias_ref[g, pl.ds(kt * SEL_BLOCKS_PER_TILE, SEL_BLOCKS_PER_TILE), :]
            b16 = jnp.concatenate([b8, jnp.zeros_like(b8)], axis=0)
            out.append(jnp.concatenate(
                [qr4[g], jnp.concatenate([b16] * GROUP_HEADS, axis=1).astype(CDT)], axis=0))
        return out

    def score_chain(kt, s_ref, qa, c):
        g, hp = divmod(c, cpg)
        k0 = pl.multiple_of(kt * TK_SEL, TK_SEL)
        s_ref[c] = _dot(ks_ref[0, g, pl.ds(k0, TK_SEL), :],
                        qa[g][:, hp * SEL_CHAIN_LANES:(hp + 1) * SEL_CHAIN_LANES])

    def consume_chain(kt, s_ref, c, st, diagonal):
        g = c // cpg
        k0 = pl.multiple_of(kt * TK_SEL, TK_SEL)
        sc = s_ref[c]
        if diagonal:
            kpos = k0 + lax.broadcasted_iota(jnp.int32, (TK_SEL, 1), 0)
            sc = jnp.where(kpos <= tc, sc, MASK_VALUE)
        vt_aug = _with_ones(vst_ref[0, g * HEAD_DIM:(g + 1) * HEAD_DIM, pl.ds(k0, TK_SEL)])
        return _softmax_step(sc, st[0], st[1], vt_aug)

    def step(kt, s_cur, s_nxt, state, diagonal=False):
        n_chain = cpg * G
        if s_nxt is not None:
            qa = augmented_queries(kt + 1)
            for c in range(QK_LEAD):
                score_chain(kt + 1, s_nxt, qa, c)
        out = []
        for c in range(n_chain):
            out.append(consume_chain(kt, s_cur, c, state[c], diagonal))
            if s_nxt is not None and c + QK_LEAD < n_chain:
                score_chain(kt + 1, s_nxt, qa, c + QK_LEAD)
        return tuple(out)

    def finish(state):
        for g in range(G):
            for r in range(GROUP_HEADS):
                _, acc = state[cpg * g + r // hpc]
                sl = slice((r % hpc) * tq, (r % hpc + 1) * tq)
                o_sel = acc[0:HEAD_DIM, sl] * (1.0 / acc[HEAD_DIM:HEAD_DIM + 1, sl])
                g1 = gate_ref[0, g, 3 * r + 1:3 * r + 2, :]
                c_part, w_part = fixed[g][r]
                row = (g * GROUP_HEADS + r) * HEAD_DIM
                o_ref[0, row:row + HEAD_DIM, :] = (c_part + g1 * o_sel + w_part).astype(o_ref.dtype)

    def pair(j, state):
        return step(2 * j + 1, s_b, s_a, step(2 * j, s_a, s_b, state))

    init = tuple((jnp.full((1, SEL_CHAIN_LANES), MASK_VALUE, jnp.float32),
                  jnp.zeros((HEAD_DIM + ONES_ROWS, SEL_CHAIN_LANES), jnp.float32))
                 for _ in range(cpg * G))
    n_full = qs // TK_SEL
    qa0 = augmented_queries(0)
    for c in range(cpg * G):
        score_chain(0, s_a, qa0, c)
    state = lax.fori_loop(0, n_full // 2, pair, init)
    odd = n_full % 2 == 1

    @pl.when(odd)
    def _():
        finish(step(n_full, s_b, None, step(n_full - 1, s_a, s_b, state), diagonal=True))

    @pl.when(jnp.logical_not(odd))
    def _():
        finish(step(n_full, s_a, None, state, diagonal=True))


def _nsa(o1, cend, cmp_tok, cmp_fm, ks_aug, kk, gates, tq):
    B, _, S = o1.shape
    nc = cmp_tok.shape[2]
    nj = nc // 4
    n_sel = min(NSA_N_SEL, nj)
    G = NSA_KV_GROUPS
    rows = NSA_HEADS * HEAD_DIM
    vrows = G * HEAD_DIM
    assert O_BQ % rows == 0 and O_BQR % rows == 0 and O_VV % vrows == 0
    assert SEL_CHAIN_LANES % tq == 0 and tq <= TK_SEL
    return pl.pallas_call(
        functools.partial(_nsa_kernel, tq=tq, n_sel=n_sel),
        grid=(B, S // tq),
        in_specs=[pl.BlockSpec((1, rows, tq), lambda b, i: (b, O_BQ // rows, i)),
                  pl.BlockSpec((1, rows, tq), lambda b, i: (b, O_BQR // rows, i)),
                  _const_spec((nc, 1)),
                  pl.BlockSpec((1, G, nc, HEAD_DIM), lambda b, i: (b, 0, 0, 0)),
                  pl.BlockSpec((1, G, HEAD_DIM, nc), lambda b, i: (b, 1, 0, 0)),
                  pl.BlockSpec((1, G, S, KSEL_DIM), lambda b, i: (b, 0, 0, 0)),
                  pl.BlockSpec((1, vrows, S), lambda b, i: (b, O_VV // vrows + 1, 0)),
                  pl.BlockSpec((1, G, S, HEAD_DIM), lambda b, i: (b, 1, 0, 0)),
                  pl.BlockSpec((1, vrows, S), lambda b, i: (b, O_VV // vrows + 2, 0)),
                  pl.BlockSpec((1, G, 3 * GROUP_HEADS, tq), lambda b, i: (b, 0, 0, i))],
        out_specs=pl.BlockSpec((1, rows, tq), lambda b, i: (b, 0, i)),
        out_shape=jax.ShapeDtypeStruct((B, rows, S), CDT),
        scratch_shapes=[pltpu.VMEM((G, nj, tq), jnp.float32)]
                       + [pltpu.VMEM((NSA_HEADS * tq // SEL_CHAIN_LANES, TK_SEL, SEL_CHAIN_LANES),
                                     jnp.float32)] * 2,
        **_opts("nsa_attn", "parallel", "arbitrary"),
    )(o1, o1, cend, cmp_tok, cmp_fm, ks_aug, o1, kk, o1, gates)


def _mla_kernel(q_ref, k_ref, *rest, tq, hps):
    vt_refs, o_ref, s_a, s_b = rest[:hps], rest[hps], rest[hps + 1], rest[hps + 2]
    qs = pl.program_id(2) * tq
    q = [q_ref[0, h * MLA_QK_DIM:(h + 1) * MLA_QK_DIM, :] for h in range(hps)]
    t = qs + lax.broadcasted_iota(jnp.int32, (1, tq), 1)

    def score_chain(kt, s_ref, h):
        k0 = pl.multiple_of(kt * TK_MLA, TK_MLA)
        s_ref[h] = _dot(k_ref[0, h, pl.ds(k0, TK_MLA), :], q[h])

    def consume_chain(kt, s_ref, h, st, diagonal):
        k0 = pl.multiple_of(kt * TK_MLA, TK_MLA)
        sc = s_ref[h]
        if diagonal:
            kpos = k0 + lax.broadcasted_iota(jnp.int32, (TK_MLA, 1), 0)
            sc = jnp.where(kpos <= t, sc, MASK_VALUE)
        return _softmax_step(sc, st[0], st[1], _with_ones(vt_refs[h][0, :, pl.ds(k0, TK_MLA)]))

    def step(kt, s_cur, s_nxt, state, diagonal=False):
        lead = QK_LEAD
        if s_nxt is not None:
            for h in range(lead):
                score_chain(kt + 1, s_nxt, h)
        out = []
        for h in range(hps):
            out.append(consume_chain(kt, s_cur, h, state[h], diagonal))
            if s_nxt is not None and h + lead < hps:
                score_chain(kt + 1, s_nxt, h + lead)
        return tuple(out)

    def finish(state):
        for h in range(hps):
            _, acc = state[h]
            o_ref[0, h * MLA_V_DIM:(h + 1) * MLA_V_DIM, :] = (
                acc[0:MLA_V_DIM] * (1.0 / acc[MLA_V_DIM:MLA_V_DIM + 1])).astype(o_ref.dtype)

    def pair(j, state):
        return step(2 * j + 1, s_b, s_a, step(2 * j, s_a, s_b, state))

    init = tuple((jnp.full((1, tq), MASK_VALUE, jnp.float32),
                  jnp.zeros((MLA_V_DIM + ONES_ROWS, tq), jnp.float32)) for _ in range(hps))
    n_full = qs // TK_MLA
    for h in range(hps):
        score_chain(0, s_a, h)
    state = lax.fori_loop(0, n_full // 2, pair, init)
    odd = n_full % 2 == 1

    @pl.when(odd)
    def _():
        finish(step(n_full, s_b, None, step(n_full - 1, s_a, s_b, state), diagonal=True))

    @pl.when(jnp.logical_not(odd))
    def _():
        finish(step(n_full, s_a, None, state, diagonal=True))


def _mla(o1, kcat, tq):
    B, _, S = o1.shape
    hps = MLA_HEADS_PER_STEP
    assert O_QM % (hps * MLA_QK_DIM) == 0 and O_KV % MLA_V_DIM == 0 and MLA_HEADS % hps == 0
    v_spec = lambda h: pl.BlockSpec(
        (1, MLA_V_DIM, S), lambda b, hg, i: (b, O_KV // MLA_V_DIM + 2 * (hg * hps + h) + 1, 0),
        pipeline_mode=pl.Buffered(1))
    return pl.pallas_call(
        functools.partial(_mla_kernel, tq=tq, hps=hps),
        grid=(B, MLA_HEADS // hps, S // tq),
        in_specs=[pl.BlockSpec((1, hps * MLA_QK_DIM, tq),
                               lambda b, hg, i: (b, O_QM // (hps * MLA_QK_DIM) + hg, i)),
                  pl.BlockSpec((1, hps, S, MLA_QK_DIM), lambda b, hg, i: (b, hg, 0, 0),
                               pipeline_mode=pl.Buffered(1))]
                 + [v_spec(h) for h in range(hps)],
        out_specs=pl.BlockSpec((1, hps * MLA_V_DIM, tq), lambda b, hg, i: (b, hg, i)),
        out_shape=jax.ShapeDtypeStruct((B, MLA_HEADS * MLA_V_DIM, S), CDT),
        scratch_shapes=[pltpu.VMEM((hps, TK_MLA, tq), jnp.float32)] * 2,
        **_opts("mla_attn", "parallel", "parallel", "parallel"),
    )(o1, kcat, *([o1] * hps))


def _merge_kernel(x_ref, g_ref, wg_ref, oa_ref, ob_ref, oc_ref, wa_ref, wb_ref, wc_ref, wo_ref, o_ref,
                  *, token_major):
    x = _load_residual(x_ref, token_major)
    d = x.shape[0]
    h = _rms_fm(x, g_ref[...]).astype(CDT)
    merged = None
    for i, (br_ref, w_ref) in enumerate(((oa_ref, wa_ref), (ob_ref, wb_ref), (oc_ref, wc_ref))):
        gate = jax.nn.sigmoid(_dot(wg_ref[i * d:(i + 1) * d, :], h))
        term = gate * _dot(w_ref[...], br_ref[0])
        merged = term if merged is None else merged + term
    o_ref[0] = x + _dot(wo_ref[...], merged.astype(CDT))


def _merge(x, g_col, wgT, oa, ob, oc, waT, wbT, wcT, woT, tm, token_major):
    B, D, S = (x.shape[0], x.shape[2], x.shape[1]) if token_major else x.shape
    tok = lambda rows: pl.BlockSpec((1, rows, tm), lambda b, i: (b, 0, i))
    return pl.pallas_call(
        functools.partial(_merge_kernel, token_major=token_major),
        grid=(B, S // tm),
        in_specs=[_residual_spec(D, tm, token_major), _const_spec((D, 1)), _const_spec(wgT.shape),
                  tok(oa.shape[1]), tok(ob.shape[1]), tok(oc.shape[1]),
                  _const_spec(waT.shape), _const_spec(wbT.shape), _const_spec(wcT.shape),
                  _const_spec(woT.shape)],
        out_specs=tok(D),
        out_shape=jax.ShapeDtypeStruct((B, D, S), x.dtype),
        **_opts("merge_out", "parallel", "parallel"),
    )(x, g_col, wgT, oa, ob, oc, waT, wbT, wcT, woT)


def _mem_kv_kernel(mem_ref, g_ref, wk_ref, wvt_ref, k_ref, vt_ref):
    m = mem_ref[0]
    ms = jnp.mean(m * m, axis=-1, keepdims=True)
    hm = (m * lax.rsqrt(ms + NORM_EPS) * g_ref[...]).astype(CDT)
    k_ref[0] = _dot(hm, wk_ref[...]).astype(k_ref.dtype)
    vt_ref[0] = _dot_nt(wvt_ref[...], hm).astype(vt_ref.dtype)


def _mem_kv(mem, g_row, wk, wvT):
    B, M, D = mem.shape
    n = wk.shape[1]
    return pl.pallas_call(
        _mem_kv_kernel,
        grid=(B,),
        in_specs=[pl.BlockSpec((1, M, D), lambda b: (b, 0, 0)), _const_spec((1, D)),
                  _const_spec(wk.shape), _const_spec(wvT.shape)],
        out_specs=[pl.BlockSpec((1, M, n), lambda b: (b, 0, 0)),
                   pl.BlockSpec((1, n, M), lambda b: (b, 0, 0))],
        out_shape=[jax.ShapeDtypeStruct((B, M, n), CDT), jax.ShapeDtypeStruct((B, n, M), CDT)],
        **_opts("mem_kv", "parallel"),
    )(mem, g_row, wk, wvT)


def _xattn_kernel(x_ref, g_ref, wq_ref, k_ref, vt_ref, wo_ref, o_ref):
    x = x_ref[0]
    h = _rms_fm(x, g_ref[...]).astype(CDT)
    q = (_dot(wq_ref[...], h) * (XATTN_HEAD_DIM ** -0.5 * LOG2E)).astype(CDT)
    head_rows = [slice(hd * XATTN_HEAD_DIM, (hd + 1) * XATTN_HEAD_DIM) for hd in range(XATTN_HEADS)]
    scores = [_dot(k_ref[0, :, rows], q[rows]) for rows in head_rows]
    probs = [jnp.exp2(s - jnp.max(s, axis=0, keepdims=True)).astype(CDT) for s in scores]
    outs = []
    for rows, e in zip(head_rows, probs):
        pv = _dot(_with_ones(vt_ref[0, rows, :]), e)
        outs.append((pv[0:XATTN_HEAD_DIM] * (1.0 / pv[XATTN_HEAD_DIM:XATTN_HEAD_DIM + 1])).astype(CDT))
    o_ref[0] = x + _dot(wo_ref[...], jnp.concatenate(outs, axis=0))


def _xattn(xT, g_col, wqT, kmem, vmemT, woT, tm):
    B, D, S = xT.shape
    M, n = kmem.shape[1], kmem.shape[2]
    tok = lambda rows: pl.BlockSpec((1, rows, tm), lambda b, i: (b, 0, i))
    return pl.pallas_call(
        _xattn_kernel,
        grid=(B, S // tm),
        in_specs=[tok(D), _const_spec((D, 1)), _const_spec(wqT.shape),
                  pl.BlockSpec((1, M, n), lambda b, i: (b, 0, 0)),
                  pl.BlockSpec((1, n, M), lambda b, i: (b, 0, 0)),
                  _const_spec(woT.shape)],
        out_specs=tok(D),
        out_shape=jax.ShapeDtypeStruct((B, D, S), xT.dtype),
        **_opts("xattn", "parallel", "parallel"),
    )(xT, g_col, wqT, kmem, vmemT, woT)


def _ffn_kernel(x_ref, g_ref, wgu_ref, wd_ref, *rest, d_ff, final_norm):
    o_ref = rest[-1]
    x = x_ref[0]
    h = _rms_fm(x, g_ref[...]).astype(CDT)
    acc = x
    for c in range(d_ff // FF_CHUNK):
        r = c * FF_CHUNK
        gate = _dot(wgu_ref[r:r + FF_CHUNK, :], h)
        up = _dot(wgu_ref[d_ff + r:d_ff + r + FF_CHUNK, :], h)
        act = (gate * jax.nn.sigmoid(gate) * up).astype(CDT)
        acc = acc + _dot(wd_ref[:, r:r + FF_CHUNK], act)
    if final_norm:
        o_ref[0] = _rms_fm(acc, rest[0][...]).T
    else:
        o_ref[0] = acc


def _ffn(xT, g_col, wguT, wdT, tm, final_g_col=None):
    B, D, S = xT.shape
    d_ff = wdT.shape[1]
    final_norm = final_g_col is not None
    tok = lambda rows: pl.BlockSpec((1, rows, tm), lambda b, i: (b, 0, i))
    extra = [final_g_col] if final_norm else []
    return pl.pallas_call(
        functools.partial(_ffn_kernel, d_ff=d_ff, final_norm=final_norm),
        grid=(B, S // tm),
        in_specs=[tok(D), _const_spec((D, 1)), _const_spec(wguT.shape), _const_spec(wdT.shape)]
                 + [_const_spec((D, 1))] * len(extra),
        out_specs=_residual_spec(D, tm, final_norm),
        out_shape=jax.ShapeDtypeStruct((B, S, D) if final_norm else (B, D, S), xT.dtype),
        **_opts("ffn", "parallel", "parallel"),
    )(xT, g_col, wguT, wdT, *extra)


def _rope_tables(positions, dim):
    half = dim // 2
    inv_freq = ROPE_THETA ** (-jnp.arange(half, dtype=jnp.float32) / half)
    ang = positions.astype(jnp.float32)[:, None, :] * inv_freq[None, :, None]
    return jnp.cos(ang), jnp.sin(ang)


def _pack_mixer_weight(w_in):
    off = np.cumsum((0, 512, 128, 128, 512, 128, 128, 128, 128, 128, 128, 24, 384, 256, 32))
    a_q, a_k, a_v, b_q, b_kc, b_vc, b_ks, b_vs, b_kw, b_vw, b_g, c_qa, c_kv, c_kr = [
        w_in[:, off[i]:off[i + 1]] for i in range(14)]
    packed = jnp.concatenate([a_q, b_q, a_k, b_ks, b_kw, a_v, b_vs, b_vw, b_kc, b_vc,
                              c_qa, c_kv, c_kr, b_g], axis=1)
    packed = jnp.pad(packed, ((0, 0), (0, N_MIX - packed.shape[1])))
    return packed.T.astype(CDT), w_in[:, off[14]:].T.astype(CDT)


def kernel(x, mem, positions, norm_mix, w_in, swa_sinks, nsa_pe_k, nsa_pe_v, nsa_wk1, nsa_wk2,
           nsa_wv1, nsa_wv2, mla_q_norm, mla_w_q_b, mla_kv_norm, mla_w_kv_b, w_br_a, w_br_b,
           w_br_c, w_out, norm_xattn, norm_mem, w_xq, w_xkv, w_xo, norm_ffn, w_gate_up, w_down,
           norm_final):
    B, S, D = x.shape
    depth = w_in.shape[0]
    nj = S // NSA_SEL_BLOCK
    nc = 4 * nj
    assert S % TK_SEL == 0 and S % TM_PROJ == 0 and S >= NSA_WINDOW + TQ_NSA
    col = lambda v: v.reshape(-1, 1)
    wt = lambda w: w.T.astype(CDT)

    cos, sin = _rope_tables(positions, HEAD_DIM)
    cosm, sinm = _rope_tables(positions, MLA_ROPE_DIM)
    rr, jj = np.divmod(np.arange(nc), nj)
    cend = jnp.asarray(((4 * jj + rr) * NSA_CMP_STRIDE + NSA_CMP_BLOCK - 1).reshape(nc, 1), jnp.int32)

    res = x
    for l in range(depth):
        token_major = l == 0
        wmT, wgT = _pack_mixer_weight(w_in[l])
        o1, gates, kcv, kk, ks_aug, kcat = _proj(
            res, col(norm_mix[l]), wmT, cos, sin, cosm, sinm,
            col(mla_q_norm[l]), wt(mla_w_q_b[l]), col(mla_kv_norm[l]), wt(mla_w_kv_b[l]), TM_PROJ,
            token_major)
        kr = kcv.reshape(B, 4, nc, NSA_CMP_STRIDE * HEAD_DIM)
        gates = gates.reshape(B, NSA_KV_GROUPS, 3 * GROUP_HEADS, S)

        pe = jnp.stack([nsa_pe_k[l], nsa_pe_v[l]]).reshape(2, 2, NSA_CMP_STRIDE * HEAD_DIM)
        hpad = LANES - HEAD_DIM
        w1 = jnp.pad(jnp.stack([nsa_wk1[l], nsa_wv1[l]]), ((0, 0), (0, 0), (0, hpad))).astype(CDT)
        w2 = jnp.pad(jnp.stack([nsa_wk2[l], nsa_wv2[l]]), ((0, 0), (0, hpad), (0, 0))).astype(CDT)
        w2t = jnp.swapaxes(w2, 1, 2)
        cmp_tok, cmp_fm = _compress(kr, pe, w1, w2, w2t)

        o_a = _swa(swa_sinks[l], o1, kk, TQ_SWA)
        o_b = _nsa(o1, cend, cmp_tok, cmp_fm, ks_aug, kk, gates, TQ_NSA)
        o_c = _mla(o1, kcat, TQ_MLA)
        xT = _merge(res, col(norm_mix[l]), wgT, o_a, o_b, o_c,
                    wt(w_br_a[l]), wt(w_br_b[l]), wt(w_br_c[l]), wt(w_out[l]), TM_PROJ, token_major)

        n_kv = XATTN_HEADS * XATTN_HEAD_DIM
        kmem, vmemT = _mem_kv(mem, norm_mem[l].reshape(1, D), w_xkv[l][:, :n_kv].astype(CDT),
                              wt(w_xkv[l][:, n_kv:]))
        xT = _xattn(xT, col(norm_xattn[l]), wt(w_xq[l]), kmem, vmemT, wt(w_xo[l]), TM_PROJ)
        res = _ffn(xT, col(norm_ffn[l]), wt(w_gate_up[l]), wt(w_down[l]), TM_PROJ,
                   col(norm_final) if l == depth - 1 else None)
    return res
```

```python
import functools

import numpy as np
import jax
import jax.numpy as jnp
from jax import lax
from jax.experimental import pallas as pl
from jax.experimental.pallas import tpu as pltpu

HEAD_DIM = 64
ROPE_THETA = 10000.0
NORM_EPS = 1e-6
SWA_HEADS = 8
SWA_KV_HEADS = 2
SWA_WINDOW = 128
NSA_HEADS = 8
NSA_KV_GROUPS = 2
NSA_CMP_BLOCK = 32
NSA_CMP_STRIDE = 16
NSA_SEL_BLOCK = 64
NSA_N_SEL = 16
NSA_WINDOW = 512
NSA_FORCE_BONUS = 1e4
MLA_HEADS = 8
MLA_Q_RANK = 384
MLA_KV_RANK = 256
MLA_NOPE_DIM = 64
MLA_ROPE_DIM = 32
MLA_V_DIM = 64
MLA_QK_DIM = MLA_NOPE_DIM + MLA_ROPE_DIM
XATTN_HEADS = 4
XATTN_HEAD_DIM = 128
N_BRANCH = 3
GROUP_HEADS = 4
GROUP_ROWS = GROUP_HEADS * HEAD_DIM
LOG2E = 1.4426950408889634

V7X_VMEM_LIMIT_BYTES = 56 * 1024 * 1024
LANES = 128
ONES_ROWS = 16

CDT = jnp.bfloat16

MASK_VALUE = -1e30

TM_PROJ = 512
TQ_SWA = 256
TQ_NSA = 256
TK_SEL = 512
SEL_BLOCKS_PER_TILE = TK_SEL // NSA_SEL_BLOCK
SEL_CHAIN_LANES = 256
KSEL_DIM = HEAD_DIM + 16
TQ_MLA = 256
TK_MLA = 512
MLA_HEADS_PER_STEP = 8
QK_LEAD = 2
FF_CHUNK = 704

R_AQ, R_BQ, R_K, R_V, R_C, R_CQA, R_CKV, R_CKR = 0, 512, 1024, 1408, 1792, 2048, 2432, 2688
N_MIX = 2752
O_AQ, O_BQ, O_BQR, O_QM, O_VV, O_KV = 0, 512, 1024, 1536, 2304, 2688
N_OUT1 = 3712
N_GATE = NSA_HEADS * N_BRANCH


def _dot(a, b):
    return jnp.dot(a, b, preferred_element_type=jnp.float32)


def _dot_nt(a, b):
    return lax.dot_general(a, b, (((1,), (1,)), ((), ())), preferred_element_type=jnp.float32)


def _opts(name, *sem):
    return dict(name=name, compiler_params=pltpu.CompilerParams(
        dimension_semantics=sem, vmem_limit_bytes=V7X_VMEM_LIMIT_BYTES))


def _rms_fm(x, g_col):
    ms = jnp.mean(x * x, axis=0, keepdims=True)
    return x * lax.rsqrt(ms + NORM_EPS) * g_col


def _const_spec(shape):
    nd = len(shape)
    return pl.BlockSpec(shape, lambda *_: (0,) * nd)


def _residual_spec(d, tm, token_major):
    if token_major:
        return pl.BlockSpec((1, tm, d), lambda b, i: (b, i, 0))
    return pl.BlockSpec((1, d, tm), lambda b, i: (b, 0, i))


def _load_residual(x_ref, token_major):
    return x_ref[0].T if token_major else x_ref[0]


def _rope_store(o_ref, row0, y, cos, sin, n_heads, head_dim, scale):
    half = head_dim // 2
    for h in range(n_heads):
        x1 = y[h * head_dim:h * head_dim + half]
        x2 = y[h * head_dim + half:(h + 1) * head_dim]
        r = row0 + h * head_dim
        o_ref[0, r:r + half, :] = ((x1 * cos - x2 * sin) * scale).astype(o_ref.dtype)
        o_ref[0, r + half:r + head_dim, :] = ((x2 * cos + x1 * sin) * scale).astype(o_ref.dtype)


def _proj_kernel(x_ref, g_ref, wm_ref, cos_ref, sin_ref, cosm_ref, sinm_ref,
                 qn_ref, wqb_ref, kvn_ref, wkvb_ref, o1_ref, gate_ref, kcv_ref, kk_ref, ks_ref, kc_ref,
                 *, token_major):
    h = _rms_fm(_load_residual(x_ref, token_major), g_ref[...]).astype(CDT)
    cos, sin = cos_ref[0], sin_ref[0]
    cosm, sinm = cosm_ref[0], sinm_ref[0]
    qk_scale = HEAD_DIM ** -0.5 * LOG2E

    lat_q = _rms_fm(_dot(wm_ref[R_CQA:R_CQA + MLA_Q_RANK, :], h), qn_ref[...]).astype(CDT)
    lat_kv = _rms_fm(_dot(wm_ref[R_CKV:R_CKV + MLA_KV_RANK, :], h), kvn_ref[...]).astype(CDT)

    y = _dot(wm_ref[R_AQ:R_AQ + 512, :], h)
    _rope_store(o1_ref, O_AQ, y, cos, sin, SWA_HEADS, HEAD_DIM, qk_scale)
    y = _dot(wm_ref[R_BQ:R_BQ + 512, :], h)
    o1_ref[0, O_BQ:O_BQ + 512, :] = (y * qk_scale).astype(o1_ref.dtype)
    _rope_store(o1_ref, O_BQR, y, cos, sin, NSA_HEADS, HEAD_DIM, qk_scale)
    y = _dot(wm_ref[R_K:R_K + 384, :], h)
    tm = y.shape[1]
    tok = pl.program_id(1) * tm + lax.broadcasted_iota(jnp.int32, (KSEL_DIM - HEAD_DIM, tm), 1)
    blk = lax.shift_right_arithmetic(tok, int(np.log2(NSA_SEL_BLOCK))) & (SEL_BLOCKS_PER_TILE - 1)
    row = lax.broadcasted_iota(jnp.int32, (KSEL_DIM - HEAD_DIM, tm), 0)
    onehot = jnp.where(blk == row, 1.0, 0.0)
    half = HEAD_DIM // 2
    for kh in range(6):
        x1 = y[kh * HEAD_DIM:kh * HEAD_DIM + half]
        x2 = y[kh * HEAD_DIM + half:(kh + 1) * HEAD_DIM]
        rows = [x1 * cos - x2 * sin, x2 * cos + x1 * sin]
        if kh in (2, 3):
            ks_ref[0, kh - 2] = jnp.concatenate(rows + [onehot], axis=0).T.astype(ks_ref.dtype)
        else:
            kk_ref[0, kh if kh < 2 else kh - 2] = jnp.concatenate(rows, axis=0).T.astype(kk_ref.dtype)
    y = _dot(wm_ref[R_V:R_V + 384, :], h)
    o1_ref[0, O_VV:O_VV + 384, :] = y.astype(o1_ref.dtype)
    y = _dot(wm_ref[R_C:R_C + 256, :], h)
    for n in range(4):
        kcv_ref[0, n] = y[n * HEAD_DIM:(n + 1) * HEAD_DIM].T

    y = _dot(wm_ref[R_CKR:R_CKR + 64, :], h)
    hr = MLA_ROPE_DIM // 2
    x1, x2 = y[0:hr], y[hr:2 * hr]
    k_pe = [x1 * cosm - x2 * sinm, x2 * cosm + x1 * sinm]
    gate_ref[0] = jax.nn.sigmoid(y[MLA_ROPE_DIM:MLA_ROPE_DIM + N_GATE])

    qm = _dot(wqb_ref[...], lat_q)
    m_scale = MLA_QK_DIM ** -0.5 * LOG2E
    for hd in range(MLA_HEADS):
        r = hd * MLA_QK_DIM
        o1_ref[0, O_QM + r:O_QM + r + MLA_NOPE_DIM, :] = (
            qm[r:r + MLA_NOPE_DIM] * m_scale).astype(o1_ref.dtype)
        x1 = qm[r + MLA_NOPE_DIM:r + MLA_NOPE_DIM + hr]
        x2 = qm[r + MLA_NOPE_DIM + hr:r + MLA_QK_DIM]
        o1_ref[0, O_QM + r + MLA_NOPE_DIM:O_QM + r + MLA_NOPE_DIM + hr, :] = (
            (x1 * cosm - x2 * sinm) * m_scale).astype(o1_ref.dtype)
        o1_ref[0, O_QM + r + MLA_NOPE_DIM + hr:O_QM + r + MLA_QK_DIM, :] = (
            (x2 * cosm + x1 * sinm) * m_scale).astype(o1_ref.dtype)
    kv = _dot(wkvb_ref[...], lat_kv)
    o1_ref[0, O_KV:O_KV + 1024, :] = kv.astype(o1_ref.dtype)
    for hd in range(MLA_HEADS):
        k_nope = kv[hd * 2 * MLA_NOPE_DIM:hd * 2 * MLA_NOPE_DIM + MLA_NOPE_DIM]
        kc_ref[0, hd] = jnp.concatenate([k_nope] + k_pe, axis=0).T.astype(kc_ref.dtype)


def _proj(x, g_col, wmT, cos, sin, cosm, sinm, qn, wqbT, kvn, wkvbT, tm, token_major):
    B, D, S = (x.shape[0], x.shape[2], x.shape[1]) if token_major else x.shape
    tok = lambda rows: pl.BlockSpec((1, rows, tm), lambda b, i: (b, 0, i))
    return pl.pallas_call(
        functools.partial(_proj_kernel, token_major=token_major),
        grid=(B, S // tm),
        in_specs=[_residual_spec(D, tm, token_major), _const_spec((D, 1)), _const_spec(wmT.shape),
                  tok(32), tok(32), tok(16), tok(16),
                  _const_spec(qn.shape), _const_spec(wqbT.shape),
                  _const_spec(kvn.shape), _const_spec(wkvbT.shape)],
        out_specs=[tok(N_OUT1), tok(N_GATE),
                   pl.BlockSpec((1, 4, tm, HEAD_DIM), lambda b, i: (b, 0, i, 0)),
                   pl.BlockSpec((1, 4, tm, HEAD_DIM), lambda b, i: (b, 0, i, 0)),
                   pl.BlockSpec((1, NSA_KV_GROUPS, tm, KSEL_DIM), lambda b, i: (b, 0, i, 0)),
                   pl.BlockSpec((1, MLA_HEADS, tm, MLA_QK_DIM), lambda b, i: (b, 0, i, 0))],
        out_shape=[jax.ShapeDtypeStruct((B, N_OUT1, S), CDT),
                   jax.ShapeDtypeStruct((B, N_GATE, S), jnp.float32),
                   jax.ShapeDtypeStruct((B, 4, S, HEAD_DIM), jnp.float32),
                   jax.ShapeDtypeStruct((B, 4, S, HEAD_DIM), CDT),
                   jax.ShapeDtypeStruct((B, NSA_KV_GROUPS, S, KSEL_DIM), CDT),
                   jax.ShapeDtypeStruct((B, MLA_HEADS, S, MLA_QK_DIM), CDT)],
        **_opts("mixer_proj", "parallel", "parallel"),
    )(x, g_col, wmT, cos, sin, cosm, sinm, qn, wqbT, kvn, wkvbT)


def _compress_kernel(kr_ref, pe_ref, w1_ref, w2_ref, w2t_ref, tok_ref, fm_ref, hid_ref):
    kr = kr_ref[0, 0]
    nc, half = kr.shape
    nj = nc // 4
    a = _dot((kr + pe_ref[0, 0:1, :]).astype(CDT), w1_ref[0, 0:half, :])
    bm = _dot((kr + pe_ref[0, 1:2, :]).astype(CDT), w1_ref[0, half:2 * half, :])
    nxt = pltpu.roll(bm, nc - 1, 0)
    row = lax.broadcasted_iota(jnp.int32, (nc, 1), 0)
    pre = a + jnp.where(row == nc - 1, 0.0, nxt)
    hid_ref[...] = pre * jax.nn.sigmoid(pre)
    hid = jnp.concatenate([hid_ref[pl.ds(r, nj, stride=4), :] for r in range(4)], axis=0).astype(CDT)
    tok_ref[0, 0] = _dot(hid, w2_ref[0]).astype(tok_ref.dtype)
    fm_ref[0, 0] = _dot_nt(w2t_ref[0], hid).astype(fm_ref.dtype)


def _compress(kr, pe, w1, w2, w2t):
    B, _, NC, W = kr.shape
    return pl.pallas_call(
        _compress_kernel,
        grid=(B, 4),
        in_specs=[pl.BlockSpec((1, 1, NC, W), lambda b, n: (b, n, 0, 0)),
                  pl.BlockSpec((1, 2, W), lambda b, n: (n // 2, 0, 0)),
                  pl.BlockSpec((1, 2 * W, LANES), lambda b, n: (n // 2, 0, 0)),
                  pl.BlockSpec((1, LANES, HEAD_DIM), lambda b, n: (n // 2, 0, 0)),
                  pl.BlockSpec((1, HEAD_DIM, LANES), lambda b, n: (n // 2, 0, 0))],
        out_specs=[pl.BlockSpec((1, 1, NC, HEAD_DIM), lambda b, n: (b, n, 0, 0)),
                   pl.BlockSpec((1, 1, HEAD_DIM, NC), lambda b, n: (b, n, 0, 0))],
        out_shape=[jax.ShapeDtypeStruct((B, 4, NC, HEAD_DIM), CDT),
                   jax.ShapeDtypeStruct((B, 4, HEAD_DIM, NC), CDT)],
        scratch_shapes=[pltpu.VMEM((NC, LANES), jnp.float32)],
        **_opts("nsa_compress", "parallel", "parallel"),
    )(kr, pe, w1, w2, w2t)


def _group_queries(q_ref, g):
    return jnp.concatenate([q_ref[0, (g * GROUP_HEADS + r) * HEAD_DIM:(g * GROUP_HEADS + r + 1) * HEAD_DIM, :]
                            for r in range(GROUP_HEADS)], axis=1)


def _with_ones(vt):
    return jnp.concatenate([vt, jnp.ones((ONES_ROWS, vt.shape[1]), vt.dtype)], axis=0)


def _softmax_probs(sc, m_run):
    m_new = jnp.maximum(m_run, jnp.max(sc, axis=0, keepdims=True))
    return m_new, jnp.exp2(sc - m_new).astype(CDT), jnp.exp2(m_run - m_new)


def _softmax_step(sc, m_run, acc, vt_aug):
    m_new, p, alpha = _softmax_probs(sc, m_run)
    return m_new, alpha * acc + _dot(vt_aug, p)


def _window_scores(q4, t4, k, k0, window):
    span = k.shape[0]
    s = _dot(k, q4)
    kpos = k0 + lax.broadcasted_iota(jnp.int32, (span, 1), 0)
    return jnp.where(kpos <= t4, jnp.where(kpos > t4 - window, s, -jnp.inf), -jnp.inf)


def _window_output(s, vt, sink4=None):
    m = jnp.max(s, axis=0, keepdims=True)
    if sink4 is not None:
        m = jnp.maximum(m, sink4)
    pv = _dot(_with_ones(vt), jnp.exp2(s - m).astype(CDT))
    den = pv[HEAD_DIM:HEAD_DIM + 1]
    if sink4 is not None:
        den = den + jnp.exp2(sink4 - m)
    return pv[0:HEAD_DIM] * (1.0 / den)


def _swa_kernel(sink_ref, q_ref, k_ref, vt_ref, o_ref, *, tq):
    qs = pl.program_id(1) * tq
    t = qs + lax.broadcasted_iota(jnp.int32, (1, tq), 1)
    t4 = jnp.concatenate([t] * GROUP_HEADS, axis=1)
    span = SWA_WINDOW + tq
    k0 = pl.multiple_of(jnp.maximum(qs - SWA_WINDOW, 0), LANES)
    scores = [_window_scores(_group_queries(q_ref, g), t4, k_ref[0, g, pl.ds(k0, span), :],
                             k0, SWA_WINDOW) for g in range(SWA_KV_HEADS)]
    for g in range(SWA_KV_HEADS):
        sink4 = jnp.concatenate([jnp.full((1, tq), sink_ref[g * GROUP_HEADS + r] * LOG2E, jnp.float32)
                                 for r in range(GROUP_HEADS)], axis=1)
        o4 = _window_output(scores[g], vt_ref[0, g * HEAD_DIM:(g + 1) * HEAD_DIM, pl.ds(k0, span)], sink4)
        for r in range(GROUP_HEADS):
            row = (g * GROUP_HEADS + r) * HEAD_DIM
            o_ref[0, row:row + HEAD_DIM, :] = o4[:, r * tq:(r + 1) * tq].astype(o_ref.dtype)


def _swa(sinks, o1, kk, tq):
    B, _, S = o1.shape
    rows = SWA_HEADS * HEAD_DIM
    vrows = SWA_KV_HEADS * HEAD_DIM
    assert O_AQ % rows == 0 and O_VV % vrows == 0
    return pl.pallas_call(
        functools.partial(_swa_kernel, tq=tq),
        grid=(B, S // tq),
        in_specs=[pl.BlockSpec(memory_space=pltpu.SMEM),
                  pl.BlockSpec((1, rows, tq), lambda b, i: (b, O_AQ // rows, i)),
                  pl.BlockSpec((1, SWA_KV_HEADS, S, HEAD_DIM), lambda b, i: (b, 0, 0, 0)),
                  pl.BlockSpec((1, vrows, S), lambda b, i: (b, O_VV // vrows, 0))],
        out_specs=pl.BlockSpec((1, rows, tq), lambda b, i: (b, 0, i)),
        out_shape=jax.ShapeDtypeStruct((B, rows, S), CDT),
        **_opts("swa_attn", "parallel", "parallel"),
    )(sinks, o1, kk, o1)


def _nsa_kernel(q_ref, qr_ref, cend_ref, kc_ref, vct_ref, ks_ref, vst_ref, kw_ref, vwt_ref,
                gate_ref, o_ref, bias_ref, s_a, s_b, *, tq, n_sel):
    qs = pl.program_id(1) * tq
    nc = kc_ref.shape[2]
    nj = nc // 4
    G = NSA_KV_GROUPS
    t = qs + lax.broadcasted_iota(jnp.int32, (1, tq), 1)
    hpc = SEL_CHAIN_LANES // tq
    cpg = GROUP_HEADS // hpc
    tc = jnp.concatenate([t] * hpc, axis=1)
    t4 = jnp.concatenate([t] * GROUP_HEADS, axis=1)
    jrow = lax.broadcasted_iota(jnp.int32, (nj, tq), 0)
    jf = jrow.astype(jnp.float32)
    cur = lax.shift_right_arithmetic(t, int(np.log2(NSA_SEL_BLOCK)))
    bonus = jnp.where((jrow == 0) | (jrow == cur) | (jrow == cur - 1), NSA_FORCE_BONUS, 0.0)
    span = NSA_WINDOW + tq
    kw0 = pl.multiple_of(jnp.maximum(qs - NSA_WINDOW, 0), LANES)

    qr4 = [_group_queries(qr_ref, g) for g in range(G)]
    s_cmp = [_dot(kc_ref[0, g], _group_queries(q_ref, g)) for g in range(G)]
    s_win = [_window_scores(qr4[g], t4, kw_ref[0, g, pl.ds(kw0, span), :], kw0, NSA_WINDOW)
             for g in range(G)]
    p_cmp = []
    for g in range(G):
        s = jnp.where(cend_ref[...] <= t4, s_cmp[g], -jnp.inf)
        m = jnp.max(s, axis=0, keepdims=True)
        m = jnp.where(m == -jnp.inf, 0.0, m)
        e = jnp.exp2(s - m)
        den = jnp.sum(e, axis=0, keepdims=True)
        p_cmp.append(e * (1.0 / jnp.where(den > 0, den, 1.0)))
    o_cmps = [_dot(vct_ref[0, g], p_cmp[g].astype(CDT)) for g in range(G)]
    o_wins = [_window_output(s_win[g], vwt_ref[0, g * HEAD_DIM:(g + 1) * HEAD_DIM, pl.ds(kw0, span)])
              for g in range(G)]

    fixed = []
    for g in range(G):
        p, o_cmp, o_win = p_cmp[g], o_cmps[g], o_wins[g]
        psum = p[:, 0:tq]
        for r in range(1, GROUP_HEADS):
            psum = psum + p[:, r * tq:(r + 1) * tq]
        p3 = psum[3 * nj:4 * nj]
        prev = jnp.where(jrow == 0, 0.0, pltpu.roll(p3, 1, 0))
        imp = prev + 2.0 * (psum[0:nj] + psum[nj:2 * nj] + psum[2 * nj:3 * nj]) + p3

        score = jnp.where(jrow <= cur, imp + bonus, -jnp.inf)
        for _ in range(n_sel):
            mx = jnp.max(score, axis=0, keepdims=True)
            idx = jnp.min(jnp.where(score == mx, jf, float(nj)), axis=0, keepdims=True)
            idx = jnp.where(mx > -jnp.inf, idx, float(nj))
            score = jnp.where(jf == idx, -jnp.inf, score)
        bias_ref[g] = jnp.where(jrow <= cur, jnp.where(score == -jnp.inf, 0.0, MASK_VALUE), MASK_VALUE)

        parts = []
        for r in range(GROUP_HEADS):
            sl = slice(r * tq, (r + 1) * tq)
            g0 = gate_ref[0, g, 3 * r:3 * r + 1, :]
            g2 = gate_ref[0, g, 3 * r + 2:3 * r + 3, :]
            parts.append((g0 * o_cmp[:, sl], g2 * o_win[:, sl]))
        fixed.append(parts)

    def augmented_queries(kt):
        out = []
        for g in range(G):
            b8 = bias_ref[g, pl.ds(kt * SEL_BLOCKS_PER_TILE, SEL_BLOCKS_PER_TILE), :]
            b16 = jnp.concatenate([b8, jnp.zeros_like(b8)], axis=0)
            out.append(jnp.concatenate(
                [qr4[g], jnp.concatenate([b16] * GROUP_HEADS, axis=1).astype(CDT)], axis=0))
        return out

    def score_chain(kt, s_ref, qa, c):
        g, hp = divmod(c, cpg)
        k0 = pl.multiple_of(kt * TK_SEL, TK_SEL)
        s_ref[c] = _dot(ks_ref[0, g, pl.ds(k0, TK_SEL), :],
                        qa[g][:, hp * SEL_CHAIN_LANES:(hp + 1) * SEL_CHAIN_LANES])

    def consume_chain(kt, s_ref, c, st, diagonal):
        g = c // cpg
        k0 = pl.multiple_of(kt * TK_SEL, TK_SEL)
        sc = s_ref[c]
        if diagonal:
            kpos = k0 + lax.broadcasted_iota(jnp.int32, (TK_SEL, 1), 0)
            sc = jnp.where(kpos <= tc, sc, MASK_VALUE)
        vt_aug = _with_ones(vst_ref[0, g * HEAD_DIM:(g + 1) * HEAD_DIM, pl.ds(k0, TK_SEL)])
        return _softmax_step(sc, st[0], st[1], vt_aug)

    def step(kt, s_cur, s_nxt, state, diagonal=False):
        n_chain = cpg * G
        if s_nxt is not None:
            qa = augmented_queries(kt + 1)
            for c in range(QK_LEAD):
                score_chain(kt + 1, s_nxt, qa, c)
        out = []
        for c in range(n_chain):
            out.append(consume_chain(kt, s_cur, c, state[c], diagonal))
            if s_nxt is not None and c + QK_LEAD < n_chain:
                score_chain(kt + 1, s_nxt, qa, c + QK_LEAD)
        return tuple(out)

    def finish(state):
        for g in range(G):
            for r in range(GROUP_HEADS):
                _, acc = state[cpg * g + r // hpc]
                sl = slice((r % hpc) * tq, (r % hpc + 1) * tq)
                o_sel = acc[0:HEAD_DIM, sl] * (1.0 / acc[HEAD_DIM:HEAD_DIM + 1, sl])
                g1 = gate_ref[0, g, 3 * r + 1:3 * r + 2, :]
                c_part, w_part = fixed[g][r]
                row = (g * GROUP_HEADS + r) * HEAD_DIM
                o_ref[0, row:row + HEAD_DIM, :] = (c_part + g1 * o_sel + w_part).astype(o_ref.dtype)

    def pair(j, state):
        return step(2 * j + 1, s_b, s_a, step(2 * j, s_a, s_b, state))

    init = tuple((jnp.full((1, SEL_CHAIN_LANES), MASK_VALUE, jnp.float32),
                  jnp.zeros((HEAD_DIM + ONES_ROWS, SEL_CHAIN_LANES), jnp.float32))
                 for _ in range(cpg * G))
    n_full = qs // TK_SEL
    qa0 = augmented_queries(0)
    for c in range(cpg * G):
        score_chain(0, s_a, qa0, c)
    state = lax.fori_loop(0, n_full // 2, pair, init)
    odd = n_full % 2 == 1

    @pl.when(odd)
    def _():
        finish(step(n_full, s_b, None, step(n_full - 1, s_a, s_b, state), diagonal=True))

    @pl.when(jnp.logical_not(odd))
    def _():
        finish(step(n_full, s_a, None, state, diagonal=True))


def _nsa(o1, cend, cmp_tok, cmp_fm, ks_aug, kk, gates, tq):
    B, _, S = o1.shape
    nc = cmp_tok.shape[2]
    nj = nc // 4
    n_sel = min(NSA_N_SEL, nj)
    G = NSA_KV_GROUPS
    rows = NSA_HEADS * HEAD_DIM
    vrows = G * HEAD_DIM
    assert O_BQ % rows == 0 and O_BQR % rows == 0 and O_VV % vrows == 0
    assert SEL_CHAIN_LANES % tq == 0 and tq <= TK_SEL
    return pl.pallas_call(
        functools.partial(_nsa_kernel, tq=tq, n_sel=n_sel),
        grid=(B, S // tq),
        in_specs=[pl.BlockSpec((1, rows, tq), lambda b, i: (b, O_BQ // rows, i)),
                  pl.BlockSpec((1, rows, tq), lambda b, i: (b, O_BQR // rows, i)),
                  _const_spec((nc, 1)),
                  pl.BlockSpec((1, G, nc, HEAD_DIM), lambda b, i: (b, 0, 0, 0)),
                  pl.BlockSpec((1, G, HEAD_DIM, nc), lambda b, i: (b, 1, 0, 0)),
                  pl.BlockSpec((1, G, S, KSEL_DIM), lambda b, i: (b, 0, 0, 0)),
                  pl.BlockSpec((1, vrows, S), lambda b, i: (b, O_VV // vrows + 1, 0)),
                  pl.BlockSpec((1, G, S, HEAD_DIM), lambda b, i: (b, 1, 0, 0)),
                  pl.BlockSpec((1, vrows, S), lambda b, i: (b, O_VV // vrows + 2, 0)),
                  pl.BlockSpec((1, G, 3 * GROUP_HEADS, tq), lambda b, i: (b, 0, 0, i))],
        out_specs=pl.BlockSpec((1, rows, tq), lambda b, i: (b, 0, i)),
        out_shape=jax.ShapeDtypeStruct((B, rows, S), CDT),
        scratch_shapes=[pltpu.VMEM((G, nj, tq), jnp.float32)]
                       + [pltpu.VMEM((NSA_HEADS * tq // SEL_CHAIN_LANES, TK_SEL, SEL_CHAIN_LANES),
                                     jnp.float32)] * 2,
        **_opts("nsa_attn", "parallel", "arbitrary"),
    )(o1, o1, cend, cmp_tok, cmp_fm, ks_aug, o1, kk, o1, gates)


def _mla_kernel(q_ref, k_ref, *rest, tq, hps):
    vt_refs, o_ref, s_a, s_b = rest[:hps], rest[hps], rest[hps + 1], rest[hps + 2]
    qs = pl.program_id(2) * tq
    q = [q_ref[0, h * MLA_QK_DIM:(h + 1) * MLA_QK_DIM, :] for h in range(hps)]
    t = qs + lax.broadcasted_iota(jnp.int32, (1, tq), 1)

    def score_chain(kt, s_ref, h):
        k0 = pl.multiple_of(kt * TK_MLA, TK_MLA)
        s_ref[h] = _dot(k_ref[0, h, pl.ds(k0, TK_MLA), :], q[h])

    def consume_chain(kt, s_ref, h, st, diagonal):
        k0 = pl.multiple_of(kt * TK_MLA, TK_MLA)
        sc = s_ref[h]
        if diagonal:
            kpos = k0 + lax.broadcasted_iota(jnp.int32, (TK_MLA, 1), 0)
            sc = jnp.where(kpos <= t, sc, MASK_VALUE)
        return _softmax_step(sc, st[0], st[1], _with_ones(vt_refs[h][0, :, pl.ds(k0, TK_MLA)]))

    def step(kt, s_cur, s_nxt, state, diagonal=False):
        lead = QK_LEAD
        if s_nxt is not None:
            for h in range(lead):
                score_chain(kt + 1, s_nxt, h)
        out = []
        for h in range(hps):
            out.append(consume_chain(kt, s_cur, h, state[h], diagonal))
            if s_nxt is not None and h + lead < hps:
                score_chain(kt + 1, s_nxt, h + lead)
        return tuple(out)

    def finish(state):
        for h in range(hps):
            _, acc = state[h]
            o_ref[0, h * MLA_V_DIM:(h + 1) * MLA_V_DIM, :] = (
                acc[0:MLA_V_DIM] * (1.0 / acc[MLA_V_DIM:MLA_V_DIM + 1])).astype(o_ref.dtype)

    def pair(j, state):
        return step(2 * j + 1, s_b, s_a, step(2 * j, s_a, s_b, state))

    init = tuple((jnp.full((1, tq), MASK_VALUE, jnp.float32),
                  jnp.zeros((MLA_V_DIM + ONES_ROWS, tq), jnp.float32)) for _ in range(hps))
    n_full = qs // TK_MLA
    for h in range(hps):
        score_chain(0, s_a, h)
    state = lax.fori_loop(0, n_full // 2, pair, init)
    odd = n_full % 2 == 1

    @pl.when(odd)
    def _():
        finish(step(n_full, s_b, None, step(n_full - 1, s_a, s_b, state), diagonal=True))

    @pl.when(jnp.logical_not(odd))
    def _():
        finish(step(n_full, s_a, None, state, diagonal=True))


def _mla(o1, kcat, tq):
    B, _, S = o1.shape
    hps = MLA_HEADS_PER_STEP
    assert O_QM % (hps * MLA_QK_DIM) == 0 and O_KV % MLA_V_DIM == 0 and MLA_HEADS % hps == 0
    v_spec = lambda h: pl.BlockSpec(
        (1, MLA_V_DIM, S), lambda b, hg, i: (b, O_KV // MLA_V_DIM + 2 * (hg * hps + h) + 1, 0),
        pipeline_mode=pl.Buffered(1))
    return pl.pallas_call(
        functools.partial(_mla_kernel, tq=tq, hps=hps),
        grid=(B, MLA_HEADS // hps, S // tq),
        in_specs=[pl.BlockSpec((1, hps * MLA_QK_DIM, tq),
                               lambda b, hg, i: (b, O_QM // (hps * MLA_QK_DIM) + hg, i)),
                  pl.BlockSpec((1, hps, S, MLA_QK_DIM), lambda b, hg, i: (b, hg, 0, 0),
                               pipeline_mode=pl.Buffered(1))]
                 + [v_spec(h) for h in range(hps)],
        out_specs=pl.BlockSpec((1, hps * MLA_V_DIM, tq), lambda b, hg, i: (b, hg, i)),
        out_shape=jax.ShapeDtypeStruct((B, MLA_HEADS * MLA_V_DIM, S), CDT),
        scratch_shapes=[pltpu.VMEM((hps, TK_MLA, tq), jnp.float32)] * 2,
        **_opts("mla_attn", "parallel", "parallel", "parallel"),
    )(o1, kcat, *([o1] * hps))


def _merge_kernel(x_ref, g_ref, wg_ref, oa_ref, ob_ref, oc_ref, wa_ref, wb_ref, wc_ref, wo_ref, o_ref,
                  *, token_major):
    x = _load_residual(x_ref, token_major)
    d = x.shape[0]
    h = _rms_fm(x, g_ref[...]).astype(CDT)
    merged = None
    for i, (br_ref, w_ref) in enumerate(((oa_ref, wa_ref), (ob_ref, wb_ref), (oc_ref, wc_ref))):
        gate = jax.nn.sigmoid(_dot(wg_ref[i * d:(i + 1) * d, :], h))
        term = gate * _dot(w_ref[...], br_ref[0])
        merged = term if merged is None else merged + term
    o_ref[0] = x + _dot(wo_ref[...], merged.astype(CDT))


def _merge(x, g_col, wgT, oa, ob, oc, waT, wbT, wcT, woT, tm, token_major):
    B, D, S = (x.shape[0], x.shape[2], x.shape[1]) if token_major else x.shape
    tok = lambda rows: pl.BlockSpec((1, rows, tm), lambda b, i: (b, 0, i))
    return pl.pallas_call(
        functools.partial(_merge_kernel, token_major=token_major),
        grid=(B, S // tm),
        in_specs=[_residual_spec(D, tm, token_major), _const_spec((D, 1)), _const_spec(wgT.shape),
                  tok(oa.shape[1]), tok(ob.shape[1]), tok(oc.shape[1]),
                  _const_spec(waT.shape), _const_spec(wbT.shape), _const_spec(wcT.shape),
                  _const_spec(woT.shape)],
        out_specs=tok(D),
        out_shape=jax.ShapeDtypeStruct((B, D, S), x.dtype),
        **_opts("merge_out", "parallel", "parallel"),
    )(x, g_col, wgT, oa, ob, oc, waT, wbT, wcT, woT)


def _mem_kv_kernel(mem_ref, g_ref, wk_ref, wvt_ref, k_ref, vt_ref):
    m = mem_ref[0]
    ms = jnp.mean(m * m, axis=-1, keepdims=True)
    hm = (m * lax.rsqrt(ms + NORM_EPS) * g_ref[...]).astype(CDT)
    k_ref[0] = _dot(hm, wk_ref[...]).astype(k_ref.dtype)
    vt_ref[0] = _dot_nt(wvt_ref[...], hm).astype(vt_ref.dtype)


def _mem_kv(mem, g_row, wk, wvT):
    B, M, D = mem.shape
    n = wk.shape[1]
    return pl.pallas_call(
        _mem_kv_kernel,
        grid=(B,),
        in_specs=[pl.BlockSpec((1, M, D), lambda b: (b, 0, 0)), _const_spec((1, D)),
                  _const_spec(wk.shape), _const_spec(wvT.shape)],
        out_specs=[pl.BlockSpec((1, M, n), lambda b: (b, 0, 0)),
                   pl.BlockSpec((1, n, M), lambda b: (b, 0, 0))],
        out_shape=[jax.ShapeDtypeStruct((B, M, n), CDT), jax.ShapeDtypeStruct((B, n, M), CDT)],
        **_opts("mem_kv", "parallel"),
    )(mem, g_row, wk, wvT)


def _xattn_kernel(x_ref, g_ref, wq_ref, k_ref, vt_ref, wo_ref, o_ref):
    x = x_ref[0]
    h = _rms_fm(x, g_ref[...]).astype(CDT)
    q = (_dot(wq_ref[...], h) * (XATTN_HEAD_DIM ** -0.5 * LOG2E)).astype(CDT)
    head_rows = [slice(hd * XATTN_HEAD_DIM, (hd + 1) * XATTN_HEAD_DIM) for hd in range(XATTN_HEADS)]
    scores = [_dot(k_ref[0, :, rows], q[rows]) for rows in head_rows]
    probs = [jnp.exp2(s - jnp.max(s, axis=0, keepdims=True)).astype(CDT) for s in scores]
    outs = []
    for rows, e in zip(head_rows, probs):
        pv = _dot(_with_ones(vt_ref[0, rows, :]), e)
        outs.append((pv[0:XATTN_HEAD_DIM] * (1.0 / pv[XATTN_HEAD_DIM:XATTN_HEAD_DIM + 1])).astype(CDT))
    o_ref[0] = x + _dot(wo_ref[...], jnp.concatenate(outs, axis=0))


def _xattn(xT, g_col, wqT, kmem, vmemT, woT, tm):
    B, D, S = xT.shape
    M, n = kmem.shape[1], kmem.shape[2]
    tok = lambda rows: pl.BlockSpec((1, rows, tm), lambda b, i: (b, 0, i))
    return pl.pallas_call(
        _xattn_kernel,
        grid=(B, S // tm),
        in_specs=[tok(D), _const_spec((D, 1)), _const_spec(wqT.shape),
                  pl.BlockSpec((1, M, n), lambda b, i: (b, 0, 0)),
                  pl.BlockSpec((1, n, M), lambda b, i: (b, 0, 0)),
                  _const_spec(woT.shape)],
        out_specs=tok(D),
        out_shape=jax.ShapeDtypeStruct((B, D, S), xT.dtype),
        **_opts("xattn", "parallel", "parallel"),
    )(xT, g_col, wqT, kmem, vmemT, woT)


def _ffn_kernel(x_ref, g_ref, wgu_ref, wd_ref, *rest, d_ff, final_norm):
    o_ref = rest[-1]
    x = x_ref[0]
    h = _rms_fm(x, g_ref[...]).astype(CDT)
    acc = x
    for c in range(d_ff // FF_CHUNK):
        r = c * FF_CHUNK
        gate = _dot(wgu_ref[r:r + FF_CHUNK, :], h)
        up = _dot(wgu_ref[d_ff + r:d_ff + r + FF_CHUNK, :], h)
        act = (gate * jax.nn.sigmoid(gate) * up).astype(CDT)
        acc = acc + _dot(wd_ref[:, r:r + FF_CHUNK], act)
    if final_norm:
        o_ref[0] = _rms_fm(acc, rest[0][...]).T
    else:
        o_ref[0] = acc


def _ffn(xT, g_col, wguT, wdT, tm, final_g_col=None):
    B, D, S = xT.shape
    d_ff = wdT.shape[1]
    final_norm = final_g_col is not None
    tok = lambda rows: pl.BlockSpec((1, rows, tm), lambda b, i: (b, 0, i))
    extra = [final_g_col] if final_norm else []
    return pl.pallas_call(
        functools.partial(_ffn_kernel, d_ff=d_ff, final_norm=final_norm),
        grid=(B, S // tm),
        in_specs=[tok(D), _const_spec((D, 1)), _const_spec(wguT.shape), _const_spec(wdT.shape)]
                 + [_const_spec((D, 1))] * len(extra),
        out_specs=_residual_spec(D, tm, final_norm),
        out_shape=jax.ShapeDtypeStruct((B, S, D) if final_norm else (B, D, S), xT.dtype),
        **_opts("ffn", "parallel", "parallel"),
    )(xT, g_col, wguT, wdT, *extra)


def _rope_tables(positions, dim):
    half = dim // 2
    inv_freq = ROPE_THETA ** (-jnp.arange(half, dtype=jnp.float32) / half)
    ang = positions.astype(jnp.float32)[:, None, :] * inv_freq[None, :, None]
    return jnp.cos(ang), jnp.sin(ang)


def _pack_mixer_weight(w_in):
    off = np.cumsum((0, 512, 128, 128, 512, 128, 128, 128, 128, 128, 128, 24, 384, 256, 32))
    a_q, a_k, a_v, b_q, b_kc, b_vc, b_ks, b_vs, b_kw, b_vw, b_g, c_qa, c_kv, c_kr = [
        w_in[:, off[i]:off[i + 1]] for i in range(14)]
    packed = jnp.concatenate([a_q, b_q, a_k, b_ks, b_kw, a_v, b_vs, b_vw, b_kc, b_vc,
                              c_qa, c_kv, c_kr, b_g], axis=1)
    packed = jnp.pad(packed, ((0, 0), (0, N_MIX - packed.shape[1])))
    return packed.T.astype(CDT), w_in[:, off[14]:].T.astype(CDT)


def kernel(x, mem, positions, norm_mix, w_in, swa_sinks, nsa_pe_k, nsa_pe_v, nsa_wk1, nsa_wk2,
           nsa_wv1, nsa_wv2, mla_q_norm, mla_w_q_b, mla_kv_norm, mla_w_kv_b, w_br_a, w_br_b,
           w_br_c, w_out, norm_xattn, norm_mem, w_xq, w_xkv, w_xo, norm_ffn, w_gate_up, w_down,
           norm_final):
    B, S, D = x.shape
    depth = w_in.shape[0]
    nj = S // NSA_SEL_BLOCK
    nc = 4 * nj
    assert S % TK_SEL == 0 and S % TM_PROJ == 0 and S >= NSA_WINDOW + TQ_NSA
    col = lambda v: v.reshape(-1, 1)
    wt = lambda w: w.T.astype(CDT)

    cos, sin = _rope_tables(positions, HEAD_DIM)
    cosm, sinm = _rope_tables(positions, MLA_ROPE_DIM)
    rr, jj = np.divmod(np.arange(nc), nj)
    cend = jnp.asarray(((4 * jj + rr) * NSA_CMP_STRIDE + NSA_CMP_BLOCK - 1).reshape(nc, 1), jnp.int32)

    res = x
    for l in range(depth):
        token_major = l == 0
        wmT, wgT = _pack_mixer_weight(w_in[l])
        o1, gates, kcv, kk, ks_aug, kcat = _proj(
            res, col(norm_mix[l]), wmT, cos, sin, cosm, sinm,
            col(mla_q_norm[l]), wt(mla_w_q_b[l]), col(mla_kv_norm[l]), wt(mla_w_kv_b[l]), TM_PROJ,
            token_major)
        kr = kcv.reshape(B, 4, nc, NSA_CMP_STRIDE * HEAD_DIM)
        gates = gates.reshape(B, NSA_KV_GROUPS, 3 * GROUP_HEADS, S)

        pe = jnp.stack([nsa_pe_k[l], nsa_pe_v[l]]).reshape(2, 2, NSA_CMP_STRIDE * HEAD_DIM)
        hpad = LANES - HEAD_DIM
        w1 = jnp.pad(jnp.stack([nsa_wk1[l], nsa_wv1[l]]), ((0, 0), (0, 0), (0, hpad))).astype(CDT)
        w2 = jnp.pad(jnp.stack([nsa_wk2[l], nsa_wv2[l]]), ((0, 0), (0, hpad), (0, 0))).astype(CDT)
        w2t = jnp.swapaxes(w2, 1, 2)
        cmp_tok, cmp_fm = _compress(kr, pe, w1, w2, w2t)

        o_a = _swa(swa_sinks[l], o1, kk, TQ_SWA)
        o_b = _nsa(o1, cend, cmp_tok, cmp_fm, ks_aug, kk, gates, TQ_NSA)
        o_c = _mla(o1, kcat, TQ_MLA)
        xT = _merge(res, col(norm_mix[l]), wgT, o_a, o_b, o_c,
                    wt(w_br_a[l]), wt(w_br_b[l]), wt(w_br_c[l]), wt(w_out[l]), TM_PROJ, token_major)

        n_kv = XATTN_HEADS * XATTN_HEAD_DIM
        kmem, vmemT = _mem_kv(mem, norm_mem[l].reshape(1, D), w_xkv[l][:, :n_kv].astype(CDT),
                              wt(w_xkv[l][:, n_kv:]))
        xT = _xattn(xT, col(norm_xattn[l]), wt(w_xq[l]), kmem, vmemT, wt(w_xo[l]), TM_PROJ)
        res = _ffn(xT, col(norm_ffn[l]), wt(w_gate_up[l]), wt(w_down[l]), TM_PROJ,
                   col(norm_final) if l == depth - 1 else None)
    return res
```

```python
import functools

import numpy as np
import jax
import jax.numpy as jnp
from jax import lax
from jax.experimental import pallas as pl
from jax.experimental.pallas import tpu as pltpu

HEAD_DIM = 64
ROPE_THETA = 10000.0
NORM_EPS = 1e-6
SWA_HEADS = 8
SWA_KV_HEADS = 2
SWA_WINDOW = 128
NSA_HEADS = 8
NSA_KV_GROUPS = 2
NSA_CMP_BLOCK = 32
NSA_CMP_STRIDE = 16
NSA_SEL_BLOCK = 64
NSA_N_SEL = 16
NSA_WINDOW = 512
NSA_FORCE_BONUS = 1e4
MLA_HEADS = 8
MLA_Q_RANK = 384
MLA_KV_RANK = 256
MLA_NOPE_DIM = 64
MLA_ROPE_DIM = 32
MLA_V_DIM = 64
MLA_QK_DIM = MLA_NOPE_DIM + MLA_ROPE_DIM
XATTN_HEADS = 4
XATTN_HEAD_DIM = 128
N_BRANCH = 3
GROUP_HEADS = 4
GROUP_ROWS = GROUP_HEADS * HEAD_DIM
LOG2E = 1.4426950408889634

V7X_VMEM_LIMIT_BYTES = 56 * 1024 * 1024
LANES = 128
ONES_ROWS = 16

CDT = jnp.bfloat16

MASK_VALUE = -1e30

TM_PROJ = 512
TQ_SWA = 256
TQ_NSA = 256
TK_SEL = 512
SEL_BLOCKS_PER_TILE = TK_SEL // NSA_SEL_BLOCK
SEL_CHAIN_LANES = 256
KSEL_DIM = HEAD_DIM + 16
TQ_MLA = 256
TK_MLA = 512
MLA_HEADS_PER_STEP = 8
QK_LEAD = 2
FF_CHUNK = 704

N_Q = SWA_HEADS * HEAD_DIM
N_KV = SWA_KV_HEADS * HEAD_DIM
N_KV3 = 3 * N_KV
N_CMP = 2 * N_KV
N_QM = MLA_HEADS * MLA_QK_DIM
N_KVM = MLA_HEADS * (MLA_NOPE_DIM + MLA_V_DIM)
N_GATE = NSA_HEADS * N_BRANCH
N_TAIL = 64
assert NSA_HEADS * HEAD_DIM == N_Q and NSA_KV_GROUPS == SWA_KV_HEADS
assert MLA_ROPE_DIM + N_GATE <= N_TAIL


def _offsets(*widths):
    return tuple(int(v) for v in np.cumsum((0,) + widths))


R_AQ, R_BQ, R_K, R_V, R_C, R_CQA, R_CKV, R_CKR, N_MIX = _offsets(
    N_Q, N_Q, N_KV3, N_KV3, N_CMP, MLA_Q_RANK, MLA_KV_RANK, N_TAIL)
O_AQ, O_BQ, O_BQR, O_QM, O_VV, O_KV, N_OUT1 = _offsets(N_Q, N_Q, N_Q, N_QM, N_KV3, N_KVM)


def _dot(a, b):
    return jnp.dot(a, b, preferred_element_type=jnp.float32)


def _dot_nt(a, b):
    return lax.dot_general(a, b, (((1,), (1,)), ((), ())), preferred_element_type=jnp.float32)


def _opts(name, *sem):
    return dict(name=name, compiler_params=pltpu.CompilerParams(
        dimension_semantics=sem, vmem_limit_bytes=V7X_VMEM_LIMIT_BYTES))


def _rms_fm(x, g_col):
    ms = jnp.mean(x * x, axis=0, keepdims=True)
    return x * lax.rsqrt(ms + NORM_EPS) * g_col


def _const_spec(shape):
    nd = len(shape)
    return pl.BlockSpec(shape, lambda *_: (0,) * nd, pipeline_mode=pl.Buffered(1))


def _residual_spec(d, tm, token_major):
    if token_major:
        return pl.BlockSpec((1, tm, d), lambda b, i: (b, i, 0))
    return pl.BlockSpec((1, d, tm), lambda b, i: (b, 0, i))


def _load_residual(x_ref, token_major):
    return x_ref[0].T if token_major else x_ref[0]


def _rope_store(o_ref, row0, y, cos, sin, n_heads, head_dim, scale):
    half = head_dim // 2
    for h in range(n_heads):
        x1 = y[h * head_dim:h * head_dim + half]
        x2 = y[h * head_dim + half:(h + 1) * head_dim]
        r = row0 + h * head_dim
        o_ref[0, r:r + half, :] = ((x1 * cos - x2 * sin) * scale).astype(o_ref.dtype)
        o_ref[0, r + half:r + head_dim, :] = ((x2 * cos + x1 * sin) * scale).astype(o_ref.dtype)


def _proj_kernel(x_ref, g_ref, wm_ref, cos_ref, sin_ref, cosm_ref, sinm_ref,
                 qn_ref, wqb_ref, kvn_ref, wkvb_ref, o1_ref, gate_ref, kcv_ref, kk_ref, ks_ref, kc_ref,
                 *, token_major):
    h = _rms_fm(_load_residual(x_ref, token_major), g_ref[...]).astype(CDT)
    cos, sin = cos_ref[0], sin_ref[0]
    cosm, sinm = cosm_ref[0], sinm_ref[0]
    qk_scale = HEAD_DIM ** -0.5 * LOG2E

    lat_q = _rms_fm(_dot(wm_ref[R_CQA:R_CQA + MLA_Q_RANK, :], h), qn_ref[...]).astype(CDT)
    lat_kv = _rms_fm(_dot(wm_ref[R_CKV:R_CKV + MLA_KV_RANK, :], h), kvn_ref[...]).astype(CDT)

    y = _dot(wm_ref[R_AQ:R_AQ + N_Q, :], h)
    _rope_store(o1_ref, O_AQ, y, cos, sin, SWA_HEADS, HEAD_DIM, qk_scale)
    y = _dot(wm_ref[R_BQ:R_BQ + N_Q, :], h)
    o1_ref[0, O_BQ:O_BQ + N_Q, :] = (y * qk_scale).astype(o1_ref.dtype)
    _rope_store(o1_ref, O_BQR, y, cos, sin, NSA_HEADS, HEAD_DIM, qk_scale)
    y = _dot(wm_ref[R_K:R_K + N_KV3, :], h)
    tm = y.shape[1]
    tok = pl.program_id(1) * tm + lax.broadcasted_iota(jnp.int32, (KSEL_DIM - HEAD_DIM, tm), 1)
    blk = lax.shift_right_arithmetic(tok, int(np.log2(NSA_SEL_BLOCK))) & (SEL_BLOCKS_PER_TILE - 1)
    row = lax.broadcasted_iota(jnp.int32, (KSEL_DIM - HEAD_DIM, tm), 0)
    onehot = jnp.where(blk == row, 1.0, 0.0)
    half = HEAD_DIM // 2
    for kh in range(N_KV3 // HEAD_DIM):
        x1 = y[kh * HEAD_DIM:kh * HEAD_DIM + half]
        x2 = y[kh * HEAD_DIM + half:(kh + 1) * HEAD_DIM]
        rows = [x1 * cos - x2 * sin, x2 * cos + x1 * sin]
        if kh in (2, 3):
            ks_ref[0, kh - 2] = jnp.concatenate(rows + [onehot], axis=0).T.astype(ks_ref.dtype)
        else:
            kk_ref[0, kh if kh < 2 else kh - 2] = jnp.concatenate(rows, axis=0).T.astype(kk_ref.dtype)
    y = _dot(wm_ref[R_V:R_V + N_KV3, :], h)
    o1_ref[0, O_VV:O_VV + N_KV3, :] = y.astype(o1_ref.dtype)
    y = _dot(wm_ref[R_C:R_C + N_CMP, :], h)
    for n in range(N_CMP // HEAD_DIM):
        kcv_ref[0, n] = y[n * HEAD_DIM:(n + 1) * HEAD_DIM].T

    y = _dot(wm_ref[R_CKR:R_CKR + N_TAIL, :], h)
    hr = MLA_ROPE_DIM // 2
    x1, x2 = y[0:hr], y[hr:2 * hr]
    k_pe = [x1 * cosm - x2 * sinm, x2 * cosm + x1 * sinm]
    gate_ref[0] = jax.nn.sigmoid(y[MLA_ROPE_DIM:MLA_ROPE_DIM + N_GATE])

    qm = _dot(wqb_ref[...], lat_q)
    m_scale = MLA_QK_DIM ** -0.5 * LOG2E
    for hd in range(MLA_HEADS):
        r = hd * MLA_QK_DIM
        o1_ref[0, O_QM + r:O_QM + r + MLA_NOPE_DIM, :] = (
            qm[r:r + MLA_NOPE_DIM] * m_scale).astype(o1_ref.dtype)
        x1 = qm[r + MLA_NOPE_DIM:r + MLA_NOPE_DIM + hr]
        x2 = qm[r + MLA_NOPE_DIM + hr:r + MLA_QK_DIM]
        o1_ref[0, O_QM + r + MLA_NOPE_DIM:O_QM + r + MLA_NOPE_DIM + hr, :] = (
            (x1 * cosm - x2 * sinm) * m_scale).astype(o1_ref.dtype)
        o1_ref[0, O_QM + r + MLA_NOPE_DIM + hr:O_QM + r + MLA_QK_DIM, :] = (
            (x2 * cosm + x1 * sinm) * m_scale).astype(o1_ref.dtype)
    kv = _dot(wkvb_ref[...], lat_kv)
    o1_ref[0, O_KV:O_KV + N_KVM, :] = kv.astype(o1_ref.dtype)
    for hd in range(MLA_HEADS):
        k_nope = kv[hd * 2 * MLA_NOPE_DIM:hd * 2 * MLA_NOPE_DIM + MLA_NOPE_DIM]
        kc_ref[0, hd] = jnp.concatenate([k_nope] + k_pe, axis=0).T.astype(kc_ref.dtype)


def _proj(x, g_col, wmT, cos, sin, cosm, sinm, qn, wqbT, kvn, wkvbT, tm, token_major):
    B, D, S = (x.shape[0], x.shape[2], x.shape[1]) if token_major else x.shape
    tok = lambda rows: pl.BlockSpec((1, rows, tm), lambda b, i: (b, 0, i))
    return pl.pallas_call(
        functools.partial(_proj_kernel, token_major=token_major),
        grid=(B, S // tm),
        in_specs=[_residual_spec(D, tm, token_major), _const_spec((D, 1)), _const_spec(wmT.shape),
                  tok(HEAD_DIM // 2), tok(HEAD_DIM // 2), tok(MLA_ROPE_DIM // 2), tok(MLA_ROPE_DIM // 2),
                  _const_spec(qn.shape), _const_spec(wqbT.shape),
                  _const_spec(kvn.shape), _const_spec(wkvbT.shape)],
        out_specs=[tok(N_OUT1), tok(N_GATE),
                   pl.BlockSpec((1, N_CMP // HEAD_DIM, tm, HEAD_DIM), lambda b, i: (b, 0, i, 0)),
                   pl.BlockSpec((1, 2 * SWA_KV_HEADS, tm, HEAD_DIM), lambda b, i: (b, 0, i, 0)),
                   pl.BlockSpec((1, NSA_KV_GROUPS, tm, KSEL_DIM), lambda b, i: (b, 0, i, 0)),
                   pl.BlockSpec((1, MLA_HEADS, tm, MLA_QK_DIM), lambda b, i: (b, 0, i, 0))],
        out_shape=[jax.ShapeDtypeStruct((B, N_OUT1, S), CDT),
                   jax.ShapeDtypeStruct((B, N_GATE, S), jnp.float32),
                   jax.ShapeDtypeStruct((B, N_CMP // HEAD_DIM, S, HEAD_DIM), jnp.float32),
                   jax.ShapeDtypeStruct((B, 2 * SWA_KV_HEADS, S, HEAD_DIM), CDT),
                   jax.ShapeDtypeStruct((B, NSA_KV_GROUPS, S, KSEL_DIM), CDT),
                   jax.ShapeDtypeStruct((B, MLA_HEADS, S, MLA_QK_DIM), CDT)],
        **_opts("mixer_proj", "parallel", "parallel"),
    )(x, g_col, wmT, cos, sin, cosm, sinm, qn, wqbT, kvn, wkvbT)


def _compress_kernel(kr_ref, pe_ref, w1_ref, w2_ref, w2t_ref, tok_ref, fm_ref, hid_ref):
    kr = kr_ref[0, 0]
    nc, half = kr.shape
    nj = nc // 4
    a = _dot((kr + pe_ref[0, 0:1, :]).astype(CDT), w1_ref[0, 0:half, :])
    bm = _dot((kr + pe_ref[0, 1:2, :]).astype(CDT), w1_ref[0, half:2 * half, :])
    nxt = pltpu.roll(bm, nc - 1, 0)
    row = lax.broadcasted_iota(jnp.int32, (nc, 1), 0)
    pre = a + jnp.where(row == nc - 1, 0.0, nxt)
    hid_ref[...] = pre * jax.nn.sigmoid(pre)
    hid = jnp.concatenate([hid_ref[pl.ds(r, nj, stride=4), :] for r in range(4)], axis=0).astype(CDT)
    tok_ref[0, 0] = _dot(hid, w2_ref[0]).astype(tok_ref.dtype)
    fm_ref[0, 0] = _dot_nt(w2t_ref[0], hid).astype(fm_ref.dtype)


def _compress(kr, pe, w1, w2, w2t):
    B, _, NC, W = kr.shape
    return pl.pallas_call(
        _compress_kernel,
        grid=(B, 4),
        in_specs=[pl.BlockSpec((1, 1, NC, W), lambda b, n: (b, n, 0, 0)),
                  pl.BlockSpec((1, 2, W), lambda b, n: (n // 2, 0, 0)),
                  pl.BlockSpec((1, 2 * W, LANES), lambda b, n: (n // 2, 0, 0)),
                  pl.BlockSpec((1, LANES, HEAD_DIM), lambda b, n: (n // 2, 0, 0)),
                  pl.BlockSpec((1, HEAD_DIM, LANES), lambda b, n: (n // 2, 0, 0))],
        out_specs=[pl.BlockSpec((1, 1, NC, HEAD_DIM), lambda b, n: (b, n, 0, 0)),
                   pl.BlockSpec((1, 1, HEAD_DIM, NC), lambda b, n: (b, n, 0, 0))],
        out_shape=[jax.ShapeDtypeStruct((B, 4, NC, HEAD_DIM), CDT),
                   jax.ShapeDtypeStruct((B, 4, HEAD_DIM, NC), CDT)],
        scratch_shapes=[pltpu.VMEM((NC, LANES), jnp.float32)],
        **_opts("nsa_compress", "parallel", "parallel"),
    )(kr, pe, w1, w2, w2t)


def _group_queries(q_ref, g):
    return jnp.concatenate([q_ref[0, (g * GROUP_HEADS + r) * HEAD_DIM:(g * GROUP_HEADS + r + 1) * HEAD_DIM, :]
                            for r in range(GROUP_HEADS)], axis=1)


def _with_ones(vt):
    return jnp.concatenate([vt, jnp.ones((ONES_ROWS, vt.shape[1]), vt.dtype)], axis=0)


def _softmax_probs(sc, m_run):
    m_new = jnp.maximum(m_run, jnp.max(sc, axis=0, keepdims=True))
    return m_new, jnp.exp2(sc - m_new).astype(CDT), jnp.exp2(m_run - m_new)


def _softmax_step(sc, m_run, acc, vt_aug):
    m_new, p, alpha = _softmax_probs(sc, m_run)
    return m_new, alpha * acc + _dot(vt_aug, p)


def _window_scores(q4, t4, k, k0, window):
    span = k.shape[0]
    s = _dot(k, q4)
    kpos = k0 + lax.broadcasted_iota(jnp.int32, (span, 1), 0)
    return jnp.where(kpos <= t4, jnp.where(kpos > t4 - window, s, -jnp.inf), -jnp.inf)


def _window_output(s, vt, sink4=None):
    m = jnp.max(s, axis=0, keepdims=True)
    if sink4 is not None:
        m = jnp.maximum(m, sink4)
    pv = _dot(_with_ones(vt), jnp.exp2(s - m).astype(CDT))
    den = pv[HEAD_DIM:HEAD_DIM + 1]
    if sink4 is not None:
        den = den + jnp.exp2(sink4 - m)
    return pv[0:HEAD_DIM] * (1.0 / den)


def _swa_kernel(sink_ref, q_ref, k_ref, vt_ref, o_ref, *, tq):
    qs = pl.program_id(1) * tq
    t = qs + lax.broadcasted_iota(jnp.int32, (1, tq), 1)
    t4 = jnp.concatenate([t] * GROUP_HEADS, axis=1)
    span = SWA_WINDOW + tq
    k0 = pl.multiple_of(jnp.maximum(qs - SWA_WINDOW, 0), LANES)
    scores = [_window_scores(_group_queries(q_ref, g), t4, k_ref[0, g, pl.ds(k0, span), :],
                             k0, SWA_WINDOW) for g in range(SWA_KV_HEADS)]
    for g in range(SWA_KV_HEADS):
        sink4 = jnp.concatenate([jnp.full((1, tq), sink_ref[g * GROUP_HEADS + r] * LOG2E, jnp.float32)
                                 for r in range(GROUP_HEADS)], axis=1)
        o4 = _window_output(scores[g], vt_ref[0, g * HEAD_DIM:(g + 1) * HEAD_DIM, pl.ds(k0, span)], sink4)
        for r in range(GROUP_HEADS):
            row = (g * GROUP_HEADS + r) * HEAD_DIM
            o_ref[0, row:row + HEAD_DIM, :] = o4[:, r * tq:(r + 1) * tq].astype(o_ref.dtype)


def _swa(sinks, o1, kk, tq):
    B, _, S = o1.shape
    rows = SWA_HEADS * HEAD_DIM
    vrows = SWA_KV_HEADS * HEAD_DIM
    assert O_AQ % rows == 0 and O_VV % vrows == 0
    return pl.pallas_call(
        functools.partial(_swa_kernel, tq=tq),
        grid=(B, S // tq),
        in_specs=[pl.BlockSpec(memory_space=pltpu.SMEM),
                  pl.BlockSpec((1, rows, tq), lambda b, i: (b, O_AQ // rows, i)),
                  pl.BlockSpec((1, SWA_KV_HEADS, S, HEAD_DIM), lambda b, i: (b, 0, 0, 0)),
                  pl.BlockSpec((1, vrows, S), lambda b, i: (b, O_VV // vrows, 0))],
        out_specs=pl.BlockSpec((1, rows, tq), lambda b, i: (b, 0, i)),
        out_shape=jax.ShapeDtypeStruct((B, rows, S), CDT),
        **_opts("swa_attn", "parallel", "parallel"),
    )(sinks, o1, kk, o1)


def _nsa_kernel(q_ref, qr_ref, cend_ref, kc_ref, vct_ref, ks_ref, vst_ref, kw_ref, vwt_ref,
                gate_ref, o_ref, bias_ref, s_a, s_b, *, tq, n_sel):
    qs = pl.program_id(1) * tq
    nc = kc_ref.shape[2]
    nj = nc // 4
    G = NSA_KV_GROUPS
    t = qs + lax.broadcasted_iota(jnp.int32, (1, tq), 1)
    hpc = SEL_CHAIN_LANES // tq
    cpg = GROUP_HEADS // hpc
    tc = jnp.concatenate([t] * hpc, axis=1)
    t4 = jnp.concatenate([t] * GROUP_HEADS, axis=1)
    jrow = lax.broadcasted_iota(jnp.int32, (nj, tq), 0)
    jf = jrow.astype(jnp.float32)
    cur = lax.shift_right_arithmetic(t, int(np.log2(NSA_SEL_BLOCK)))
    bonus = jnp.where((jrow == 0) | (jrow == cur) | (jrow == cur - 1), NSA_FORCE_BONUS, 0.0)
    span = NSA_WINDOW + tq
    kw0 = pl.multiple_of(jnp.maximum(qs - NSA_WINDOW, 0), LANES)

    qr4 = [_group_queries(qr_ref, g) for g in range(G)]
    s_cmp = [_dot(kc_ref[0, g], _group_queries(q_ref, g)) for g in range(G)]
    s_win = [_window_scores(qr4[g], t4, kw_ref[0, g, pl.ds(kw0, span), :], kw0, NSA_WINDOW)
             for g in range(G)]
    p_cmp = []
    for g in range(G):
        s = jnp.where(cend_ref[...] <= t4, s_cmp[g], -jnp.inf)
        m = jnp.max(s, axis=0, keepdims=True)
        m = jnp.where(m == -jnp.inf, 0.0, m)
        e = jnp.exp2(s - m)
        den = jnp.sum(e, axis=0, keepdims=True)
        p_cmp.append(e * (1.0 / jnp.where(den > 0, den, 1.0)))
    o_cmps = [_dot(vct_ref[0, g], p_cmp[g].astype(CDT)) for g in range(G)]
    o_wins = [_window_output(s_win[g], vwt_ref[0, g * HEAD_DIM:(g + 1) * HEAD_DIM, pl.ds(kw0, span)])
              for g in range(G)]

    fixed = []
    for g in range(G):
        p, o_cmp, o_win = p_cmp[g], o_cmps[g], o_wins[g]
        psum = p[:, 0:tq]
        for r in range(1, GROUP_HEADS):
            psum = psum + p[:, r * tq:(r + 1) * tq]
        p3 = psum[3 * nj:4 * nj]
        prev = jnp.where(jrow == 0, 0.0, pltpu.roll(p3, 1, 0))
        imp = prev + 2.0 * (psum[0:nj] + psum[nj:2 * nj] + psum[2 * nj:3 * nj]) + p3

        score = jnp.where(jrow <= cur, imp + bonus, -jnp.inf)
        for _ in range(n_sel):
            mx = jnp.max(score, axis=0, keepdims=True)
            idx = jnp.min(jnp.where(score == mx, jf, float(nj)), axis=0, keepdims=True)
            idx = jnp.where(mx > -jnp.inf, idx, float(nj))
            score = jnp.where(jf == idx, -jnp.inf, score)
        bias_ref[g] = jnp.where(jrow <= cur, jnp.where(score == -jnp.inf, 0.0, MASK_VALUE), MASK_VALUE)

        parts = []
        for r in range(GROUP_HEADS):
            sl = slice(r * tq, (r + 1) * tq)
            g0 = gate_ref[0, g, 3 * r:3 * r + 1, :]
            g2 = gate_ref[0, g, 3 * r + 2:3 * r + 3, :]
            parts.append((g0 * o_cmp[:, sl], g2 * o_win[:, sl]))
        fixed.append(parts)

    def augmented_queries(kt):
        out = []
        for g in range(G):
            b8 = bias_ref[g, pl.ds(kt * SEL_BLOCKS_PER_TILE, SEL_BLOCKS_PER_TILE), :]
            b16 = jnp.concatenate([b8, jnp.zeros_like(b8)], axis=0)
            out.append(jnp.concatenate(
                [qr4[g], jnp.concatenate([b16] * GROUP_HEADS, axis=1).astype(CDT)], axis=0))
        return out

    def score_chain(kt, s_ref, qa, c):
        g, hp = divmod(c, cpg)
        k0 = pl.multiple_of(kt * TK_SEL, TK_SEL)
        s_ref[c] = _dot(ks_ref[0, g, pl.ds(k0, TK_SEL), :],
                        qa[g][:, hp * SEL_CHAIN_LANES:(hp + 1) * SEL_CHAIN_LANES])

    def consume_chain(kt, s_ref, c, st, diagonal):
        g = c // cpg
        k0 = pl.multiple_of(kt * TK_SEL, TK_SEL)
        sc = s_ref[c]
        if diagonal:
            kpos = k0 + lax.broadcasted_iota(jnp.int32, (TK_SEL, 1), 0)
            sc = jnp.where(kpos <= tc, sc, MASK_VALUE)
        vt_aug = _with_ones(vst_ref[0, g * HEAD_DIM:(g + 1) * HEAD_DIM, pl.ds(k0, TK_SEL)])
        return _softmax_step(sc, st[0], st[1], vt_aug)

    def step(kt, s_cur, s_nxt, state, diagonal=False):
        n_chain = cpg * G
        if s_nxt is not None:
            qa = augmented_queries(kt + 1)
            for c in range(QK_LEAD):
                score_chain(kt + 1, s_nxt, qa, c)
        out = []
        for c in range(n_chain):
            out.append(consume_chain(kt, s_cur, c, state[c], diagonal))
            if s_nxt is not None and c + QK_LEAD < n_chain:
                score_chain(kt + 1, s_nxt, qa, c + QK_LEAD)
        return tuple(out)

    def finish(state):
        for g in range(G):
            for r in range(GROUP_HEADS):
                _, acc = state[cpg * g + r // hpc]
                sl = slice((r % hpc) * tq, (r % hpc + 1) * tq)
                o_sel = acc[0:HEAD_DIM, sl] * (1.0 / acc[HEAD_DIM:HEAD_DIM + 1, sl])
                g1 = gate_ref[0, g, 3 * r + 1:3 * r + 2, :]
                c_part, w_part = fixed[g][r]
                row = (g * GROUP_HEADS + r) * HEAD_DIM
                o_ref[0, row:row + HEAD_DIM, :] = (c_part + g1 * o_sel + w_part).astype(o_ref.dtype)

    def pair(j, state):
        return step(2 * j + 1, s_b, s_a, step(2 * j, s_a, s_b, state))

    init = tuple((jnp.full((1, SEL_CHAIN_LANES), MASK_VALUE, jnp.float32),
                  jnp.zeros((HEAD_DIM + ONES_ROWS, SEL_CHAIN_LANES), jnp.float32))
                 for _ in range(cpg * G))
    n_full = qs // TK_SEL
    qa0 = augmented_queries(0)
    for c in range(cpg * G):
        score_chain(0, s_a, qa0, c)
    state = lax.fori_loop(0, n_full // 2, pair, init)
    odd = n_full % 2 == 1

    @pl.when(odd)
    def _():
        finish(step(n_full, s_b, None, step(n_full - 1, s_a, s_b, state), diagonal=True))

    @pl.when(jnp.logical_not(odd))
    def _():
        finish(step(n_full, s_a, None, state, diagonal=True))


def _nsa(o1, cend, cmp_tok, cmp_fm, ks_aug, kk, gates, tq):
    B, _, S = o1.shape
    nc = cmp_tok.shape[2]
    nj = nc // 4
    n_sel = min(NSA_N_SEL, nj)
    G = NSA_KV_GROUPS
    rows = NSA_HEADS * HEAD_DIM
    vrows = G * HEAD_DIM
    assert O_BQ % rows == 0 and O_BQR % rows == 0 and O_VV % vrows == 0
    assert SEL_CHAIN_LANES % tq == 0 and tq <= TK_SEL
    return pl.pallas_call(
        functools.partial(_nsa_kernel, tq=tq, n_sel=n_sel),
        grid=(B, S // tq),
        in_specs=[pl.BlockSpec((1, rows, tq), lambda b, i: (b, O_BQ // rows, i)),
                  pl.BlockSpec((1, rows, tq), lambda b, i: (b, O_BQR // rows, i)),
                  _const_spec((nc, 1)),
                  pl.BlockSpec((1, G, nc, HEAD_DIM), lambda b, i: (b, 0, 0, 0)),
                  pl.BlockSpec((1, G, HEAD_DIM, nc), lambda b, i: (b, 1, 0, 0)),
                  pl.BlockSpec((1, G, S, KSEL_DIM), lambda b, i: (b, 0, 0, 0)),
                  pl.BlockSpec((1, vrows, S), lambda b, i: (b, O_VV // vrows + 1, 0)),
                  pl.BlockSpec((1, G, S, HEAD_DIM), lambda b, i: (b, 1, 0, 0)),
                  pl.BlockSpec((1, vrows, S), lambda b, i: (b, O_VV // vrows + 2, 0)),
                  pl.BlockSpec((1, G, 3 * GROUP_HEADS, tq), lambda b, i: (b, 0, 0, i))],
        out_specs=pl.BlockSpec((1, rows, tq), lambda b, i: (b, 0, i)),
        out_shape=jax.ShapeDtypeStruct((B, rows, S), CDT),
        scratch_shapes=[pltpu.VMEM((G, nj, tq), jnp.float32)]
                       + [pltpu.VMEM((NSA_HEADS * tq // SEL_CHAIN_LANES, TK_SEL, SEL_CHAIN_LANES),
                                     jnp.float32)] * 2,
        **_opts("nsa_attn", "parallel", "arbitrary"),
    )(o1, o1, cend, cmp_tok, cmp_fm, ks_aug, o1, kk, o1, gates)


def _mla_kernel(q_ref, k_ref, *rest, tq, hps):
    vt_refs, o_ref, s_a, s_b = rest[:hps], rest[hps], rest[hps + 1], rest[hps + 2]
    qs = pl.program_id(2) * tq
    q = [q_ref[0, h * MLA_QK_DIM:(h + 1) * MLA_QK_DIM, :] for h in range(hps)]
    t = qs + lax.broadcasted_iota(jnp.int32, (1, tq), 1)

    def score_chain(kt, s_ref, h):
        k0 = pl.multiple_of(kt * TK_MLA, TK_MLA)
        s_ref[h] = _dot(k_ref[0, h, pl.ds(k0, TK_MLA), :], q[h])

    def consume_chain(kt, s_ref, h, st, diagonal):
        k0 = pl.multiple_of(kt * TK_MLA, TK_MLA)
        sc = s_ref[h]
        if diagonal:
            kpos = k0 + lax.broadcasted_iota(jnp.int32, (TK_MLA, 1), 0)
            sc = jnp.where(kpos <= t, sc, MASK_VALUE)
        return _softmax_step(sc, st[0], st[1], _with_ones(vt_refs[h][0, :, pl.ds(k0, TK_MLA)]))

    def step(kt, s_cur, s_nxt, state, diagonal=False):
        lead = QK_LEAD
        if s_nxt is not None:
            for h in range(lead):
                score_chain(kt + 1, s_nxt, h)
        out = []
        for h in range(hps):
            out.append(consume_chain(kt, s_cur, h, state[h], diagonal))
            if s_nxt is not None and h + lead < hps:
                score_chain(kt + 1, s_nxt, h + lead)
        return tuple(out)

    def finish(state):
        for h in range(hps):
            _, acc = state[h]
            o_ref[0, h * MLA_V_DIM:(h + 1) * MLA_V_DIM, :] = (
                acc[0:MLA_V_DIM] * (1.0 / acc[MLA_V_DIM:MLA_V_DIM + 1])).astype(o_ref.dtype)

    def pair(j, state):
        return step(2 * j + 1, s_b, s_a, step(2 * j, s_a, s_b, state))

    init = tuple((jnp.full((1, tq), MASK_VALUE, jnp.float32),
                  jnp.zeros((MLA_V_DIM + ONES_ROWS, tq), jnp.float32)) for _ in range(hps))
    n_full = qs // TK_MLA
    for h in range(hps):
        score_chain(0, s_a, h)
    state = lax.fori_loop(0, n_full // 2, pair, init)
    odd = n_full % 2 == 1

    @pl.when(odd)
    def _():
        finish(step(n_full, s_b, None, step(n_full - 1, s_a, s_b, state), diagonal=True))

    @pl.when(jnp.logical_not(odd))
    def _():
        finish(step(n_full, s_a, None, state, diagonal=True))


def _mla(o1, kcat, tq):
    B, _, S = o1.shape
    hps = MLA_HEADS_PER_STEP
    assert O_QM % (hps * MLA_QK_DIM) == 0 and O_KV % MLA_V_DIM == 0 and MLA_HEADS % hps == 0
    v_spec = lambda h: pl.BlockSpec(
        (1, MLA_V_DIM, S), lambda b, hg, i: (b, O_KV // MLA_V_DIM + 2 * (hg * hps + h) + 1, 0),
        pipeline_mode=pl.Buffered(1))
    return pl.pallas_call(
        functools.partial(_mla_kernel, tq=tq, hps=hps),
        grid=(B, MLA_HEADS // hps, S // tq),
        in_specs=[pl.BlockSpec((1, hps * MLA_QK_DIM, tq),
                               lambda b, hg, i: (b, O_QM // (hps * MLA_QK_DIM) + hg, i)),
                  pl.BlockSpec((1, hps, S, MLA_QK_DIM), lambda b, hg, i: (b, hg, 0, 0),
                               pipeline_mode=pl.Buffered(1))]
                 + [v_spec(h) for h in range(hps)],
        out_specs=pl.BlockSpec((1, hps * MLA_V_DIM, tq), lambda b, hg, i: (b, hg, i)),
        out_shape=jax.ShapeDtypeStruct((B, MLA_HEADS * MLA_V_DIM, S), CDT),
        scratch_shapes=[pltpu.VMEM((hps, TK_MLA, tq), jnp.float32)] * 2,
        **_opts("mla_attn", "parallel", "parallel", "parallel"),
    )(o1, kcat, *([o1] * hps))


def _merge_kernel(x_ref, g_ref, wg_ref, oa_ref, ob_ref, oc_ref, wa_ref, wb_ref, wc_ref, wo_ref, o_ref,
                  *, token_major):
    x = _load_residual(x_ref, token_major)
    d = x.shape[0]
    h = _rms_fm(x, g_ref[...]).astype(CDT)
    merged = None
    for i, (br_ref, w_ref) in enumerate(((oa_ref, wa_ref), (ob_ref, wb_ref), (oc_ref, wc_ref))):
        gate = jax.nn.sigmoid(_dot(wg_ref[i * d:(i + 1) * d, :], h))
        term = gate * _dot(w_ref[...], br_ref[0])
        merged = term if merged is None else merged + term
    o_ref[0] = x + _dot(wo_ref[...], merged.astype(CDT))


def _merge(x, g_col, wgT, oa, ob, oc, waT, wbT, wcT, woT, tm, token_major):
    B, D, S = (x.shape[0], x.shape[2], x.shape[1]) if token_major else x.shape
    tok = lambda rows: pl.BlockSpec((1, rows, tm), lambda b, i: (b, 0, i))
    return pl.pallas_call(
        functools.partial(_merge_kernel, token_major=token_major),
        grid=(B, S // tm),
        in_specs=[_residual_spec(D, tm, token_major), _const_spec((D, 1)), _const_spec(wgT.shape),
                  tok(oa.shape[1]), tok(ob.shape[1]), tok(oc.shape[1]),
                  _const_spec(waT.shape), _const_spec(wbT.shape), _const_spec(wcT.shape),
                  _const_spec(woT.shape)],
        out_specs=tok(D),
        out_shape=jax.ShapeDtypeStruct((B, D, S), x.dtype),
        **_opts("merge_out", "parallel", "parallel"),
    )(x, g_col, wgT, oa, ob, oc, waT, wbT, wcT, woT)


def _mem_kv_kernel(mem_ref, g_ref, wk_ref, wvt_ref, k_ref, vt_ref):
    m = mem_ref[0]
    ms = jnp.mean(m * m, axis=-1, keepdims=True)
    hm = (m * lax.rsqrt(ms + NORM_EPS) * g_ref[...]).astype(CDT)
    k_ref[0] = _dot(hm, wk_ref[...]).astype(k_ref.dtype)
    vt_ref[0] = _dot_nt(wvt_ref[...], hm).astype(vt_ref.dtype)


def _mem_kv(mem, g_row, wk, wvT):
    B, M, D = mem.shape
    n = wk.shape[1]
    return pl.pallas_call(
        _mem_kv_kernel,
        grid=(B,),
        in_specs=[pl.BlockSpec((1, M, D), lambda b: (b, 0, 0)), _const_spec((1, D)),
                  _const_spec(wk.shape), _const_spec(wvT.shape)],
        out_specs=[pl.BlockSpec((1, M, n), lambda b: (b, 0, 0)),
                   pl.BlockSpec((1, n, M), lambda b: (b, 0, 0))],
        out_shape=[jax.ShapeDtypeStruct((B, M, n), CDT), jax.ShapeDtypeStruct((B, n, M), CDT)],
        **_opts("mem_kv", "parallel"),
    )(mem, g_row, wk, wvT)


def _xattn_kernel(x_ref, g_ref, wq_ref, k_ref, vt_ref, wo_ref, o_ref):
    x = x_ref[0]
    h = _rms_fm(x, g_ref[...]).astype(CDT)
    q = (_dot(wq_ref[...], h) * (XATTN_HEAD_DIM ** -0.5 * LOG2E)).astype(CDT)
    head_rows = [slice(hd * XATTN_HEAD_DIM, (hd + 1) * XATTN_HEAD_DIM) for hd in range(XATTN_HEADS)]
    scores = [_dot(k_ref[0, :, rows], q[rows]) for rows in head_rows]
    probs = [jnp.exp2(s - jnp.max(s, axis=0, keepdims=True)).astype(CDT) for s in scores]
    outs = []
    for rows, e in zip(head_rows, probs):
        pv = _dot(_with_ones(vt_ref[0, rows, :]), e)
        outs.append((pv[0:XATTN_HEAD_DIM] * (1.0 / pv[XATTN_HEAD_DIM:XATTN_HEAD_DIM + 1])).astype(CDT))
    o_ref[0] = x + _dot(wo_ref[...], jnp.concatenate(outs, axis=0))


def _xattn(xT, g_col, wqT, kmem, vmemT, woT, tm):
    B, D, S = xT.shape
    M, n = kmem.shape[1], kmem.shape[2]
    tok = lambda rows: pl.BlockSpec((1, rows, tm), lambda b, i: (b, 0, i))
    return pl.pallas_call(
        _xattn_kernel,
        grid=(B, S // tm),
        in_specs=[tok(D), _const_spec((D, 1)), _const_spec(wqT.shape),
                  pl.BlockSpec((1, M, n), lambda b, i: (b, 0, 0)),
                  pl.BlockSpec((1, n, M), lambda b, i: (b, 0, 0)),
                  _const_spec(woT.shape)],
        out_specs=tok(D),
        out_shape=jax.ShapeDtypeStruct((B, D, S), xT.dtype),
        **_opts("xattn", "parallel", "parallel"),
    )(xT, g_col, wqT, kmem, vmemT, woT)


def _ffn_kernel(x_ref, g_ref, wgu_ref, wd_ref, *rest, d_ff, final_norm):
    o_ref = rest[-1]
    x = x_ref[0]
    h = _rms_fm(x, g_ref[...]).astype(CDT)
    acc = x
    for c in range(d_ff // FF_CHUNK):
        r = c * FF_CHUNK
        gate = _dot(wgu_ref[r:r + FF_CHUNK, :], h)
        up = _dot(wgu_ref[d_ff + r:d_ff + r + FF_CHUNK, :], h)
        act = (gate * jax.nn.sigmoid(gate) * up).astype(CDT)
        acc = acc + _dot(wd_ref[:, r:r + FF_CHUNK], act)
    if final_norm:
        o_ref[0] = _rms_fm(acc, rest[0][...]).T
    else:
        o_ref[0] = acc


def _ffn(xT, g_col, wguT, wdT, tm, final_g_col=None):
    B, D, S = xT.shape
    d_ff = wdT.shape[1]
    final_norm = final_g_col is not None
    tok = lambda rows: pl.BlockSpec((1, rows, tm), lambda b, i: (b, 0, i))
    extra = [final_g_col] if final_norm else []
    return pl.pallas_call(
        functools.partial(_ffn_kernel, d_ff=d_ff, final_norm=final_norm),
        grid=(B, S // tm),
        in_specs=[tok(D), _const_spec((D, 1)), _const_spec(wguT.shape), _const_spec(wdT.shape)]
                 + [_const_spec((D, 1))] * len(extra),
        out_specs=_residual_spec(D, tm, final_norm),
        out_shape=jax.ShapeDtypeStruct((B, S, D) if final_norm else (B, D, S), xT.dtype),
        **_opts("ffn", "parallel", "parallel"),
    )(xT, g_col, wguT, wdT, *extra)


def _rope_tables(positions, dim):
    half = dim // 2
    inv_freq = ROPE_THETA ** (-jnp.arange(half, dtype=jnp.float32) / half)
    ang = positions.astype(jnp.float32)[:, None, :] * inv_freq[None, :, None]
    return jnp.cos(ang), jnp.sin(ang)


def _pack_mixer_weight(w_in):
    off = _offsets(N_Q, N_KV, N_KV, N_Q, N_KV, N_KV, N_KV, N_KV, N_KV, N_KV, N_GATE,
                   MLA_Q_RANK, MLA_KV_RANK, MLA_ROPE_DIM)
    a_q, a_k, a_v, b_q, b_kc, b_vc, b_ks, b_vs, b_kw, b_vw, b_g, c_qa, c_kv, c_kr = [
        w_in[:, off[i]:off[i + 1]] for i in range(14)]
    packed = jnp.concatenate([a_q, b_q, a_k, b_ks, b_kw, a_v, b_vs, b_vw, b_kc, b_vc,
                              c_qa, c_kv, c_kr, b_g], axis=1)
    packed = jnp.pad(packed, ((0, 0), (0, N_MIX - packed.shape[1])))
    return packed.T.astype(CDT), w_in[:, off[14]:].T.astype(CDT)


def kernel(x, mem, positions, norm_mix, w_in, swa_sinks, nsa_pe_k, nsa_pe_v, nsa_wk1, nsa_wk2,
           nsa_wv1, nsa_wv2, mla_q_norm, mla_w_q_b, mla_kv_norm, mla_w_kv_b, w_br_a, w_br_b,
           w_br_c, w_out, norm_xattn, norm_mem, w_xq, w_xkv, w_xo, norm_ffn, w_gate_up, w_down,
           norm_final):
    B, S, D = x.shape
    depth = w_in.shape[0]
    nj = S // NSA_SEL_BLOCK
    nc = 4 * nj
    assert S % TK_SEL == 0 and S % TM_PROJ == 0 and S >= NSA_WINDOW + TQ_NSA
    col = lambda v: v.reshape(-1, 1)
    wt = lambda w: w.T.astype(CDT)

    cos, sin = _rope_tables(positions, HEAD_DIM)
    cosm, sinm = _rope_tables(positions, MLA_ROPE_DIM)
    rr, jj = np.divmod(np.arange(nc), nj)
    cend = jnp.asarray(((4 * jj + rr) * NSA_CMP_STRIDE + NSA_CMP_BLOCK - 1).reshape(nc, 1), jnp.int32)

    res = x
    for l in range(depth):
        token_major = l == 0
        wmT, wgT = _pack_mixer_weight(w_in[l])
        o1, gates, kcv, kk, ks_aug, kcat = _proj(
            res, col(norm_mix[l]), wmT, cos, sin, cosm, sinm,
            col(mla_q_norm[l]), wt(mla_w_q_b[l]), col(mla_kv_norm[l]), wt(mla_w_kv_b[l]), TM_PROJ,
            token_major)
        kr = kcv.reshape(B, N_CMP // HEAD_DIM, nc, NSA_CMP_STRIDE * HEAD_DIM)
        gates = gates.reshape(B, NSA_KV_GROUPS, 3 * GROUP_HEADS, S)

        pe = jnp.stack([nsa_pe_k[l], nsa_pe_v[l]]).reshape(2, 2, NSA_CMP_STRIDE * HEAD_DIM)
        hpad = LANES - HEAD_DIM
        w1 = jnp.pad(jnp.stack([nsa_wk1[l], nsa_wv1[l]]), ((0, 0), (0, 0), (0, hpad))).astype(CDT)
        w2 = jnp.pad(jnp.stack([nsa_wk2[l], nsa_wv2[l]]), ((0, 0), (0, hpad), (0, 0))).astype(CDT)
        w2t = jnp.swapaxes(w2, 1, 2)
        cmp_tok, cmp_fm = _compress(kr, pe, w1, w2, w2t)

        o_a = _swa(swa_sinks[l], o1, kk, TQ_SWA)
        o_b = _nsa(o1, cend, cmp_tok, cmp_fm, ks_aug, kk, gates, TQ_NSA)
        o_c = _mla(o1, kcat, TQ_MLA)
        xT = _merge(res, col(norm_mix[l]), wgT, o_a, o_b, o_c,
                    wt(w_br_a[l]), wt(w_br_b[l]), wt(w_br_c[l]), wt(w_out[l]), TM_PROJ, token_major)

        n_kv = XATTN_HEADS * XATTN_HEAD_DIM
        kmem, vmemT = _mem_kv(mem, norm_mem[l].reshape(1, D), w_xkv[l][:, :n_kv].astype(CDT),
                              wt(w_xkv[l][:, n_kv:]))
        xT = _xattn(xT, col(norm_xattn[l]), wt(w_xq[l]), kmem, vmemT, wt(w_xo[l]), TM_PROJ)
        res = _ffn(xT, col(norm_ffn[l]), wt(w_gate_up[l]), wt(w_down[l]), TM_PROJ,
                   col(norm_final) if l == depth - 1 else None)
    return res
```

```python
import functools

import numpy as np
import jax
import jax.numpy as jnp
from jax import lax
from jax.experimental import pallas as pl
from jax.experimental.pallas import tpu as pltpu

HEAD_DIM = 64
ROPE_THETA = 10000.0
NORM_EPS = 1e-6
SWA_HEADS = 8
SWA_KV_HEADS = 2
SWA_WINDOW = 128
NSA_HEADS = 8
NSA_KV_GROUPS = 2
NSA_CMP_BLOCK = 32
NSA_CMP_STRIDE = 16
NSA_SEL_BLOCK = 64
NSA_N_SEL = 16
NSA_WINDOW = 512
NSA_FORCE_BONUS = 1e4
MLA_HEADS = 8
MLA_Q_RANK = 384
MLA_KV_RANK = 256
MLA_NOPE_DIM = 64
MLA_ROPE_DIM = 32
MLA_V_DIM = 64
MLA_QK_DIM = MLA_NOPE_DIM + MLA_ROPE_DIM
XATTN_HEADS = 4
XATTN_HEAD_DIM = 128
N_BRANCH = 3
GROUP_HEADS = 4
GROUP_ROWS = GROUP_HEADS * HEAD_DIM
LOG2E = 1.4426950408889634

V7X_VMEM_LIMIT_BYTES = 56 * 1024 * 1024
LANES = 128
ONES_ROWS = 16

CDT = jnp.bfloat16

MASK_VALUE = -1e30

TM_PROJ = 512
TQ_NSA = 256
TK_SEL = 512
SEL_BLOCKS_PER_TILE = TK_SEL // NSA_SEL_BLOCK
SEL_CHAIN_LANES = 256
KSEL_DIM = HEAD_DIM + 16
TQ_MLA = 256
TK_MLA = 512
QK_LEAD = 2
FF_CHUNK = 704

N_Q = SWA_HEADS * HEAD_DIM
N_KV = SWA_KV_HEADS * HEAD_DIM
N_KV3 = 3 * N_KV
N_CMP = 2 * N_KV
N_QM = MLA_HEADS * MLA_QK_DIM
N_KVM = MLA_HEADS * (MLA_NOPE_DIM + MLA_V_DIM)
N_GATE = NSA_HEADS * N_BRANCH
N_TAIL = 64
assert NSA_HEADS * HEAD_DIM == N_Q and NSA_KV_GROUPS == SWA_KV_HEADS
assert MLA_ROPE_DIM + N_GATE <= N_TAIL


def _offsets(*widths):
    return tuple(int(v) for v in np.cumsum((0,) + widths))


R_AQ, R_BQ, R_K, R_V, R_C, R_CQA, R_CKV, R_CKR, N_MIX = _offsets(
    N_Q, N_Q, N_KV3, N_KV3, N_CMP, MLA_Q_RANK, MLA_KV_RANK, N_TAIL)
O_AQ, O_BQ, O_BQR, O_QM, O_VV, O_KV, N_OUT1 = _offsets(N_Q, N_Q, N_Q, N_QM, N_KV3, N_KVM)


def _dot(a, b):
    return jnp.dot(a, b, preferred_element_type=jnp.float32)


def _dot_nt(a, b):
    return lax.dot_general(a, b, (((1,), (1,)), ((), ())), preferred_element_type=jnp.float32)


def _opts(name, *sem):
    return dict(name=name, compiler_params=pltpu.CompilerParams(
        dimension_semantics=sem, vmem_limit_bytes=V7X_VMEM_LIMIT_BYTES))


def _rms_fm(x, g_col):
    ms = jnp.mean(x * x, axis=0, keepdims=True)
    return x * lax.rsqrt(ms + NORM_EPS) * g_col


def _const_spec(shape):
    nd = len(shape)
    return pl.BlockSpec(shape, lambda *_: (0,) * nd, pipeline_mode=pl.Buffered(1))


def _residual_spec(d, tm, token_major):
    if token_major:
        return pl.BlockSpec((1, tm, d), lambda b, i: (b, i, 0))
    return pl.BlockSpec((1, d, tm), lambda b, i: (b, 0, i))


def _load_residual(x_ref, token_major):
    return x_ref[0].T if token_major else x_ref[0]


def _rope_store(o_ref, row0, y, cos, sin, n_heads, head_dim, scale):
    half = head_dim // 2
    for h in range(n_heads):
        x1 = y[h * head_dim:h * head_dim + half]
        x2 = y[h * head_dim + half:(h + 1) * head_dim]
        r = row0 + h * head_dim
        o_ref[0, r:r + half, :] = ((x1 * cos - x2 * sin) * scale).astype(o_ref.dtype)
        o_ref[0, r + half:r + head_dim, :] = ((x2 * cos + x1 * sin) * scale).astype(o_ref.dtype)


def _proj_kernel(x_ref, g_ref, wm_ref, cos_ref, sin_ref, cosm_ref, sinm_ref,
                 qn_ref, wqb_ref, kvn_ref, wkvb_ref, o1_ref, gate_ref, kcv_ref, kk_ref, ks_ref, kc_ref,
                 *, token_major):
    h = _rms_fm(_load_residual(x_ref, token_major), g_ref[...]).astype(CDT)
    cos, sin = cos_ref[0], sin_ref[0]
    cosm, sinm = cosm_ref[0], sinm_ref[0]
    qk_scale = HEAD_DIM ** -0.5 * LOG2E

    lat_q = _rms_fm(_dot(wm_ref[R_CQA:R_CQA + MLA_Q_RANK, :], h), qn_ref[...]).astype(CDT)
    lat_kv = _rms_fm(_dot(wm_ref[R_CKV:R_CKV + MLA_KV_RANK, :], h), kvn_ref[...]).astype(CDT)

    y = _dot(wm_ref[R_AQ:R_AQ + N_Q, :], h)
    _rope_store(o1_ref, O_AQ, y, cos, sin, SWA_HEADS, HEAD_DIM, qk_scale)
    y = _dot(wm_ref[R_BQ:R_BQ + N_Q, :], h)
    o1_ref[0, O_BQ:O_BQ + N_Q, :] = (y * qk_scale).astype(o1_ref.dtype)
    _rope_store(o1_ref, O_BQR, y, cos, sin, NSA_HEADS, HEAD_DIM, qk_scale)
    y = _dot(wm_ref[R_K:R_K + N_KV3, :], h)
    tm = y.shape[1]
    tok = pl.program_id(1) * tm + lax.broadcasted_iota(jnp.int32, (KSEL_DIM - HEAD_DIM, tm), 1)
    blk = lax.shift_right_arithmetic(tok, int(np.log2(NSA_SEL_BLOCK))) & (SEL_BLOCKS_PER_TILE - 1)
    row = lax.broadcasted_iota(jnp.int32, (KSEL_DIM - HEAD_DIM, tm), 0)
    onehot = jnp.where(blk == row, 1.0, 0.0)
    half = HEAD_DIM // 2
    for kh in range(N_KV3 // HEAD_DIM):
        x1 = y[kh * HEAD_DIM:kh * HEAD_DIM + half]
        x2 = y[kh * HEAD_DIM + half:(kh + 1) * HEAD_DIM]
        rows = [x1 * cos - x2 * sin, x2 * cos + x1 * sin]
        if kh in (2, 3):
            ks_ref[0, kh - 2] = jnp.concatenate(rows + [onehot], axis=0).T.astype(ks_ref.dtype)
        else:
            kk_ref[0, kh if kh < 2 else kh - 2] = jnp.concatenate(rows, axis=0).T.astype(kk_ref.dtype)
    y = _dot(wm_ref[R_V:R_V + N_KV3, :], h)
    o1_ref[0, O_VV:O_VV + N_KV3, :] = y.astype(o1_ref.dtype)
    y = _dot(wm_ref[R_C:R_C + N_CMP, :], h)
    for n in range(N_CMP // HEAD_DIM):
        kcv_ref[0, n] = y[n * HEAD_DIM:(n + 1) * HEAD_DIM].T

    y = _dot(wm_ref[R_CKR:R_CKR + N_TAIL, :], h)
    hr = MLA_ROPE_DIM // 2
    x1, x2 = y[0:hr], y[hr:2 * hr]
    k_pe = [x1 * cosm - x2 * sinm, x2 * cosm + x1 * sinm]
    gate_ref[0] = jax.nn.sigmoid(y[MLA_ROPE_DIM:MLA_ROPE_DIM + N_GATE])

    qm = _dot(wqb_ref[...], lat_q)
    m_scale = MLA_QK_DIM ** -0.5 * LOG2E
    for hd in range(MLA_HEADS):
        r = hd * MLA_QK_DIM
        o1_ref[0, O_QM + r:O_QM + r + MLA_NOPE_DIM, :] = (
            qm[r:r + MLA_NOPE_DIM] * m_scale).astype(o1_ref.dtype)
        x1 = qm[r + MLA_NOPE_DIM:r + MLA_NOPE_DIM + hr]
        x2 = qm[r + MLA_NOPE_DIM + hr:r + MLA_QK_DIM]
        o1_ref[0, O_QM + r + MLA_NOPE_DIM:O_QM + r + MLA_NOPE_DIM + hr, :] = (
            (x1 * cosm - x2 * sinm) * m_scale).astype(o1_ref.dtype)
        o1_ref[0, O_QM + r + MLA_NOPE_DIM + hr:O_QM + r + MLA_QK_DIM, :] = (
            (x2 * cosm + x1 * sinm) * m_scale).astype(o1_ref.dtype)
    kv = _dot(wkvb_ref[...], lat_kv)
    o1_ref[0, O_KV:O_KV + N_KVM, :] = kv.astype(o1_ref.dtype)
    for hd in range(MLA_HEADS):
        k_nope = kv[hd * 2 * MLA_NOPE_DIM:hd * 2 * MLA_NOPE_DIM + MLA_NOPE_DIM]
        kc_ref[0, hd] = jnp.concatenate([k_nope] + k_pe, axis=0).T.astype(kc_ref.dtype)


def _proj(x, g_col, wmT, cos, sin, cosm, sinm, qn, wqbT, kvn, wkvbT, tm, token_major):
    B, D, S = (x.shape[0], x.shape[2], x.shape[1]) if token_major else x.shape
    tok = lambda rows: pl.BlockSpec((1, rows, tm), lambda b, i: (b, 0, i))
    return pl.pallas_call(
        functools.partial(_proj_kernel, token_major=token_major),
        grid=(B, S // tm),
        in_specs=[_residual_spec(D, tm, token_major), _const_spec((D, 1)), _const_spec(wmT.shape),
                  tok(HEAD_DIM // 2), tok(HEAD_DIM // 2), tok(MLA_ROPE_DIM // 2), tok(MLA_ROPE_DIM // 2),
                  _const_spec(qn.shape), _const_spec(wqbT.shape),
                  _const_spec(kvn.shape), _const_spec(wkvbT.shape)],
        out_specs=[tok(N_OUT1), tok(N_GATE),
                   pl.BlockSpec((1, N_CMP // HEAD_DIM, tm, HEAD_DIM), lambda b, i: (b, 0, i, 0)),
                   pl.BlockSpec((1, 2 * SWA_KV_HEADS, tm, HEAD_DIM), lambda b, i: (b, 0, i, 0)),
                   pl.BlockSpec((1, NSA_KV_GROUPS, tm, KSEL_DIM), lambda b, i: (b, 0, i, 0)),
                   pl.BlockSpec((1, MLA_HEADS, tm, MLA_QK_DIM), lambda b, i: (b, 0, i, 0))],
        out_shape=[jax.ShapeDtypeStruct((B, N_OUT1, S), CDT),
                   jax.ShapeDtypeStruct((B, N_GATE, S), jnp.float32),
                   jax.ShapeDtypeStruct((B, N_CMP // HEAD_DIM, S, HEAD_DIM), jnp.float32),
                   jax.ShapeDtypeStruct((B, 2 * SWA_KV_HEADS, S, HEAD_DIM), CDT),
                   jax.ShapeDtypeStruct((B, NSA_KV_GROUPS, S, KSEL_DIM), CDT),
                   jax.ShapeDtypeStruct((B, MLA_HEADS, S, MLA_QK_DIM), CDT)],
        **_opts("mixer_proj", "parallel", "parallel"),
    )(x, g_col, wmT, cos, sin, cosm, sinm, qn, wqbT, kvn, wkvbT)


def _compress_kernel(kr_ref, pe_ref, w1_ref, w2_ref, w2t_ref, tok_ref, fm_ref, hid_ref):
    kr = kr_ref[0, 0]
    nc, half = kr.shape
    nj = nc // 4
    a = _dot((kr + pe_ref[0, 0:1, :]).astype(CDT), w1_ref[0, 0:half, :])
    bm = _dot((kr + pe_ref[0, 1:2, :]).astype(CDT), w1_ref[0, half:2 * half, :])
    nxt = pltpu.roll(bm, nc - 1, 0)
    row = lax.broadcasted_iota(jnp.int32, (nc, 1), 0)
    pre = a + jnp.where(row == nc - 1, 0.0, nxt)
    hid_ref[...] = pre * jax.nn.sigmoid(pre)
    hid = jnp.concatenate([hid_ref[pl.ds(r, nj, stride=4), :] for r in range(4)], axis=0).astype(CDT)
    tok_ref[0, 0] = _dot(hid, w2_ref[0]).astype(tok_ref.dtype)
    fm_ref[0, 0] = _dot_nt(w2t_ref[0], hid).astype(fm_ref.dtype)


def _compress(kr, pe, w1, w2, w2t):
    B, _, NC, W = kr.shape
    return pl.pallas_call(
        _compress_kernel,
        grid=(B, 4),
        in_specs=[pl.BlockSpec((1, 1, NC, W), lambda b, n: (b, n, 0, 0)),
                  pl.BlockSpec((1, 2, W), lambda b, n: (n // 2, 0, 0)),
                  pl.BlockSpec((1, 2 * W, LANES), lambda b, n: (n // 2, 0, 0)),
                  pl.BlockSpec((1, LANES, HEAD_DIM), lambda b, n: (n // 2, 0, 0)),
                  pl.BlockSpec((1, HEAD_DIM, LANES), lambda b, n: (n // 2, 0, 0))],
        out_specs=[pl.BlockSpec((1, 1, NC, HEAD_DIM), lambda b, n: (b, n, 0, 0)),
                   pl.BlockSpec((1, 1, HEAD_DIM, NC), lambda b, n: (b, n, 0, 0))],
        out_shape=[jax.ShapeDtypeStruct((B, 4, NC, HEAD_DIM), CDT),
                   jax.ShapeDtypeStruct((B, 4, HEAD_DIM, NC), CDT)],
        scratch_shapes=[pltpu.VMEM((NC, LANES), jnp.float32)],
        **_opts("nsa_compress", "parallel", "parallel"),
    )(kr, pe, w1, w2, w2t)


def _group_queries(q_ref, g):
    return jnp.concatenate([q_ref[0, (g * GROUP_HEADS + r) * HEAD_DIM:(g * GROUP_HEADS + r + 1) * HEAD_DIM, :]
                            for r in range(GROUP_HEADS)], axis=1)


def _with_ones(vt):
    return jnp.concatenate([vt, jnp.ones((ONES_ROWS, vt.shape[1]), vt.dtype)], axis=0)


def _softmax_probs(sc, m_run):
    m_new = jnp.maximum(m_run, jnp.max(sc, axis=0, keepdims=True))
    return m_new, jnp.exp2(sc - m_new).astype(CDT), jnp.exp2(m_run - m_new)


def _softmax_step(sc, m_run, acc, vt_aug):
    m_new, p, alpha = _softmax_probs(sc, m_run)
    return m_new, alpha * acc + _dot(vt_aug, p)


def _window_scores(q4, t4, k, k0, window):
    span = k.shape[0]
    s = _dot(k, q4)
    kpos = k0 + lax.broadcasted_iota(jnp.int32, (span, 1), 0)
    return jnp.where(kpos <= t4, jnp.where(kpos > t4 - window, s, -jnp.inf), -jnp.inf)


def _window_output(s, vt, sink4=None):
    m = jnp.max(s, axis=0, keepdims=True)
    if sink4 is not None:
        m = jnp.maximum(m, sink4)
    pv = _dot(_with_ones(vt), jnp.exp2(s - m).astype(CDT))
    den = pv[HEAD_DIM:HEAD_DIM + 1]
    if sink4 is not None:
        den = den + jnp.exp2(sink4 - m)
    return pv[0:HEAD_DIM] * (1.0 / den)


def _swa_scores(q_ref, k_ref, qs, tq):
    t = qs + lax.broadcasted_iota(jnp.int32, (1, tq), 1)
    t4 = jnp.concatenate([t] * GROUP_HEADS, axis=1)
    span = SWA_WINDOW + tq
    k0 = pl.multiple_of(jnp.maximum(qs - SWA_WINDOW, 0), LANES)
    return [_window_scores(_group_queries(q_ref, g), t4, k_ref[0, g, pl.ds(k0, span), :],
                           k0, SWA_WINDOW) for g in range(SWA_KV_HEADS)], k0


def _swa_outputs(scores, k0, sink_ref, vt_ref, o_ref, tq):
    span = SWA_WINDOW + tq
    for g in range(SWA_KV_HEADS):
        sink4 = jnp.concatenate([jnp.full((1, tq), sink_ref[g * GROUP_HEADS + r] * LOG2E, jnp.float32)
                                 for r in range(GROUP_HEADS)], axis=1)
        o4 = _window_output(scores[g], vt_ref[0, g * HEAD_DIM:(g + 1) * HEAD_DIM, pl.ds(k0, span)], sink4)
        for r in range(GROUP_HEADS):
            row = (g * GROUP_HEADS + r) * HEAD_DIM
            o_ref[0, row:row + HEAD_DIM, :] = o4[:, r * tq:(r + 1) * tq].astype(o_ref.dtype)


def _nsa_kernel(q_ref, qr_ref, cend_ref, kc_ref, vct_ref, ks_ref, vst_ref, kw_ref, vwt_ref,
                gate_ref, o_ref, bias_ref, s_a, s_b, *, tq, n_sel):
    qs = pl.program_id(1) * tq
    nc = kc_ref.shape[2]
    nj = nc // 4
    G = NSA_KV_GROUPS
    t = qs + lax.broadcasted_iota(jnp.int32, (1, tq), 1)
    hpc = SEL_CHAIN_LANES // tq
    cpg = GROUP_HEADS // hpc
    tc = jnp.concatenate([t] * hpc, axis=1)
    t4 = jnp.concatenate([t] * GROUP_HEADS, axis=1)
    jrow = lax.broadcasted_iota(jnp.int32, (nj, tq), 0)
    jf = jrow.astype(jnp.float32)
    cur = lax.shift_right_arithmetic(t, int(np.log2(NSA_SEL_BLOCK)))
    bonus = jnp.where((jrow == 0) | (jrow == cur) | (jrow == cur - 1), NSA_FORCE_BONUS, 0.0)
    span = NSA_WINDOW + tq
    kw0 = pl.multiple_of(jnp.maximum(qs - NSA_WINDOW, 0), LANES)

    qr4 = [_group_queries(qr_ref, g) for g in range(G)]
    s_cmp = [_dot(kc_ref[0, g], _group_queries(q_ref, g)) for g in range(G)]
    s_win = [_window_scores(qr4[g], t4, kw_ref[0, g, pl.ds(kw0, span), :], kw0, NSA_WINDOW)
             for g in range(G)]
    p_cmp = []
    for g in range(G):
        s = jnp.where(cend_ref[...] <= t4, s_cmp[g], -jnp.inf)
        m = jnp.max(s, axis=0, keepdims=True)
        m = jnp.where(m == -jnp.inf, 0.0, m)
        e = jnp.exp2(s - m)
        den = jnp.sum(e, axis=0, keepdims=True)
        p_cmp.append(e * (1.0 / jnp.where(den > 0, den, 1.0)))
    o_cmps = [_dot(vct_ref[0, g], p_cmp[g].astype(CDT)) for g in range(G)]
    o_wins = [_window_output(s_win[g], vwt_ref[0, g * HEAD_DIM:(g + 1) * HEAD_DIM, pl.ds(kw0, span)])
              for g in range(G)]

    fixed = []
    for g in range(G):
        p, o_cmp, o_win = p_cmp[g], o_cmps[g], o_wins[g]
        psum = p[:, 0:tq]
        for r in range(1, GROUP_HEADS):
            psum = psum + p[:, r * tq:(r + 1) * tq]
        p3 = psum[3 * nj:4 * nj]
        prev = jnp.where(jrow == 0, 0.0, pltpu.roll(p3, 1, 0))
        imp = prev + 2.0 * (psum[0:nj] + psum[nj:2 * nj] + psum[2 * nj:3 * nj]) + p3

        score = jnp.where(jrow <= cur, imp + bonus, -jnp.inf)
        for _ in range(n_sel):
            mx = jnp.max(score, axis=0, keepdims=True)
            idx = jnp.min(jnp.where(score == mx, jf, float(nj)), axis=0, keepdims=True)
            idx = jnp.where(mx > -jnp.inf, idx, float(nj))
            score = jnp.where(jf == idx, -jnp.inf, score)
        bias_ref[g] = jnp.where(jrow <= cur, jnp.where(score == -jnp.inf, 0.0, MASK_VALUE), MASK_VALUE)

        parts = []
        for r in range(GROUP_HEADS):
            sl = slice(r * tq, (r + 1) * tq)
            g0 = gate_ref[0, g, 3 * r:3 * r + 1, :]
            g2 = gate_ref[0, g, 3 * r + 2:3 * r + 3, :]
            parts.append((g0 * o_cmp[:, sl], g2 * o_win[:, sl]))
        fixed.append(parts)

    def augmented_queries(kt):
        out = []
        for g in range(G):
            b8 = bias_ref[g, pl.ds(kt * SEL_BLOCKS_PER_TILE, SEL_BLOCKS_PER_TILE), :]
            b16 = jnp.concatenate([b8, jnp.zeros_like(b8)], axis=0)
            out.append(jnp.concatenate(
                [qr4[g], jnp.concatenate([b16] * GROUP_HEADS, axis=1).astype(CDT)], axis=0))
        return out

    def score_chain(kt, s_ref, qa, c):
        g, hp = divmod(c, cpg)
        k0 = pl.multiple_of(kt * TK_SEL, TK_SEL)
        s_ref[c] = _dot(ks_ref[0, g, pl.ds(k0, TK_SEL), :],
                        qa[g][:, hp * SEL_CHAIN_LANES:(hp + 1) * SEL_CHAIN_LANES])

    def consume_chain(kt, s_ref, c, st, diagonal):
        g = c // cpg
        k0 = pl.multiple_of(kt * TK_SEL, TK_SEL)
        sc = s_ref[c]
        if diagonal:
            kpos = k0 + lax.broadcasted_iota(jnp.int32, (TK_SEL, 1), 0)
            sc = jnp.where(kpos <= tc, sc, MASK_VALUE)
        vt_aug = _with_ones(vst_ref[0, g * HEAD_DIM:(g + 1) * HEAD_DIM, pl.ds(k0, TK_SEL)])
        return _softmax_step(sc, st[0], st[1], vt_aug)

    def step(kt, s_cur, s_nxt, state, diagonal=False):
        n_chain = cpg * G
        if s_nxt is not None:
            qa = augmented_queries(kt + 1)
            for c in range(QK_LEAD):
                score_chain(kt + 1, s_nxt, qa, c)
        out = []
        for c in range(n_chain):
            out.append(consume_chain(kt, s_cur, c, state[c], diagonal))
            if s_nxt is not None and c + QK_LEAD < n_chain:
                score_chain(kt + 1, s_nxt, qa, c + QK_LEAD)
        return tuple(out)

    def finish(state):
        for g in range(G):
            for r in range(GROUP_HEADS):
                _, acc = state[cpg * g + r // hpc]
                sl = slice((r % hpc) * tq, (r % hpc + 1) * tq)
                o_sel = acc[0:HEAD_DIM, sl] * (1.0 / acc[HEAD_DIM:HEAD_DIM + 1, sl])
                g1 = gate_ref[0, g, 3 * r + 1:3 * r + 2, :]
                c_part, w_part = fixed[g][r]
                row = (g * GROUP_HEADS + r) * HEAD_DIM
                o_ref[0, row:row + HEAD_DIM, :] = (c_part + g1 * o_sel + w_part).astype(o_ref.dtype)

    def pair(j, state):
        return step(2 * j + 1, s_b, s_a, step(2 * j, s_a, s_b, state))

    init = tuple((jnp.full((1, SEL_CHAIN_LANES), MASK_VALUE, jnp.float32),
                  jnp.zeros((HEAD_DIM + ONES_ROWS, SEL_CHAIN_LANES), jnp.float32))
                 for _ in range(cpg * G))
    n_full = qs // TK_SEL
    qa0 = augmented_queries(0)
    for c in range(cpg * G):
        score_chain(0, s_a, qa0, c)
    state = lax.fori_loop(0, n_full // 2, pair, init)
    odd = n_full % 2 == 1

    @pl.when(odd)
    def _():
        finish(step(n_full, s_b, None, step(n_full - 1, s_a, s_b, state), diagonal=True))

    @pl.when(jnp.logical_not(odd))
    def _():
        finish(step(n_full, s_a, None, state, diagonal=True))


def _nsa(o1, cend, cmp_tok, cmp_fm, ks_aug, kk, gates, tq):
    B, _, S = o1.shape
    nc = cmp_tok.shape[2]
    nj = nc // 4
    n_sel = min(NSA_N_SEL, nj)
    G = NSA_KV_GROUPS
    rows = NSA_HEADS * HEAD_DIM
    vrows = G * HEAD_DIM
    assert O_BQ % rows == 0 and O_BQR % rows == 0 and O_VV % vrows == 0
    assert SEL_CHAIN_LANES % tq == 0 and tq <= TK_SEL
    return pl.pallas_call(
        functools.partial(_nsa_kernel, tq=tq, n_sel=n_sel),
        grid=(B, S // tq),
        in_specs=[pl.BlockSpec((1, rows, tq), lambda b, i: (b, O_BQ // rows, i)),
                  pl.BlockSpec((1, rows, tq), lambda b, i: (b, O_BQR // rows, i)),
                  _const_spec((nc, 1)),
                  pl.BlockSpec((1, G, nc, HEAD_DIM), lambda b, i: (b, 0, 0, 0)),
                  pl.BlockSpec((1, G, HEAD_DIM, nc), lambda b, i: (b, 1, 0, 0)),
                  pl.BlockSpec((1, G, S, KSEL_DIM), lambda b, i: (b, 0, 0, 0)),
                  pl.BlockSpec((1, vrows, S), lambda b, i: (b, O_VV // vrows + 1, 0)),
                  pl.BlockSpec((1, G, S, HEAD_DIM), lambda b, i: (b, 1, 0, 0)),
                  pl.BlockSpec((1, vrows, S), lambda b, i: (b, O_VV // vrows + 2, 0)),
                  pl.BlockSpec((1, G, 3 * GROUP_HEADS, tq), lambda b, i: (b, 0, 0, i))],
        out_specs=pl.BlockSpec((1, rows, tq), lambda b, i: (b, 0, i)),
        out_shape=jax.ShapeDtypeStruct((B, rows, S), CDT),
        scratch_shapes=[pltpu.VMEM((G, nj, tq), jnp.float32)]
                       + [pltpu.VMEM((NSA_HEADS * tq // SEL_CHAIN_LANES, TK_SEL, SEL_CHAIN_LANES),
                                     jnp.float32)] * 2,
        **_opts("nsa_attn", "parallel", "arbitrary"),
    )(o1, o1, cend, cmp_tok, cmp_fm, ks_aug, o1, kk, o1, gates)


def _mla_swa_kernel(sink_ref, q_ref, k_ref, *rest, tq, hps):
    vt_refs = rest[:hps]
    aq_ref, ak_ref, avt_ref, o_ref, oa_ref, s_a, s_b = rest[hps:]
    qs = pl.program_id(1) * tq
    q = [q_ref[0, h * MLA_QK_DIM:(h + 1) * MLA_QK_DIM, :] for h in range(hps)]
    t = qs + lax.broadcasted_iota(jnp.int32, (1, tq), 1)

    def score_chain(kt, s_ref, h):
        k0 = pl.multiple_of(kt * TK_MLA, TK_MLA)
        s_ref[h] = _dot(k_ref[0, h, pl.ds(k0, TK_MLA), :], q[h])

    def consume_chain(kt, s_ref, h, st, diagonal):
        k0 = pl.multiple_of(kt * TK_MLA, TK_MLA)
        sc = s_ref[h]
        if diagonal:
            kpos = k0 + lax.broadcasted_iota(jnp.int32, (TK_MLA, 1), 0)
            sc = jnp.where(kpos <= t, sc, MASK_VALUE)
        return _softmax_step(sc, st[0], st[1], _with_ones(vt_refs[h][0, :, pl.ds(k0, TK_MLA)]))

    def step(kt, s_cur, s_nxt, state, diagonal=False):
        lead = QK_LEAD
        if s_nxt is not None:
            for h in range(lead):
                score_chain(kt + 1, s_nxt, h)
        out = []
        for h in range(hps):
            out.append(consume_chain(kt, s_cur, h, state[h], diagonal))
            if s_nxt is not None and h + lead < hps:
                score_chain(kt + 1, s_nxt, h + lead)
        return tuple(out)

    def finish(state):
        for h in range(hps):
            _, acc = state[h]
            o_ref[0, h * MLA_V_DIM:(h + 1) * MLA_V_DIM, :] = (
                acc[0:MLA_V_DIM] * (1.0 / acc[MLA_V_DIM:MLA_V_DIM + 1])).astype(o_ref.dtype)

    def pair(j, state):
        return step(2 * j + 1, s_b, s_a, step(2 * j, s_a, s_b, state))

    init = tuple((jnp.full((1, tq), MASK_VALUE, jnp.float32),
                  jnp.zeros((MLA_V_DIM + ONES_ROWS, tq), jnp.float32)) for _ in range(hps))
    n_full = qs // TK_MLA
    swa_scores, swa_k0 = _swa_scores(aq_ref, ak_ref, qs, tq)
    for h in range(hps):
        score_chain(0, s_a, h)
    _swa_outputs(swa_scores, swa_k0, sink_ref, avt_ref, oa_ref, tq)
    state = lax.fori_loop(0, n_full // 2, pair, init)
    odd = n_full % 2 == 1

    @pl.when(odd)
    def _():
        finish(step(n_full, s_b, None, step(n_full - 1, s_a, s_b, state), diagonal=True))

    @pl.when(jnp.logical_not(odd))
    def _():
        finish(step(n_full, s_a, None, state, diagonal=True))


def _mla_swa(sinks, o1, kcat, kk, tq):
    B, _, S = o1.shape
    hps = MLA_HEADS
    assert O_QM % N_QM == 0 and O_KV % MLA_V_DIM == 0 and O_AQ % N_Q == 0 and O_VV % N_KV == 0
    once = dict(pipeline_mode=pl.Buffered(1))
    v_spec = lambda h: pl.BlockSpec(
        (1, MLA_V_DIM, S), lambda b, i: (b, O_KV // MLA_V_DIM + 2 * h + 1, 0), **once)
    return pl.pallas_call(
        functools.partial(_mla_swa_kernel, tq=tq, hps=hps),
        grid=(B, S // tq),
        in_specs=[pl.BlockSpec(memory_space=pltpu.SMEM),
                  pl.BlockSpec((1, N_QM, tq), lambda b, i: (b, O_QM // N_QM, i)),
                  pl.BlockSpec((1, hps, S, MLA_QK_DIM), lambda b, i: (b, 0, 0, 0), **once)]
                 + [v_spec(h) for h in range(hps)]
                 + [pl.BlockSpec((1, N_Q, tq), lambda b, i: (b, O_AQ // N_Q, i)),
                    pl.BlockSpec((1, SWA_KV_HEADS, S, HEAD_DIM), lambda b, i: (b, 0, 0, 0), **once),
                    pl.BlockSpec((1, N_KV, S), lambda b, i: (b, O_VV // N_KV, 0), **once)],
        out_specs=[pl.BlockSpec((1, MLA_HEADS * MLA_V_DIM, tq), lambda b, i: (b, 0, i)),
                   pl.BlockSpec((1, N_Q, tq), lambda b, i: (b, 0, i))],
        out_shape=[jax.ShapeDtypeStruct((B, MLA_HEADS * MLA_V_DIM, S), CDT),
                   jax.ShapeDtypeStruct((B, N_Q, S), CDT)],
        scratch_shapes=[pltpu.VMEM((hps, TK_MLA, tq), jnp.float32)] * 2,
        **_opts("mla_swa_attn", "parallel", "parallel"),
    )(sinks, o1, kcat, *([o1] * hps), o1, kk, o1)


def _merge_kernel(x_ref, g_ref, wg_ref, oa_ref, ob_ref, oc_ref, wa_ref, wb_ref, wc_ref, wo_ref, o_ref,
                  *, token_major):
    x = _load_residual(x_ref, token_major)
    d = x.shape[0]
    h = _rms_fm(x, g_ref[...]).astype(CDT)
    merged = None
    for i, (br_ref, w_ref) in enumerate(((oa_ref, wa_ref), (ob_ref, wb_ref), (oc_ref, wc_ref))):
        gate = jax.nn.sigmoid(_dot(wg_ref[i * d:(i + 1) * d, :], h))
        term = gate * _dot(w_ref[...], br_ref[0])
        merged = term if merged is None else merged + term
    o_ref[0] = x + _dot(wo_ref[...], merged.astype(CDT))


def _merge(x, g_col, wgT, oa, ob, oc, waT, wbT, wcT, woT, tm, token_major):
    B, D, S = (x.shape[0], x.shape[2], x.shape[1]) if token_major else x.shape
    tok = lambda rows: pl.BlockSpec((1, rows, tm), lambda b, i: (b, 0, i))
    return pl.pallas_call(
        functools.partial(_merge_kernel, token_major=token_major),
        grid=(B, S // tm),
        in_specs=[_residual_spec(D, tm, token_major), _const_spec((D, 1)), _const_spec(wgT.shape),
                  tok(oa.shape[1]), tok(ob.shape[1]), tok(oc.shape[1]),
                  _const_spec(waT.shape), _const_spec(wbT.shape), _const_spec(wcT.shape),
                  _const_spec(woT.shape)],
        out_specs=tok(D),
        out_shape=jax.ShapeDtypeStruct((B, D, S), x.dtype),
        **_opts("merge_out", "parallel", "parallel"),
    )(x, g_col, wgT, oa, ob, oc, waT, wbT, wcT, woT)


def _mem_kv_kernel(mem_ref, g_ref, wk_ref, wvt_ref, k_ref, vt_ref):
    m = mem_ref[0]
    ms = jnp.mean(m * m, axis=-1, keepdims=True)
    hm = (m * lax.rsqrt(ms + NORM_EPS) * g_ref[...]).astype(CDT)
    k_ref[0] = _dot(hm, wk_ref[...]).astype(k_ref.dtype)
    vt_ref[0] = _dot_nt(wvt_ref[...], hm).astype(vt_ref.dtype)


def _mem_kv(mem, g_row, wk, wvT):
    B, M, D = mem.shape
    n = wk.shape[1]
    return pl.pallas_call(
        _mem_kv_kernel,
        grid=(B,),
        in_specs=[pl.BlockSpec((1, M, D), lambda b: (b, 0, 0)), _const_spec((1, D)),
                  _const_spec(wk.shape), _const_spec(wvT.shape)],
        out_specs=[pl.BlockSpec((1, M, n), lambda b: (b, 0, 0)),
                   pl.BlockSpec((1, n, M), lambda b: (b, 0, 0))],
        out_shape=[jax.ShapeDtypeStruct((B, M, n), CDT), jax.ShapeDtypeStruct((B, n, M), CDT)],
        **_opts("mem_kv", "parallel"),
    )(mem, g_row, wk, wvT)


def _xattn_kernel(x_ref, g_ref, wq_ref, k_ref, vt_ref, wo_ref, o_ref):
    x = x_ref[0]
    h = _rms_fm(x, g_ref[...]).astype(CDT)
    q = (_dot(wq_ref[...], h) * (XATTN_HEAD_DIM ** -0.5 * LOG2E)).astype(CDT)
    head_rows = [slice(hd * XATTN_HEAD_DIM, (hd + 1) * XATTN_HEAD_DIM) for hd in range(XATTN_HEADS)]
    scores = [_dot(k_ref[0, :, rows], q[rows]) for rows in head_rows]
    probs = [jnp.exp2(s - jnp.max(s, axis=0, keepdims=True)).astype(CDT) for s in scores]
    outs = []
    for rows, e in zip(head_rows, probs):
        pv = _dot(_with_ones(vt_ref[0, rows, :]), e)
        outs.append((pv[0:XATTN_HEAD_DIM] * (1.0 / pv[XATTN_HEAD_DIM:XATTN_HEAD_DIM + 1])).astype(CDT))
    o_ref[0] = x + _dot(wo_ref[...], jnp.concatenate(outs, axis=0))


def _xattn(xT, g_col, wqT, kmem, vmemT, woT, tm):
    B, D, S = xT.shape
    M, n = kmem.shape[1], kmem.shape[2]
    tok = lambda rows: pl.BlockSpec((1, rows, tm), lambda b, i: (b, 0, i))
    return pl.pallas_call(
        _xattn_kernel,
        grid=(B, S // tm),
        in_specs=[tok(D), _const_spec((D, 1)), _const_spec(wqT.shape),
                  pl.BlockSpec((1, M, n), lambda b, i: (b, 0, 0)),
                  pl.BlockSpec((1, n, M), lambda b, i: (b, 0, 0)),
                  _const_spec(woT.shape)],
        out_specs=tok(D),
        out_shape=jax.ShapeDtypeStruct((B, D, S), xT.dtype),
        **_opts("xattn", "parallel", "parallel"),
    )(xT, g_col, wqT, kmem, vmemT, woT)


def _ffn_kernel(x_ref, g_ref, wgu_ref, wd_ref, *rest, d_ff, final_norm):
    o_ref = rest[-1]
    x = x_ref[0]
    h = _rms_fm(x, g_ref[...]).astype(CDT)
    acc = x
    for c in range(d_ff // FF_CHUNK):
        r = c * FF_CHUNK
        gate = _dot(wgu_ref[r:r + FF_CHUNK, :], h)
        up = _dot(wgu_ref[d_ff + r:d_ff + r + FF_CHUNK, :], h)
        act = (gate * jax.nn.sigmoid(gate) * up).astype(CDT)
        acc = acc + _dot(wd_ref[:, r:r + FF_CHUNK], act)
    if final_norm:
        o_ref[0] = _rms_fm(acc, rest[0][...]).T
    else:
        o_ref[0] = acc


def _ffn(xT, g_col, wguT, wdT, tm, final_g_col=None):
    B, D, S = xT.shape
    d_ff = wdT.shape[1]
    final_norm = final_g_col is not None
    tok = lambda rows: pl.BlockSpec((1, rows, tm), lambda b, i: (b, 0, i))
    extra = [final_g_col] if final_norm else []
    return pl.pallas_call(
        functools.partial(_ffn_kernel, d_ff=d_ff, final_norm=final_norm),
        grid=(B, S // tm),
        in_specs=[tok(D), _const_spec((D, 1)), _const_spec(wguT.shape), _const_spec(wdT.shape)]
                 + [_const_spec((D, 1))] * len(extra),
        out_specs=_residual_spec(D, tm, final_norm),
        out_shape=jax.ShapeDtypeStruct((B, S, D) if final_norm else (B, D, S), xT.dtype),
        **_opts("ffn", "parallel", "parallel"),
    )(xT, g_col, wguT, wdT, *extra)


def _rope_tables(positions, dim):
    half = dim // 2
    inv_freq = ROPE_THETA ** (-jnp.arange(half, dtype=jnp.float32) / half)
    ang = positions.astype(jnp.float32)[:, None, :] * inv_freq[None, :, None]
    return jnp.cos(ang), jnp.sin(ang)


def _pack_mixer_weight(w_in):
    off = _offsets(N_Q, N_KV, N_KV, N_Q, N_KV, N_KV, N_KV, N_KV, N_KV, N_KV, N_GATE,
                   MLA_Q_RANK, MLA_KV_RANK, MLA_ROPE_DIM)
    a_q, a_k, a_v, b_q, b_kc, b_vc, b_ks, b_vs, b_kw, b_vw, b_g, c_qa, c_kv, c_kr = [
        w_in[:, off[i]:off[i + 1]] for i in range(14)]
    packed = jnp.concatenate([a_q, b_q, a_k, b_ks, b_kw, a_v, b_vs, b_vw, b_kc, b_vc,
                              c_qa, c_kv, c_kr, b_g], axis=1)
    packed = jnp.pad(packed, ((0, 0), (0, N_MIX - packed.shape[1])))
    return packed.T.astype(CDT), w_in[:, off[14]:].T.astype(CDT)


def kernel(x, mem, positions, norm_mix, w_in, swa_sinks, nsa_pe_k, nsa_pe_v, nsa_wk1, nsa_wk2,
           nsa_wv1, nsa_wv2, mla_q_norm, mla_w_q_b, mla_kv_norm, mla_w_kv_b, w_br_a, w_br_b,
           w_br_c, w_out, norm_xattn, norm_mem, w_xq, w_xkv, w_xo, norm_ffn, w_gate_up, w_down,
           norm_final):
    B, S, D = x.shape
    depth = w_in.shape[0]
    nj = S // NSA_SEL_BLOCK
    nc = 4 * nj
    assert S % TK_SEL == 0 and S % TM_PROJ == 0 and S >= NSA_WINDOW + TQ_NSA
    col = lambda v: v.reshape(-1, 1)
    wt = lambda w: w.T.astype(CDT)

    cos, sin = _rope_tables(positions, HEAD_DIM)
    cosm, sinm = _rope_tables(positions, MLA_ROPE_DIM)
    rr, jj = np.divmod(np.arange(nc), nj)
    cend = jnp.asarray(((4 * jj + rr) * NSA_CMP_STRIDE + NSA_CMP_BLOCK - 1).reshape(nc, 1), jnp.int32)

    res = x
    for l in range(depth):
        token_major = l == 0
        wmT, wgT = _pack_mixer_weight(w_in[l])
        o1, gates, kcv, kk, ks_aug, kcat = _proj(
            res, col(norm_mix[l]), wmT, cos, sin, cosm, sinm,
            col(mla_q_norm[l]), wt(mla_w_q_b[l]), col(mla_kv_norm[l]), wt(mla_w_kv_b[l]), TM_PROJ,
            token_major)
        kr = kcv.reshape(B, N_CMP // HEAD_DIM, nc, NSA_CMP_STRIDE * HEAD_DIM)
        gates = gates.reshape(B, NSA_KV_GROUPS, 3 * GROUP_HEADS, S)

        pe = jnp.stack([nsa_pe_k[l], nsa_pe_v[l]]).reshape(2, 2, NSA_CMP_STRIDE * HEAD_DIM)
        hpad = LANES - HEAD_DIM
        w1 = jnp.pad(jnp.stack([nsa_wk1[l], nsa_wv1[l]]), ((0, 0), (0, 0), (0, hpad))).astype(CDT)
        w2 = jnp.pad(jnp.stack([nsa_wk2[l], nsa_wv2[l]]), ((0, 0), (0, hpad), (0, 0))).astype(CDT)
        w2t = jnp.swapaxes(w2, 1, 2)
        cmp_tok, cmp_fm = _compress(kr, pe, w1, w2, w2t)

        o_b = _nsa(o1, cend, cmp_tok, cmp_fm, ks_aug, kk, gates, TQ_NSA)
        o_c, o_a = _mla_swa(swa_sinks[l], o1, kcat, kk, TQ_MLA)
        xT = _merge(res, col(norm_mix[l]), wgT, o_a, o_b, o_c,
                    wt(w_br_a[l]), wt(w_br_b[l]), wt(w_br_c[l]), wt(w_out[l]), TM_PROJ, token_major)

        n_kv = XATTN_HEADS * XATTN_HEAD_DIM
        kmem, vmemT = _mem_kv(mem, norm_mem[l].reshape(1, D), w_xkv[l][:, :n_kv].astype(CDT),
                              wt(w_xkv[l][:, n_kv:]))
        xT = _xattn(xT, col(norm_xattn[l]), wt(w_xq[l]), kmem, vmemT, wt(w_xo[l]), TM_PROJ)
        res = _ffn(xT, col(norm_ffn[l]), wt(w_gate_up[l]), wt(w_down[l]), TM_PROJ,
                   col(norm_final) if l == depth - 1 else None)
    return res
```

```python
import functools

import numpy as np
import jax
import jax.numpy as jnp
from jax import lax
from jax.experimental import pallas as pl
from jax.experimental.pallas import tpu as pltpu

HEAD_DIM = 64
ROPE_THETA = 10000.0
NORM_EPS = 1e-6
SWA_HEADS = 8
SWA_KV_HEADS = 2
SWA_WINDOW = 128
NSA_HEADS = 8
NSA_KV_GROUPS = 2
NSA_CMP_BLOCK = 32
NSA_CMP_STRIDE = 16
NSA_SEL_BLOCK = 64
NSA_N_SEL = 16
NSA_WINDOW = 512
NSA_FORCE_BONUS = 1e4
MLA_HEADS = 8
MLA_Q_RANK = 384
MLA_KV_RANK = 256
MLA_NOPE_DIM = 64
MLA_ROPE_DIM = 32
MLA_V_DIM = 64
MLA_QK_DIM = MLA_NOPE_DIM + MLA_ROPE_DIM
XATTN_HEADS = 4
XATTN_HEAD_DIM = 128
N_BRANCH = 3
GROUP_HEADS = 4
GROUP_ROWS = GROUP_HEADS * HEAD_DIM
LOG2E = 1.4426950408889634

V7X_VMEM_LIMIT_BYTES = 56 * 1024 * 1024
LANES = 128
ONES_ROWS = 16

CDT = jnp.bfloat16

MASK_VALUE = -1e30

TM_PROJ = 512
TQ_NSA = 256
TK_SEL = 512
SEL_BLOCKS_PER_TILE = TK_SEL // NSA_SEL_BLOCK
SEL_CHAIN_LANES = 256
KSEL_DIM = HEAD_DIM + 16
TQ_MLA = 256
TK_MLA = 512
QK_LEAD = 2
FF_CHUNK = 704

N_Q = SWA_HEADS * HEAD_DIM
N_KV = SWA_KV_HEADS * HEAD_DIM
N_KV3 = 3 * N_KV
N_CMP = 2 * N_KV
N_QM = MLA_HEADS * MLA_QK_DIM
N_KVM = MLA_HEADS * (MLA_NOPE_DIM + MLA_V_DIM)
N_GATE = NSA_HEADS * N_BRANCH
N_TAIL = 64
assert NSA_HEADS * HEAD_DIM == N_Q and NSA_KV_GROUPS == SWA_KV_HEADS
assert MLA_ROPE_DIM + N_GATE <= N_TAIL


def _offsets(*widths):
    return tuple(int(v) for v in np.cumsum((0,) + widths))


R_AQ, R_BQ, R_K, R_V, R_C, R_CQA, R_CKV, R_CKR, N_MIX = _offsets(
    N_Q, N_Q, N_KV3, N_KV3, N_CMP, MLA_Q_RANK, MLA_KV_RANK, N_TAIL)
O_AQ, O_BQ, O_BQR, O_QM, O_VV, O_KV, N_OUT1 = _offsets(N_Q, N_Q, N_Q, N_QM, N_KV3, N_KVM)


def _dot(a, b):
    return jnp.dot(a, b, preferred_element_type=jnp.float32)


def _dot_nt(a, b):
    return lax.dot_general(a, b, (((1,), (1,)), ((), ())), preferred_element_type=jnp.float32)


def _opts(name, *sem):
    return dict(name=name, compiler_params=pltpu.CompilerParams(
        dimension_semantics=sem, vmem_limit_bytes=V7X_VMEM_LIMIT_BYTES))


def _rms_fm(x, g_col):
    ms = jnp.mean(x * x, axis=0, keepdims=True)
    return x * lax.rsqrt(ms + NORM_EPS) * g_col


def _const_spec(shape):
    nd = len(shape)
    return pl.BlockSpec(shape, lambda *_: (0,) * nd, pipeline_mode=pl.Buffered(1))


def _residual_spec(d, tm, token_major):
    if token_major:
        return pl.BlockSpec((1, tm, d), lambda b, i: (b, i, 0))
    return pl.BlockSpec((1, d, tm), lambda b, i: (b, 0, i))


def _load_residual(x_ref, token_major):
    return x_ref[0].T if token_major else x_ref[0]


def _rope_store(o_ref, row0, y, cos, sin, n_heads, head_dim, scale):
    half = head_dim // 2
    for h in range(n_heads):
        x1 = y[h * head_dim:h * head_dim + half]
        x2 = y[h * head_dim + half:(h + 1) * head_dim]
        r = row0 + h * head_dim
        o_ref[0, r:r + half, :] = ((x1 * cos - x2 * sin) * scale).astype(o_ref.dtype)
        o_ref[0, r + half:r + head_dim, :] = ((x2 * cos + x1 * sin) * scale).astype(o_ref.dtype)


def _proj_kernel(x_ref, g_ref, wm_ref, cos_ref, sin_ref, cosm_ref, sinm_ref,
                 qn_ref, wqb_ref, kvn_ref, wkvb_ref, o1_ref, gate_ref, kcv_ref, kk_ref, ks_ref, kc_ref,
                 *, token_major):
    h = _rms_fm(_load_residual(x_ref, token_major), g_ref[...]).astype(CDT)
    cos, sin = cos_ref[0], sin_ref[0]
    cosm, sinm = cosm_ref[0], sinm_ref[0]
    qk_scale = HEAD_DIM ** -0.5 * LOG2E

    lat_q = _rms_fm(_dot(wm_ref[R_CQA:R_CQA + MLA_Q_RANK, :], h), qn_ref[...]).astype(CDT)
    lat_kv = _rms_fm(_dot(wm_ref[R_CKV:R_CKV + MLA_KV_RANK, :], h), kvn_ref[...]).astype(CDT)

    y = _dot(wm_ref[R_AQ:R_AQ + N_Q, :], h)
    _rope_store(o1_ref, O_AQ, y, cos, sin, SWA_HEADS, HEAD_DIM, qk_scale)
    y = _dot(wm_ref[R_BQ:R_BQ + N_Q, :], h)
    o1_ref[0, O_BQ:O_BQ + N_Q, :] = (y * qk_scale).astype(o1_ref.dtype)
    _rope_store(o1_ref, O_BQR, y, cos, sin, NSA_HEADS, HEAD_DIM, qk_scale)
    y = _dot(wm_ref[R_K:R_K + N_KV3, :], h)
    tm = y.shape[1]
    tok = pl.program_id(1) * tm + lax.broadcasted_iota(jnp.int32, (KSEL_DIM - HEAD_DIM, tm), 1)
    blk = lax.shift_right_arithmetic(tok, int(np.log2(NSA_SEL_BLOCK))) & (SEL_BLOCKS_PER_TILE - 1)
    row = lax.broadcasted_iota(jnp.int32, (KSEL_DIM - HEAD_DIM, tm), 0)
    onehot = jnp.where(blk == row, 1.0, 0.0)
    half = HEAD_DIM // 2
    for kh in range(N_KV3 // HEAD_DIM):
        x1 = y[kh * HEAD_DIM:kh * HEAD_DIM + half]
        x2 = y[kh * HEAD_DIM + half:(kh + 1) * HEAD_DIM]
        rows = [x1 * cos - x2 * sin, x2 * cos + x1 * sin]
        if kh in (2, 3):
            ks_ref[0, kh - 2] = jnp.concatenate(rows + [onehot], axis=0).T.astype(ks_ref.dtype)
        else:
            kk_ref[0, kh if kh < 2 else kh - 2] = jnp.concatenate(rows, axis=0).T.astype(kk_ref.dtype)
    y = _dot(wm_ref[R_V:R_V + N_KV3, :], h)
    o1_ref[0, O_VV:O_VV + N_KV3, :] = y.astype(o1_ref.dtype)
    y = _dot(wm_ref[R_C:R_C + N_CMP, :], h)
    for n in range(N_CMP // HEAD_DIM):
        kcv_ref[0, n] = y[n * HEAD_DIM:(n + 1) * HEAD_DIM].T

    y = _dot(wm_ref[R_CKR:R_CKR + N_TAIL, :], h)
    hr = MLA_ROPE_DIM // 2
    x1, x2 = y[0:hr], y[hr:2 * hr]
    k_pe = [x1 * cosm - x2 * sinm, x2 * cosm + x1 * sinm]
    gate_ref[0] = jax.nn.sigmoid(y[MLA_ROPE_DIM:MLA_ROPE_DIM + N_GATE])

    qm = _dot(wqb_ref[...], lat_q)
    m_scale = MLA_QK_DIM ** -0.5 * LOG2E
    for hd in range(MLA_HEADS):
        r = hd * MLA_QK_DIM
        o1_ref[0, O_QM + r:O_QM + r + MLA_NOPE_DIM, :] = (
            qm[r:r + MLA_NOPE_DIM] * m_scale).astype(o1_ref.dtype)
        x1 = qm[r + MLA_NOPE_DIM:r + MLA_NOPE_DIM + hr]
        x2 = qm[r + MLA_NOPE_DIM + hr:r + MLA_QK_DIM]
        o1_ref[0, O_QM + r + MLA_NOPE_DIM:O_QM + r + MLA_NOPE_DIM + hr, :] = (
            (x1 * cosm - x2 * sinm) * m_scale).astype(o1_ref.dtype)
        o1_ref[0, O_QM + r + MLA_NOPE_DIM + hr:O_QM + r + MLA_QK_DIM, :] = (
            (x2 * cosm + x1 * sinm) * m_scale).astype(o1_ref.dtype)
    kv = _dot(wkvb_ref[...], lat_kv)
    o1_ref[0, O_KV:O_KV + N_KVM, :] = kv.astype(o1_ref.dtype)
    for hd in range(MLA_HEADS):
        k_nope = kv[hd * 2 * MLA_NOPE_DIM:hd * 2 * MLA_NOPE_DIM + MLA_NOPE_DIM]
        kc_ref[0, hd] = jnp.concatenate([k_nope] + k_pe, axis=0).T.astype(kc_ref.dtype)


def _proj(x, g_col, wmT, cos, sin, cosm, sinm, qn, wqbT, kvn, wkvbT, tm, token_major):
    B, D, S = (x.shape[0], x.shape[2], x.shape[1]) if token_major else x.shape
    tok = lambda rows: pl.BlockSpec((1, rows, tm), lambda b, i: (b, 0, i))
    return pl.pallas_call(
        functools.partial(_proj_kernel, token_major=token_major),
        grid=(B, S // tm),
        in_specs=[_residual_spec(D, tm, token_major), _const_spec((D, 1)), _const_spec(wmT.shape),
                  tok(HEAD_DIM // 2), tok(HEAD_DIM // 2), tok(MLA_ROPE_DIM // 2), tok(MLA_ROPE_DIM // 2),
                  _const_spec(qn.shape), _const_spec(wqbT.shape),
                  _const_spec(kvn.shape), _const_spec(wkvbT.shape)],
        out_specs=[tok(N_OUT1), tok(N_GATE),
                   pl.BlockSpec((1, N_CMP // HEAD_DIM, tm, HEAD_DIM), lambda b, i: (b, 0, i, 0)),
                   pl.BlockSpec((1, 2 * SWA_KV_HEADS, tm, HEAD_DIM), lambda b, i: (b, 0, i, 0)),
                   pl.BlockSpec((1, NSA_KV_GROUPS, tm, KSEL_DIM), lambda b, i: (b, 0, i, 0)),
                   pl.BlockSpec((1, MLA_HEADS, tm, MLA_QK_DIM), lambda b, i: (b, 0, i, 0))],
        out_shape=[jax.ShapeDtypeStruct((B, N_OUT1, S), CDT),
                   jax.ShapeDtypeStruct((B, N_GATE, S), jnp.float32),
                   jax.ShapeDtypeStruct((B, N_CMP // HEAD_DIM, S, HEAD_DIM), jnp.float32),
                   jax.ShapeDtypeStruct((B, 2 * SWA_KV_HEADS, S, HEAD_DIM), CDT),
                   jax.ShapeDtypeStruct((B, NSA_KV_GROUPS, S, KSEL_DIM), CDT),
                   jax.ShapeDtypeStruct((B, MLA_HEADS, S, MLA_QK_DIM), CDT)],
        **_opts("mixer_proj", "parallel", "parallel"),
    )(x, g_col, wmT, cos, sin, cosm, sinm, qn, wqbT, kvn, wkvbT)


def _compress_kernel(kr_ref, pe_ref, w1_ref, w2_ref, w2t_ref, tok_ref, fm_ref, hid_ref):
    kr = kr_ref[0, 0]
    nc, half = kr.shape
    nj = nc // 4
    a = _dot((kr + pe_ref[0, 0:1, :]).astype(CDT), w1_ref[0, 0:half, :])
    bm = _dot((kr + pe_ref[0, 1:2, :]).astype(CDT), w1_ref[0, half:2 * half, :])
    nxt = pltpu.roll(bm, nc - 1, 0)
    row = lax.broadcasted_iota(jnp.int32, (nc, 1), 0)
    pre = a + jnp.where(row == nc - 1, 0.0, nxt)
    hid_ref[...] = pre * jax.nn.sigmoid(pre)
    hid = jnp.concatenate([hid_ref[pl.ds(r, nj, stride=4), :] for r in range(4)], axis=0).astype(CDT)
    tok_ref[0, 0] = _dot(hid, w2_ref[0]).astype(tok_ref.dtype)
    fm_ref[0, 0] = _dot_nt(w2t_ref[0], hid).astype(fm_ref.dtype)


def _compress(kr, pe, w1, w2, w2t):
    B, _, NC, W = kr.shape
    return pl.pallas_call(
        _compress_kernel,
        grid=(B, 4),
        in_specs=[pl.BlockSpec((1, 1, NC, W), lambda b, n: (b, n, 0, 0)),
                  pl.BlockSpec((1, 2, W), lambda b, n: (n // 2, 0, 0)),
                  pl.BlockSpec((1, 2 * W, LANES), lambda b, n: (n // 2, 0, 0)),
                  pl.BlockSpec((1, LANES, HEAD_DIM), lambda b, n: (n // 2, 0, 0)),
                  pl.BlockSpec((1, HEAD_DIM, LANES), lambda b, n: (n // 2, 0, 0))],
        out_specs=[pl.BlockSpec((1, 1, NC, HEAD_DIM), lambda b, n: (b, n, 0, 0)),
                   pl.BlockSpec((1, 1, HEAD_DIM, NC), lambda b, n: (b, n, 0, 0))],
        out_shape=[jax.ShapeDtypeStruct((B, 4, NC, HEAD_DIM), CDT),
                   jax.ShapeDtypeStruct((B, 4, HEAD_DIM, NC), CDT)],
        scratch_shapes=[pltpu.VMEM((NC, LANES), jnp.float32)],
        **_opts("nsa_compress", "parallel", "parallel"),
    )(kr, pe, w1, w2, w2t)


def _group_queries(q_ref, g):
    return jnp.concatenate([q_ref[0, (g * GROUP_HEADS + r) * HEAD_DIM:(g * GROUP_HEADS + r + 1) * HEAD_DIM, :]
                            for r in range(GROUP_HEADS)], axis=1)


def _with_ones(vt):
    return jnp.concatenate([vt, jnp.ones((ONES_ROWS, vt.shape[1]), vt.dtype)], axis=0)


def _softmax_probs(sc, m_run):
    m_new = jnp.maximum(m_run, jnp.max(sc, axis=0, keepdims=True))
    return m_new, jnp.exp2(sc - m_new).astype(CDT), jnp.exp2(m_run - m_new)


def _softmax_step(sc, m_run, acc, vt_aug):
    m_new, p, alpha = _softmax_probs(sc, m_run)
    return m_new, alpha * acc + _dot(vt_aug, p)


def _band_bias(t, k0, span, window):
    kpos = k0 + lax.broadcasted_iota(jnp.int32, (span, 1), 0)
    bias = jnp.where(kpos <= t, jnp.where(kpos > t - window, 0.0, -jnp.inf), -jnp.inf)
    return jnp.concatenate([bias] * GROUP_HEADS, axis=1)


def _window_scores(q4, band, k):
    return _dot(k, q4) + band


def _window_output(s, vt, sink4=None):
    m = jnp.max(s, axis=0, keepdims=True)
    if sink4 is not None:
        m = jnp.maximum(m, sink4)
    pv = _dot(_with_ones(vt), jnp.exp2(s - m).astype(CDT))
    den = pv[HEAD_DIM:HEAD_DIM + 1]
    if sink4 is not None:
        den = den + jnp.exp2(sink4 - m)
    return pv[0:HEAD_DIM] * (1.0 / den)


def _swa_scores(q_ref, k_ref, qs, tq):
    t = qs + lax.broadcasted_iota(jnp.int32, (1, tq), 1)
    span = SWA_WINDOW + tq
    k0 = pl.multiple_of(jnp.maximum(qs - SWA_WINDOW, 0), LANES)
    band = _band_bias(t, k0, span, SWA_WINDOW)
    return [_window_scores(_group_queries(q_ref, g), band, k_ref[0, g, pl.ds(k0, span), :])
            for g in range(SWA_KV_HEADS)], k0


def _swa_outputs(scores, k0, sink_ref, vt_ref, o_ref, tq):
    span = SWA_WINDOW + tq
    for g in range(SWA_KV_HEADS):
        sink4 = jnp.concatenate([jnp.full((1, tq), sink_ref[g * GROUP_HEADS + r] * LOG2E, jnp.float32)
                                 for r in range(GROUP_HEADS)], axis=1)
        o4 = _window_output(scores[g], vt_ref[0, g * HEAD_DIM:(g + 1) * HEAD_DIM, pl.ds(k0, span)], sink4)
        for r in range(GROUP_HEADS):
            row = (g * GROUP_HEADS + r) * HEAD_DIM
            o_ref[0, row:row + HEAD_DIM, :] = o4[:, r * tq:(r + 1) * tq].astype(o_ref.dtype)


def _nsa_kernel(q_ref, qr_ref, cend_ref, kc_ref, vct_ref, ks_ref, vst_ref, kw_ref, vwt_ref,
                gate_ref, o_ref, bias_ref, s_a, s_b, *, tq, n_sel):
    qs = pl.program_id(1) * tq
    nc = kc_ref.shape[2]
    nj = nc // 4
    G = NSA_KV_GROUPS
    t = qs + lax.broadcasted_iota(jnp.int32, (1, tq), 1)
    hpc = SEL_CHAIN_LANES // tq
    cpg = GROUP_HEADS // hpc
    tc = jnp.concatenate([t] * hpc, axis=1)
    t4 = jnp.concatenate([t] * GROUP_HEADS, axis=1)
    jrow = lax.broadcasted_iota(jnp.int32, (nj, tq), 0)
    jf = jrow.astype(jnp.float32)
    cur = lax.shift_right_arithmetic(t, int(np.log2(NSA_SEL_BLOCK)))
    bonus = jnp.where((jrow == 0) | (jrow == cur) | (jrow == cur - 1), NSA_FORCE_BONUS, 0.0)
    span = NSA_WINDOW + tq
    kw0 = pl.multiple_of(jnp.maximum(qs - NSA_WINDOW, 0), LANES)

    qr4 = [_group_queries(qr_ref, g) for g in range(G)]
    s_cmp = [_dot(kc_ref[0, g], _group_queries(q_ref, g)) for g in range(G)]
    band = _band_bias(t, kw0, span, NSA_WINDOW)
    s_win = [_window_scores(qr4[g], band, kw_ref[0, g, pl.ds(kw0, span), :]) for g in range(G)]
    p_cmp = []
    for g in range(G):
        s = jnp.where(cend_ref[...] <= t4, s_cmp[g], -jnp.inf)
        m = jnp.max(s, axis=0, keepdims=True)
        m = jnp.where(m == -jnp.inf, 0.0, m)
        e = jnp.exp2(s - m)
        den = jnp.sum(e, axis=0, keepdims=True)
        p_cmp.append(e * (1.0 / jnp.where(den > 0, den, 1.0)))
    o_cmps = [_dot(vct_ref[0, g], p_cmp[g].astype(CDT)) for g in range(G)]
    o_wins = [_window_output(s_win[g], vwt_ref[0, g * HEAD_DIM:(g + 1) * HEAD_DIM, pl.ds(kw0, span)])
              for g in range(G)]

    fixed = []
    for g in range(G):
        p, o_cmp, o_win = p_cmp[g], o_cmps[g], o_wins[g]
        psum = p[:, 0:tq]
        for r in range(1, GROUP_HEADS):
            psum = psum + p[:, r * tq:(r + 1) * tq]
        p3 = psum[3 * nj:4 * nj]
        prev = jnp.where(jrow == 0, 0.0, pltpu.roll(p3, 1, 0))
        imp = prev + 2.0 * (psum[0:nj] + psum[nj:2 * nj] + psum[2 * nj:3 * nj]) + p3

        score = jnp.where(jrow <= cur, imp + bonus, -jnp.inf)
        for _ in range(n_sel):
            mx = jnp.max(score, axis=0, keepdims=True)
            idx = jnp.min(jnp.where(score == mx, jf, float(nj)), axis=0, keepdims=True)
            idx = jnp.where(mx > -jnp.inf, idx, float(nj))
            score = jnp.where(jf == idx, -jnp.inf, score)
        bias_ref[g] = jnp.where(jrow <= cur, jnp.where(score == -jnp.inf, 0.0, MASK_VALUE), MASK_VALUE)

        parts = []
        for r in range(GROUP_HEADS):
            sl = slice(r * tq, (r + 1) * tq)
            g0 = gate_ref[0, g, 3 * r:3 * r + 1, :]
            g2 = gate_ref[0, g, 3 * r + 2:3 * r + 3, :]
            parts.append((g0 * o_cmp[:, sl], g2 * o_win[:, sl]))
        fixed.append(parts)

    def augmented_queries(kt):
        out = []
        for g in range(G):
            b8 = bias_ref[g, pl.ds(kt * SEL_BLOCKS_PER_TILE, SEL_BLOCKS_PER_TILE), :]
            b16 = jnp.concatenate([b8, jnp.zeros_like(b8)], axis=0)
            out.append(jnp.concatenate(
                [qr4[g], jnp.concatenate([b16] * GROUP_HEADS, axis=1).astype(CDT)], axis=0))
        return out

    def score_chain(kt, s_ref, qa, c):
        g, hp = divmod(c, cpg)
        k0 = pl.multiple_of(kt * TK_SEL, TK_SEL)
        s_ref[c] = _dot(ks_ref[0, g, pl.ds(k0, TK_SEL), :],
                        qa[g][:, hp * SEL_CHAIN_LANES:(hp + 1) * SEL_CHAIN_LANES])

    def consume_chain(kt, s_ref, c, st, diagonal):
        g = c // cpg
        k0 = pl.multiple_of(kt * TK_SEL, TK_SEL)
        sc = s_ref[c]
        if diagonal:
            kpos = k0 + lax.broadcasted_iota(jnp.int32, (TK_SEL, 1), 0)
            sc = jnp.where(kpos <= tc, sc, MASK_VALUE)
        vt_aug = _with_ones(vst_ref[0, g * HEAD_DIM:(g + 1) * HEAD_DIM, pl.ds(k0, TK_SEL)])
        return _softmax_step(sc, st[0], st[1], vt_aug)

    def step(kt, s_cur, s_nxt, state, diagonal=False):
        n_chain = cpg * G
        if s_nxt is not None:
            qa = augmented_queries(kt + 1)
            for c in range(QK_LEAD):
                score_chain(kt + 1, s_nxt, qa, c)
        out = []
        for c in range(n_chain):
            out.append(consume_chain(kt, s_cur, c, state[c], diagonal))
            if s_nxt is not None and c + QK_LEAD < n_chain:
                score_chain(kt + 1, s_nxt, qa, c + QK_LEAD)
        return tuple(out)

    def finish(state):
        for g in range(G):
            for r in range(GROUP_HEADS):
                _, acc = state[cpg * g + r // hpc]
                sl = slice((r % hpc) * tq, (r % hpc + 1) * tq)
                o_sel = acc[0:HEAD_DIM, sl] * (1.0 / acc[HEAD_DIM:HEAD_DIM + 1, sl])
                g1 = gate_ref[0, g, 3 * r + 1:3 * r + 2, :]
                c_part, w_part = fixed[g][r]
                row = (g * GROUP_HEADS + r) * HEAD_DIM
                o_ref[0, row:row + HEAD_DIM, :] = (c_part + g1 * o_sel + w_part).astype(o_ref.dtype)

    def pair(j, state):
        return step(2 * j + 1, s_b, s_a, step(2 * j, s_a, s_b, state))

    init = tuple((jnp.full((1, SEL_CHAIN_LANES), MASK_VALUE, jnp.float32),
                  jnp.zeros((HEAD_DIM + ONES_ROWS, SEL_CHAIN_LANES), jnp.float32))
                 for _ in range(cpg * G))
    n_full = qs // TK_SEL
    qa0 = augmented_queries(0)
    for c in range(cpg * G):
        score_chain(0, s_a, qa0, c)
    state = lax.fori_loop(0, n_full // 2, pair, init)
    odd = n_full % 2 == 1

    @pl.when(odd)
    def _():
        finish(step(n_full, s_b, None, step(n_full - 1, s_a, s_b, state), diagonal=True))

    @pl.when(jnp.logical_not(odd))
    def _():
        finish(step(n_full, s_a, None, state, diagonal=True))


def _nsa(o1, cend, cmp_tok, cmp_fm, ks_aug, kk, gates, tq):
    B, _, S = o1.shape
    nc = cmp_tok.shape[2]
    nj = nc // 4
    n_sel = min(NSA_N_SEL, nj)
    G = NSA_KV_GROUPS
    rows = NSA_HEADS * HEAD_DIM
    vrows = G * HEAD_DIM
    assert O_BQ % rows == 0 and O_BQR % rows == 0 and O_VV % vrows == 0
    assert SEL_CHAIN_LANES % tq == 0 and tq <= TK_SEL
    return pl.pallas_call(
        functools.partial(_nsa_kernel, tq=tq, n_sel=n_sel),
        grid=(B, S // tq),
        in_specs=[pl.BlockSpec((1, rows, tq), lambda b, i: (b, O_BQ // rows, i)),
                  pl.BlockSpec((1, rows, tq), lambda b, i: (b, O_BQR // rows, i)),
                  _const_spec((nc, 1)),
                  pl.BlockSpec((1, G, nc, HEAD_DIM), lambda b, i: (b, 0, 0, 0)),
                  pl.BlockSpec((1, G, HEAD_DIM, nc), lambda b, i: (b, 1, 0, 0)),
                  pl.BlockSpec((1, G, S, KSEL_DIM), lambda b, i: (b, 0, 0, 0)),
                  pl.BlockSpec((1, vrows, S), lambda b, i: (b, O_VV // vrows + 1, 0)),
                  pl.BlockSpec((1, G, S, HEAD_DIM), lambda b, i: (b, 1, 0, 0)),
                  pl.BlockSpec((1, vrows, S), lambda b, i: (b, O_VV // vrows + 2, 0)),
                  pl.BlockSpec((1, G, 3 * GROUP_HEADS, tq), lambda b, i: (b, 0, 0, i))],
        out_specs=pl.BlockSpec((1, rows, tq), lambda b, i: (b, 0, i)),
        out_shape=jax.ShapeDtypeStruct((B, rows, S), CDT),
        scratch_shapes=[pltpu.VMEM((G, nj, tq), jnp.float32)]
                       + [pltpu.VMEM((NSA_HEADS * tq // SEL_CHAIN_LANES, TK_SEL, SEL_CHAIN_LANES),
                                     jnp.float32)] * 2,
        **_opts("nsa_attn", "parallel", "arbitrary"),
    )(o1, o1, cend, cmp_tok, cmp_fm, ks_aug, o1, kk, o1, gates)


def _mla_swa_kernel(sink_ref, q_ref, k_ref, *rest, tq, hps):
    vt_refs = rest[:hps]
    aq_ref, ak_ref, avt_ref, o_ref, oa_ref, s_a, s_b = rest[hps:]
    qs = pl.program_id(1) * tq
    q = [q_ref[0, h * MLA_QK_DIM:(h + 1) * MLA_QK_DIM, :] for h in range(hps)]
    t = qs + lax.broadcasted_iota(jnp.int32, (1, tq), 1)

    def score_chain(kt, s_ref, h):
        k0 = pl.multiple_of(kt * TK_MLA, TK_MLA)
        s_ref[h] = _dot(k_ref[0, h, pl.ds(k0, TK_MLA), :], q[h])

    def consume_chain(kt, s_ref, h, st, diagonal):
        k0 = pl.multiple_of(kt * TK_MLA, TK_MLA)
        sc = s_ref[h]
        if diagonal:
            kpos = k0 + lax.broadcasted_iota(jnp.int32, (TK_MLA, 1), 0)
            sc = jnp.where(kpos <= t, sc, MASK_VALUE)
        return _softmax_step(sc, st[0], st[1], _with_ones(vt_refs[h][0, :, pl.ds(k0, TK_MLA)]))

    def step(kt, s_cur, s_nxt, state, diagonal=False):
        lead = QK_LEAD
        if s_nxt is not None:
            for h in range(lead):
                score_chain(kt + 1, s_nxt, h)
        out = []
        for h in range(hps):
            out.append(consume_chain(kt, s_cur, h, state[h], diagonal))
            if s_nxt is not None and h + lead < hps:
                score_chain(kt + 1, s_nxt, h + lead)
        return tuple(out)

    def finish(state):
        for h in range(hps):
            _, acc = state[h]
            o_ref[0, h * MLA_V_DIM:(h + 1) * MLA_V_DIM, :] = (
                acc[0:MLA_V_DIM] * (1.0 / acc[MLA_V_DIM:MLA_V_DIM + 1])).astype(o_ref.dtype)

    def pair(j, state):
        return step(2 * j + 1, s_b, s_a, step(2 * j, s_a, s_b, state))

    init = tuple((jnp.full((1, tq), MASK_VALUE, jnp.float32),
                  jnp.zeros((MLA_V_DIM + ONES_ROWS, tq), jnp.float32)) for _ in range(hps))
    n_full = qs // TK_MLA
    swa_scores, swa_k0 = _swa_scores(aq_ref, ak_ref, qs, tq)
    for h in range(hps):
        score_chain(0, s_a, h)
    _swa_outputs(swa_scores, swa_k0, sink_ref, avt_ref, oa_ref, tq)
    state = lax.fori_loop(0, n_full // 2, pair, init)
    odd = n_full % 2 == 1

    @pl.when(odd)
    def _():
        finish(step(n_full, s_b, None, step(n_full - 1, s_a, s_b, state), diagonal=True))

    @pl.when(jnp.logical_not(odd))
    def _():
        finish(step(n_full, s_a, None, state, diagonal=True))


def _mla_swa(sinks, o1, kcat, kk, tq):
    B, _, S = o1.shape
    hps = MLA_HEADS
    assert O_QM % N_QM == 0 and O_KV % MLA_V_DIM == 0 and O_AQ % N_Q == 0 and O_VV % N_KV == 0
    once = dict(pipeline_mode=pl.Buffered(1))
    v_spec = lambda h: pl.BlockSpec(
        (1, MLA_V_DIM, S), lambda b, i: (b, O_KV // MLA_V_DIM + 2 * h + 1, 0), **once)
    return pl.pallas_call(
        functools.partial(_mla_swa_kernel, tq=tq, hps=hps),
        grid=(B, S // tq),
        in_specs=[pl.BlockSpec(memory_space=pltpu.SMEM),
                  pl.BlockSpec((1, N_QM, tq), lambda b, i: (b, O_QM // N_QM, i)),
                  pl.BlockSpec((1, hps, S, MLA_QK_DIM), lambda b, i: (b, 0, 0, 0), **once)]
                 + [v_spec(h) for h in range(hps)]
                 + [pl.BlockSpec((1, N_Q, tq), lambda b, i: (b, O_AQ // N_Q, i)),
                    pl.BlockSpec((1, SWA_KV_HEADS, S, HEAD_DIM), lambda b, i: (b, 0, 0, 0), **once),
                    pl.BlockSpec((1, N_KV, S), lambda b, i: (b, O_VV // N_KV, 0), **once)],
        out_specs=[pl.BlockSpec((1, MLA_HEADS * MLA_V_DIM, tq), lambda b, i: (b, 0, i)),
                   pl.BlockSpec((1, N_Q, tq), lambda b, i: (b, 0, i))],
        out_shape=[jax.ShapeDtypeStruct((B, MLA_HEADS * MLA_V_DIM, S), CDT),
                   jax.ShapeDtypeStruct((B, N_Q, S), CDT)],
        scratch_shapes=[pltpu.VMEM((hps, TK_MLA, tq), jnp.float32)] * 2,
        **_opts("mla_swa_attn", "parallel", "parallel"),
    )(sinks, o1, kcat, *([o1] * hps), o1, kk, o1)


def _merge_kernel(x_ref, g_ref, wg_ref, oa_ref, ob_ref, oc_ref, wa_ref, wb_ref, wc_ref, wo_ref, o_ref,
                  *, token_major):
    x = _load_residual(x_ref, token_major)
    d = x.shape[0]
    h = _rms_fm(x, g_ref[...]).astype(CDT)
    merged = None
    for i, (br_ref, w_ref) in enumerate(((oa_ref, wa_ref), (ob_ref, wb_ref), (oc_ref, wc_ref))):
        gate = jax.nn.sigmoid(_dot(wg_ref[i * d:(i + 1) * d, :], h))
        term = gate * _dot(w_ref[...], br_ref[0])
        merged = term if merged is None else merged + term
    o_ref[0] = x + _dot(wo_ref[...], merged.astype(CDT))


def _merge(x, g_col, wgT, oa, ob, oc, waT, wbT, wcT, woT, tm, token_major):
    B, D, S = (x.shape[0], x.shape[2], x.shape[1]) if token_major else x.shape
    tok = lambda rows: pl.BlockSpec((1, rows, tm), lambda b, i: (b, 0, i))
    return pl.pallas_call(
        functools.partial(_merge_kernel, token_major=token_major),
        grid=(B, S // tm),
        in_specs=[_residual_spec(D, tm, token_major), _const_spec((D, 1)), _const_spec(wgT.shape),
                  tok(oa.shape[1]), tok(ob.shape[1]), tok(oc.shape[1]),
                  _const_spec(waT.shape), _const_spec(wbT.shape), _const_spec(wcT.shape),
                  _const_spec(woT.shape)],
        out_specs=tok(D),
        out_shape=jax.ShapeDtypeStruct((B, D, S), x.dtype),
        **_opts("merge_out", "parallel", "parallel"),
    )(x, g_col, wgT, oa, ob, oc, waT, wbT, wcT, woT)


def _mem_kv_kernel(mem_ref, g_ref, wk_ref, wvt_ref, k_ref, vt_ref):
    m = mem_ref[0]
    ms = jnp.mean(m * m, axis=-1, keepdims=True)
    hm = (m * lax.rsqrt(ms + NORM_EPS) * g_ref[...]).astype(CDT)
    k_ref[0] = _dot(hm, wk_ref[...]).astype(k_ref.dtype)
    vt_ref[0] = _dot_nt(wvt_ref[...], hm).astype(vt_ref.dtype)


def _mem_kv(mem, g_row, wk, wvT):
    B, M, D = mem.shape
    n = wk.shape[1]
    return pl.pallas_call(
        _mem_kv_kernel,
        grid=(B,),
        in_specs=[pl.BlockSpec((1, M, D), lambda b: (b, 0, 0)), _const_spec((1, D)),
                  _const_spec(wk.shape), _const_spec(wvT.shape)],
        out_specs=[pl.BlockSpec((1, M, n), lambda b: (b, 0, 0)),
                   pl.BlockSpec((1, n, M), lambda b: (b, 0, 0))],
        out_shape=[jax.ShapeDtypeStruct((B, M, n), CDT), jax.ShapeDtypeStruct((B, n, M), CDT)],
        **_opts("mem_kv", "parallel"),
    )(mem, g_row, wk, wvT)


def _xattn_kernel(x_ref, g_ref, wq_ref, k_ref, vt_ref, wo_ref, o_ref):
    x = x_ref[0]
    h = _rms_fm(x, g_ref[...]).astype(CDT)
    q = (_dot(wq_ref[...], h) * (XATTN_HEAD_DIM ** -0.5 * LOG2E)).astype(CDT)
    head_rows = [slice(hd * XATTN_HEAD_DIM, (hd + 1) * XATTN_HEAD_DIM) for hd in range(XATTN_HEADS)]
    scores = [_dot(k_ref[0, :, rows], q[rows]) for rows in head_rows]
    probs = [jnp.exp2(s - jnp.max(s, axis=0, keepdims=True)).astype(CDT) for s in scores]
    outs = []
    for rows, e in zip(head_rows, probs):
        pv = _dot(_with_ones(vt_ref[0, rows, :]), e)
        outs.append((pv[0:XATTN_HEAD_DIM] * (1.0 / pv[XATTN_HEAD_DIM:XATTN_HEAD_DIM + 1])).astype(CDT))
    o_ref[0] = x + _dot(wo_ref[...], jnp.concatenate(outs, axis=0))


def _xattn(xT, g_col, wqT, kmem, vmemT, woT, tm):
    B, D, S = xT.shape
    M, n = kmem.shape[1], kmem.shape[2]
    tok = lambda rows: pl.BlockSpec((1, rows, tm), lambda b, i: (b, 0, i))
    return pl.pallas_call(
        _xattn_kernel,
        grid=(B, S // tm),
        in_specs=[tok(D), _const_spec((D, 1)), _const_spec(wqT.shape),
                  pl.BlockSpec((1, M, n), lambda b, i: (b, 0, 0)),
                  pl.BlockSpec((1, n, M), lambda b, i: (b, 0, 0)),
                  _const_spec(woT.shape)],
        out_specs=tok(D),
        out_shape=jax.ShapeDtypeStruct((B, D, S), xT.dtype),
        **_opts("xattn", "parallel", "parallel"),
    )(xT, g_col, wqT, kmem, vmemT, woT)


def _ffn_kernel(x_ref, g_ref, wgu_ref, wd_ref, *rest, d_ff, final_norm):
    o_ref = rest[-1]
    x = x_ref[0]
    h = _rms_fm(x, g_ref[...]).astype(CDT)
    acc = x
    for c in range(d_ff // FF_CHUNK):
        r = c * FF_CHUNK
        gate = _dot(wgu_ref[r:r + FF_CHUNK, :], h)
        up = _dot(wgu_ref[d_ff + r:d_ff + r + FF_CHUNK, :], h)
        act = (gate * jax.nn.sigmoid(gate) * up).astype(CDT)
        acc = acc + _dot(wd_ref[:, r:r + FF_CHUNK], act)
    if final_norm:
        o_ref[0] = _rms_fm(acc, rest[0][...]).T
    else:
        o_ref[0] = acc


def _ffn(xT, g_col, wguT, wdT, tm, final_g_col=None):
    B, D, S = xT.shape
    d_ff = wdT.shape[1]
    final_norm = final_g_col is not None
    tok = lambda rows: pl.BlockSpec((1, rows, tm), lambda b, i: (b, 0, i))
    extra = [final_g_col] if final_norm else []
    return pl.pallas_call(
        functools.partial(_ffn_kernel, d_ff=d_ff, final_norm=final_norm),
        grid=(B, S // tm),
        in_specs=[tok(D), _const_spec((D, 1)), _const_spec(wguT.shape), _const_spec(wdT.shape)]
                 + [_const_spec((D, 1))] * len(extra),
        out_specs=_residual_spec(D, tm, final_norm),
        out_shape=jax.ShapeDtypeStruct((B, S, D) if final_norm else (B, D, S), xT.dtype),
        **_opts("ffn", "parallel", "parallel"),
    )(xT, g_col, wguT, wdT, *extra)


def _rope_tables(positions, dim):
    half = dim // 2
    inv_freq = ROPE_THETA ** (-jnp.arange(half, dtype=jnp.float32) / half)
    ang = positions.astype(jnp.float32)[:, None, :] * inv_freq[None, :, None]
    return jnp.cos(ang), jnp.sin(ang)


def _pack_mixer_weight(w_in):
    off = _offsets(N_Q, N_KV, N_KV, N_Q, N_KV, N_KV, N_KV, N_KV, N_KV, N_KV, N_GATE,
                   MLA_Q_RANK, MLA_KV_RANK, MLA_ROPE_DIM)
    a_q, a_k, a_v, b_q, b_kc, b_vc, b_ks, b_vs, b_kw, b_vw, b_g, c_qa, c_kv, c_kr = [
        w_in[:, off[i]:off[i + 1]] for i in range(14)]
    packed = jnp.concatenate([a_q, b_q, a_k, b_ks, b_kw, a_v, b_vs, b_vw, b_kc, b_vc,
                              c_qa, c_kv, c_kr, b_g], axis=1)
    packed = jnp.pad(packed, ((0, 0), (0, N_MIX - packed.shape[1])))
    return packed.T.astype(CDT), w_in[:, off[14]:].T.astype(CDT)


def kernel(x, mem, positions, norm_mix, w_in, swa_sinks, nsa_pe_k, nsa_pe_v, nsa_wk1, nsa_wk2,
           nsa_wv1, nsa_wv2, mla_q_norm, mla_w_q_b, mla_kv_norm, mla_w_kv_b, w_br_a, w_br_b,
           w_br_c, w_out, norm_xattn, norm_mem, w_xq, w_xkv, w_xo, norm_ffn, w_gate_up, w_down,
           norm_final):
    B, S, D = x.shape
    depth = w_in.shape[0]
    nj = S // NSA_SEL_BLOCK
    nc = 4 * nj
    assert S % TK_SEL == 0 and S % TM_PROJ == 0 and S >= NSA_WINDOW + TQ_NSA
    col = lambda v: v.reshape(-1, 1)
    wt = lambda w: w.T.astype(CDT)

    cos, sin = _rope_tables(positions, HEAD_DIM)
    cosm, sinm = _rope_tables(positions, MLA_ROPE_DIM)
    rr, jj = np.divmod(np.arange(nc), nj)
    cend = jnp.asarray(((4 * jj + rr) * NSA_CMP_STRIDE + NSA_CMP_BLOCK - 1).reshape(nc, 1), jnp.int32)

    res = x
    for l in range(depth):
        token_major = l == 0
        wmT, wgT = _pack_mixer_weight(w_in[l])
        o1, gates, kcv, kk, ks_aug, kcat = _proj(
            res, col(norm_mix[l]), wmT, cos, sin, cosm, sinm,
            col(mla_q_norm[l]), wt(mla_w_q_b[l]), col(mla_kv_norm[l]), wt(mla_w_kv_b[l]), TM_PROJ,
            token_major)
        kr = kcv.reshape(B, N_CMP // HEAD_DIM, nc, NSA_CMP_STRIDE * HEAD_DIM)
        gates = gates.reshape(B, NSA_KV_GROUPS, 3 * GROUP_HEADS, S)

        pe = jnp.stack([nsa_pe_k[l], nsa_pe_v[l]]).reshape(2, 2, NSA_CMP_STRIDE * HEAD_DIM)
        hpad = LANES - HEAD_DIM
        w1 = jnp.pad(jnp.stack([nsa_wk1[l], nsa_wv1[l]]), ((0, 0), (0, 0), (0, hpad))).astype(CDT)
        w2 = jnp.pad(jnp.stack([nsa_wk2[l], nsa_wv2[l]]), ((0, 0), (0, hpad), (0, 0))).astype(CDT)
        w2t = jnp.swapaxes(w2, 1, 2)
        cmp_tok, cmp_fm = _compress(kr, pe, w1, w2, w2t)

        o_b = _nsa(o1, cend, cmp_tok, cmp_fm, ks_aug, kk, gates, TQ_NSA)
        o_c, o_a = _mla_swa(swa_sinks[l], o1, kcat, kk, TQ_MLA)
        xT = _merge(res, col(norm_mix[l]), wgT, o_a, o_b, o_c,
                    wt(w_br_a[l]), wt(w_br_b[l]), wt(w_br_c[l]), wt(w_out[l]), TM_PROJ, token_major)

        n_kv = XATTN_HEADS * XATTN_HEAD_DIM
        kmem, vmemT = _mem_kv(mem, norm_mem[l].reshape(1, D), w_xkv[l][:, :n_kv].astype(CDT),
                              wt(w_xkv[l][:, n_kv:]))
        xT = _xattn(xT, col(norm_xattn[l]), wt(w_xq[l]), kmem, vmemT, wt(w_xo[l]), TM_PROJ)
        res = _ffn(xT, col(norm_ffn[l]), wt(w_gate_up[l]), wt(w_down[l]), TM_PROJ,
                   col(norm_final) if l == depth - 1 else None)
    return res
```
